```python
import jax
import jax.numpy as jnp
from jax import lax
import numpy as np

D_MODEL = 1024
BATCH = 8
SEQ = 4096
DEPTH = 4

HEAD_DIM = 64
SB_HEADS = 8
SB_BLOCK = 128
NSA_HEADS = 8
NSA_KV_GROUPS = 2
NSA_GROUP = NSA_HEADS // NSA_KV_GROUPS
CMP_BLOCK = 32
CMP_HIDDEN = 128
SEL_BLOCK = 64
N_SEL = 16
WINDOW = 512
NSA_BLOCK = 64
N_EXPERTS = 32
TOP_K = 4
D_FF = 1024
SWIGLU_ALPHA = 1.702
SWIGLU_LIMIT = 7.0
PLE_DIM = 256
LN_EPS = 1e-5
DN_ALPHA = (2 * DEPTH) ** 0.25
DN_BETA = (8 * DEPTH) ** -0.25
NEG_INF = -1e30
FORCED_SCORE = 1e6

SB_W = SB_HEADS * HEAD_DIM
NSA_QW = NSA_HEADS * HEAD_DIM
NSA_KVW = NSA_KV_GROUPS * HEAD_DIM
NSA_GATE_W = 3 * NSA_HEADS
IN_SPLITS = (SB_W, SB_W, SB_W, NSA_QW) + (NSA_KVW,) * 6 + (NSA_GATE_W, D_MODEL, D_MODEL)
N_IN = 3 * SB_W + NSA_QW + 6 * NSA_KVW + NSA_GATE_W + 2 * D_MODEL

kernel_name = "hybrid_stickbreak_nsa_moe_deepnorm"


def layer_norm(x, g, b):
    xf = x.astype(jnp.float32)
    mu = jnp.mean(xf, axis=-1, keepdims=True)
    var = jnp.mean(jnp.square(xf - mu), axis=-1, keepdims=True)
    y = (xf - mu) * lax.rsqrt(var + LN_EPS)
    return (y * g.astype(jnp.float32) + b.astype(jnp.float32)).astype(x.dtype)


def split_columns(t):
    parts, start = [], 0
    for width in IN_SPLITS:
        parts.append(t[..., start:start + width])
        start += width
    return parts


def to_heads(t, n):
    B, S, _ = t.shape
    return t.reshape(B, S, n, HEAD_DIM).transpose(0, 2, 1, 3).astype(jnp.float32)


def alibi_slopes():
    return jnp.exp2(-8.0 * jnp.arange(1, NSA_HEADS + 1, dtype=jnp.float32) / NSA_HEADS)


def stick_breaking_attention(q, k, v):
    B, H, S, Dh = q.shape
    scale = Dh ** -0.5
    kpos = jnp.arange(S)

    def block(jb):
        qs = jb * SB_BLOCK
        qb = lax.dynamic_slice_in_dim(q, qs, SB_BLOCK, axis=2)
        tq = qs + jnp.arange(SB_BLOCK)
        mask = kpos[None, :] < tq[:, None]
        z = jnp.einsum('bhqd,bhkd->bhqk', qb, k) * scale
        log_stay = jnp.where(mask, jax.nn.log_sigmoid(-z), 0.0)
        later = lax.cumsum(log_stay, axis=3, reverse=True) - log_stay
        w = jnp.where(mask, jnp.exp(jax.nn.log_sigmoid(z) + later), 0.0)
        return jnp.einsum('bhqk,bhkd->bhqd', w, v)

    o = lax.map(block, jnp.arange(S // SB_BLOCK))
    return jnp.moveaxis(o, 0, 2).reshape(B, H, S, Dh)


def compress_tokens(kv, w1, w2, pe):
    B, G, S, Dh = kv.shape
    blk = kv.reshape(B, G, S // CMP_BLOCK, CMP_BLOCK, Dh) + pe
    hid = jax.nn.silu(jnp.einsum('bgcld,ldh->bgch', blk, w1))
    return jnp.einsum('bgch,hd->bgcd', hid, w2)


def gather_blocks(blocks, idx):
    return jax.vmap(jax.vmap(lambda blk, ix: blk[ix]))(blocks, idx)


def native_sparse_attention(q, k_c, v_c, k_s, v_s, k_w, v_w, gate_logits, w_cmp1, w_cmp2, pe_cmp):
    B, G, R, S, Dh = q.shape
    scale = Dh ** -0.5
    n_cmp = S // CMP_BLOCK
    n_blk = S // SEL_BLOCK
    n_sel = min(N_SEL, n_blk)
    slopes = alibi_slopes().reshape(G, R)[None, :, :, None, None]
    pos = jnp.arange(S)

    kc = compress_tokens(k_c, w_cmp1[0], w_cmp2[0], pe_cmp[0])
    vc = compress_tokens(v_c, w_cmp1[1], w_cmp2[1], pe_cmp[1])
    c_end = jnp.arange(n_cmp) * CMP_BLOCK + (CMP_BLOCK - 1)
    dist_c = pos[:, None] - c_end[None, :]
    mask_c = dist_c >= 0
    s_c = jnp.einsum('bgrtd,bgcd->bgrtc', q, kc) * scale - slopes * dist_c.astype(jnp.float32)
    p_c = jnp.where(mask_c, jax.nn.softmax(jnp.where(mask_c, s_c, NEG_INF), axis=-1), 0.0)
    o_cmp = jnp.einsum('bgrtc,bgcd->bgrtd', p_c, vc)

    imp = p_c.sum(axis=2).reshape(B, G, S, n_blk, SEL_BLOCK // CMP_BLOCK).sum(-1)
    blk = jnp.arange(n_blk)[None, :]
    cur = (pos // SEL_BLOCK)[:, None]
    forced = (blk == 0) | (blk == cur) | (blk == cur - 1)
    valid = blk * SEL_BLOCK <= pos[:, None]
    imp = jnp.where(forced, FORCED_SCORE, jnp.where(valid, imp, NEG_INF))
    _, sel_idx = lax.top_k(imp, n_sel)
    ks_blocks = k_s.reshape(B, G, n_blk, SEL_BLOCK, Dh)
    vs_blocks = v_s.reshape(B, G, n_blk, SEL_BLOCK, Dh)
    kw_pad = jnp.pad(k_w, ((0, 0), (0, 0), (WINDOW, 0), (0, 0)))
    vw_pad = jnp.pad(v_w, ((0, 0), (0, 0), (WINDOW, 0), (0, 0)))
    n_keys = n_sel * SEL_BLOCK

    def block(jb):
        qs = jb * NSA_BLOCK
        qb = lax.dynamic_slice_in_dim(q, qs, NSA_BLOCK, axis=3)
        tq = qs + jnp.arange(NSA_BLOCK)
        idx = lax.dynamic_slice_in_dim(sel_idx, qs, NSA_BLOCK, axis=2)
        kg = gather_blocks(ks_blocks, idx).reshape(B, G, NSA_BLOCK, n_keys, Dh)
        vg = gather_blocks(vs_blocks, idx).reshape(B, G, NSA_BLOCK, n_keys, Dh)
        kpos = (idx[..., None] * SEL_BLOCK + jnp.arange(SEL_BLOCK)).reshape(B, G, NSA_BLOCK, n_keys)
        dist_s = (tq[:, None] - kpos)[:, :, None]
        s_s = jnp.einsum('bgrqd,bgqmd->bgrqm', qb, kg) * scale - slopes * dist_s.astype(jnp.float32)
        p_s = jax.nn.softmax(jnp.where(dist_s >= 0, s_s, NEG_INF), axis=-1)
        o_s = jnp.einsum('bgrqm,bgqmd->bgrqd', p_s, vg)
        kwb = lax.dynamic_slice_in_dim(kw_pad, qs, NSA_BLOCK + WINDOW, axis=2)
        vwb = lax.dynamic_slice_in_dim(vw_pad, qs, NSA_BLOCK + WINDOW, axis=2)
        wpos = qs - WINDOW + jnp.arange(NSA_BLOCK + WINDOW)
        dist_w = tq[:, None] - wpos[None, :]
        mask_w = (dist_w >= 0) & (dist_w < WINDOW) & (wpos[None, :] >= 0)
        s_w = jnp.einsum('bgrqd,bgkd->bgrqk', qb, kwb) * scale - slopes * dist_w.astype(jnp.float32)
        p_w = jax.nn.softmax(jnp.where(mask_w, s_w, NEG_INF), axis=-1)
        o_w = jnp.einsum('bgrqk,bgkd->bgrqd', p_w, vwb)
        return o_s, o_w

    o_sel, o_win = lax.map(block, jnp.arange(S // NSA_BLOCK))
    o_sel = jnp.moveaxis(o_sel, 0, 3).reshape(B, G, R, S, Dh)
    o_win = jnp.moveaxis(o_win, 0, 3).reshape(B, G, R, S, Dh)
    g = jax.nn.sigmoid(gate_logits)
    return g[..., 0:1] * o_cmp + g[..., 1:2] * o_sel + g[..., 2:3] * o_win


def moe_ffn(h, w_router, b_router, w_gu, b_gu, w_down, b_down):
    B, S, D = h.shape
    t = h.reshape(B * S, D)
    logits = (t @ w_router + b_router).astype(jnp.float32)
    top_val, top_idx = lax.top_k(logits, TOP_K)
    top_w = jax.nn.softmax(top_val, axis=-1)
    combine = jnp.sum(jax.nn.one_hot(top_idx, N_EXPERTS, dtype=jnp.float32) * top_w[..., None], axis=1)

    def expert(acc, params):
        wgu, bgu, wd, bd, c = params
        gu = t @ wgu + bgu
        gate = jnp.minimum(gu[:, :D_FF], SWIGLU_LIMIT)
        up = jnp.clip(gu[:, D_FF:], -SWIGLU_LIMIT, SWIGLU_LIMIT)
        act = (up + 1.0) * gate * jax.nn.sigmoid(SWIGLU_ALPHA * gate)
        return acc + c[:, None] * (act @ wd + bd), None

    acc, _ = lax.scan(expert, jnp.zeros_like(t), (w_gu, b_gu, w_down, b_down, combine.T.astype(t.dtype)))
    return acc.reshape(B, S, D)


def decoder_layer(h, p_i, w_in, w_cmp1, w_cmp2, pe_cmp, w_br_sb, w_br_nsa, w_o,
                  ln1_g, ln1_b, ln2_g, ln2_b, w_router, b_router, w_gu, b_gu,
                  w_down, b_down, w_ple_gate, w_ple_proj):
    B, S, _ = h.shape
    f32 = jnp.float32
    (sb_q, sb_k, sb_v, nsa_q, k_cmp, v_cmp, k_sel, v_sel, k_win, v_win,
     nsa_gate, g_sb, g_nsa) = split_columns(h @ w_in)

    y_sb = stick_breaking_attention(to_heads(sb_q, SB_HEADS), to_heads(sb_k, SB_HEADS), to_heads(sb_v, SB_HEADS))
    y_sb = y_sb.transpose(0, 2, 1, 3).reshape(B, S, SB_W).astype(h.dtype)

    q_grp = nsa_q.reshape(B, S, NSA_KV_GROUPS, NSA_GROUP, HEAD_DIM).transpose(0, 2, 3, 1, 4).astype(f32)
    gate_grp = nsa_gate.reshape(B, S, NSA_KV_GROUPS, NSA_GROUP, 3).transpose(0, 2, 3, 1, 4).astype(f32)
    y_nsa = native_sparse_attention(
        q_grp,
        to_heads(k_cmp, NSA_KV_GROUPS), to_heads(v_cmp, NSA_KV_GROUPS),
        to_heads(k_sel, NSA_KV_GROUPS), to_heads(v_sel, NSA_KV_GROUPS),
        to_heads(k_win, NSA_KV_GROUPS), to_heads(v_win, NSA_KV_GROUPS),
        gate_grp, w_cmp1.astype(f32), w_cmp2.astype(f32), pe_cmp.astype(f32))
    y_nsa = y_nsa.transpose(0, 3, 1, 2, 4).reshape(B, S, NSA_QW).astype(h.dtype)

    merged = jax.nn.sigmoid(g_sb) * (y_sb @ w_br_sb) + jax.nn.sigmoid(g_nsa) * (y_nsa @ w_br_nsa)
    h1 = layer_norm(DN_ALPHA * h + merged @ w_o, ln1_g, ln1_b)

    ple = jax.nn.sigmoid(h1 @ w_ple_gate) * (p_i @ w_ple_proj)
    moe_out = moe_ffn(h1, w_router, b_router, w_gu, b_gu, w_down, b_down)
    return layer_norm(DN_ALPHA * h1 + moe_out + ple, ln2_g, ln2_b)


def setup_inputs(seed: int = 0) -> dict:
    key = jax.random.key(seed)
    ks = jax.random.split(key, 21)
    f32 = jnp.float32
    L = DEPTH

    def normal(k, shape, std):
        return jax.random.normal(k, shape, f32) * std

    in_scale = jnp.concatenate(
        [jnp.ones((2 * SB_W,), f32), jnp.full((SB_W,), DN_BETA, f32), jnp.ones((NSA_QW,), f32)]
        + [jnp.ones((NSA_KVW,), f32), jnp.full((NSA_KVW,), DN_BETA, f32)] * 3
        + [jnp.ones((NSA_GATE_W + 2 * D_MODEL,), f32)])
    return {
        "x": normal(ks[0], (BATCH, SEQ, D_MODEL), 1.0),
        "p": normal(ks[1], (DEPTH, BATCH, SEQ, PLE_DIM), 1.0),
        "w_in": normal(ks[2], (L, D_MODEL, N_IN), D_MODEL ** -0.5) * in_scale,
        "w_cmp1": normal(ks[3], (L, 2, CMP_BLOCK, HEAD_DIM, CMP_HIDDEN), (CMP_BLOCK * HEAD_DIM) ** -0.5),
        "w_cmp2": normal(ks[4], (L, 2, CMP_HIDDEN, HEAD_DIM), CMP_HIDDEN ** -0.5),
        "pe_cmp": normal(ks[5], (L, 2, CMP_BLOCK, HEAD_DIM), 0.1),
        "w_br_sb": normal(ks[6], (L, SB_W, D_MODEL), DN_BETA * SB_W ** -0.5),
        "w_br_nsa": normal(ks[7], (L, NSA_QW, D_MODEL), DN_BETA * NSA_QW ** -0.5),
        "w_o": normal(ks[8], (L, D_MODEL, D_MODEL), DN_BETA * D_MODEL ** -0.5),
        "ln1_g": 1.0 + normal(ks[9], (L, D_MODEL), 0.02),
        "ln1_b": normal(ks[10], (L, D_MODEL), 0.02),
        "ln2_g": 1.0 + normal(ks[11], (L, D_MODEL), 0.02),
        "ln2_b": normal(ks[12], (L, D_MODEL), 0.02),
        "w_router": normal(ks[13], (L, D_MODEL, N_EXPERTS), D_MODEL ** -0.5),
        "b_router": normal(ks[14], (L, N_EXPERTS), 0.01),
        "w_gu": normal(ks[15], (L, N_EXPERTS, D_MODEL, 2 * D_FF), DN_BETA * D_MODEL ** -0.5),
        "b_gu": normal(ks[16], (L, N_EXPERTS, 2 * D_FF), 0.01),
        "w_down": normal(ks[17], (L, N_EXPERTS, D_FF, D_MODEL), DN_BETA * D_FF ** -0.5),
        "b_down": normal(ks[18], (L, N_EXPERTS, D_MODEL), 0.01),
        "w_ple_gate": normal(ks[19], (L, D_MODEL, D_MODEL), D_MODEL ** -0.5),
        "w_ple_proj": normal(ks[20], (L, PLE_DIM, D_MODEL), DN_BETA * PLE_DIM ** -0.5),
    }


def reference(x, p, w_in, w_cmp1, w_cmp2, pe_cmp, w_br_sb, w_br_nsa, w_o,
              ln1_g, ln1_b, ln2_g, ln2_b, w_router, b_router, w_gu, b_gu,
              w_down, b_down, w_ple_gate, w_ple_proj):
    h = x
    for i in range(DEPTH):
        h = decoder_layer(h, p[i], w_in[i], w_cmp1[i], w_cmp2[i], pe_cmp[i],
                          w_br_sb[i], w_br_nsa[i], w_o[i],
                          ln1_g[i], ln1_b[i], ln2_g[i], ln2_b[i],
                          w_router[i], b_router[i], w_gu[i], b_gu[i],
                          w_down[i], b_down[i], w_ple_gate[i], w_ple_proj[i])
    return h
```

```python
import functools

import jax
import jax.numpy as jnp
from jax import lax
from jax.experimental import pallas as pl
from jax.experimental.pallas import tpu as pltpu

F32 = jnp.float32
BF16 = jnp.bfloat16

HEAD_DIM = 64
LANES = 128
SB_HEADS = 8
NSA_HEADS = 8
NSA_KV_GROUPS = 2
NSA_GROUP = NSA_HEADS // NSA_KV_GROUPS
CMP_BLOCK = 32
CMP_HIDDEN = 128
SEL_BLOCK = 64
N_SEL = 16
WINDOW = 512
N_EXPERTS = 32
TOP_K = 4
D_FF = 1024
SWIGLU_ALPHA = 1.702
SWIGLU_LIMIT = 7.0
LN_EPS = 1e-5
NEG_INF = -1e30
FORCED_SCORE = 1e6
GATE_PAD = 128

VMEM_LIMIT = 56 * 1024 * 1024

ATTN_TQ = 128
SEL_TK = 512
PROJ_TM = 256
MERGE_TM = 256
ROUTER_TM = 512
EXPERT_TM = 512
OUT_TM = 256


def _cparams(sem):
    return pltpu.CompilerParams(dimension_semantics=sem, vmem_limit_bytes=VMEM_LIMIT)


def _sigmoid(x):
    return 1.0 / (1.0 + jnp.exp(-x))


def _div_pow2(x, n):
    assert n & (n - 1) == 0
    return lax.shift_right_logical(x, n.bit_length() - 1)


def _mod_pow2(x, n):
    assert n & (n - 1) == 0
    return x & (n - 1)


def _split_bf16(x):
    hi = x.astype(BF16)
    lo = (x - hi.astype(F32)).astype(BF16)
    return hi, lo


def _dot(a, b):
    return jnp.dot(a, b, preferred_element_type=F32)


def _dot_nt(a, b):
    return lax.dot_general(a, b, (((1,), (1,)), ((), ())), preferred_element_type=F32)


def _dot_tn(a, b):
    return lax.dot_general(a, b, (((0,), (0,)), ((), ())), preferred_element_type=F32)


def _layer_norm(u, g, b):
    mu = jnp.mean(u, axis=-1, keepdims=True)
    d = u - mu
    var = jnp.mean(d * d, axis=-1, keepdims=True)
    return d * lax.rsqrt(var + LN_EPS) * g + b


def _in_proj_kernel(x_ref, *refs, scales):
    n = len(scales)
    x = x_ref[...].astype(BF16)
    for w_ref, o_ref, s in zip(refs[:n], refs[n:], scales):
        acc = _dot(x, w_ref[...])
        if s != 1.0:
            acc = acc * s
        o_ref[...] = acc.astype(o_ref.dtype)


def _in_proj(h, weights, dtypes, scales):
    t, d = h.shape
    tm = PROJ_TM
    in_specs = [pl.BlockSpec((tm, d), lambda i: (i, 0))]
    in_specs += [pl.BlockSpec(w.shape, lambda i: (0, 0)) for w in weights]
    out_specs = [pl.BlockSpec((tm, w.shape[1]), lambda i: (i, 0)) for w in weights]
    out_shape = [jax.ShapeDtypeStruct((t, w.shape[1]), dt) for w, dt in zip(weights, dtypes)]
    return pl.pallas_call(
        functools.partial(_in_proj_kernel, scales=tuple(scales)),
        grid=(t // tm,), in_specs=in_specs, out_specs=out_specs, out_shape=out_shape,
        compiler_params=_cparams(("parallel",)), name="in_proj",
    )(h, *weights)


def _stack_heads(q2, tq):
    lane = lax.broadcasted_iota(jnp.int32, (tq, LANES), 1)
    zero = jnp.zeros_like(q2)
    return jnp.concatenate([jnp.where(lane < HEAD_DIM, q2, zero),
                            jnp.where(lane >= HEAD_DIM, q2, zero)], axis=0)


def _unstack_heads(o, tq):
    lane = lax.broadcasted_iota(jnp.int32, (tq, LANES), 1)
    return jnp.where(lane < HEAD_DIM, o[:tq], o[tq:])


def _sb_kernel(q_ref, k_ref, v_ref, o_ref, *, tq):
    i = pl.program_id(2)
    qs = _stack_heads(q_ref[...], tq)
    r_i = lax.broadcasted_iota(jnp.int32, (tq, tq), 0)
    c_i = lax.broadcasted_iota(jnp.int32, (tq, tq), 1)
    tri = jnp.where(r_i > c_i, 1.0, 0.0).astype(BF16)
    tri2 = jnp.concatenate([tri, tri], axis=0)
    r2 = lax.broadcasted_iota(jnp.int32, (2 * tq, tq), 0) & (tq - 1)
    causal = lax.broadcasted_iota(jnp.int32, (2 * tq, tq), 1) < r2

    def tile(j, carry, acc, diag):
        start = pl.multiple_of(j * tq, tq)
        kj = k_ref[pl.ds(start, tq), :]
        vj = v_ref[pl.ds(start, tq), :]
        z = _dot_nt(qs, kj)
        ls = -(jnp.maximum(z, 0.0) + jnp.log(1.0 + jnp.exp(-jnp.abs(z))))
        if diag:
            ls = jnp.where(causal, ls, 0.0)
        hi, lo = _split_bf16(ls)
        later = _dot(jnp.concatenate([hi, lo], axis=1), tri2)
        w = jnp.exp(z + ls + later + carry)
        if diag:
            w = jnp.where(causal, w, 0.0)
        acc = acc + _dot(w.astype(BF16), vj)
        carry = carry + jnp.sum(ls, axis=1, keepdims=True)
        return carry, acc

    carry0 = jnp.zeros((2 * tq, 1), F32)
    acc0 = jnp.zeros((2 * tq, LANES), F32)
    carry, acc = tile(i, carry0, acc0, True)

    def body(step, c):
        return tile(i - step, c[0], c[1], False)

    carry, acc = lax.fori_loop(1, i + 1, body, (carry, acc))
    o_ref[...] = _unstack_heads(acc, tq).astype(o_ref.dtype)


def _sb_attention(q, k, v):
    b, s, w = q.shape
    tq = ATTN_TQ
    return pl.pallas_call(
        functools.partial(_sb_kernel, tq=tq),
        grid=(b, w // LANES, s // tq),
        in_specs=[pl.BlockSpec((None, tq, LANES), lambda bi, hp, i: (bi, i, hp)),
                  pl.BlockSpec((None, s, LANES), lambda bi, hp, i: (bi, 0, hp)),
                  pl.BlockSpec((None, s, LANES), lambda bi, hp, i: (bi, 0, hp))],
        out_specs=pl.BlockSpec((None, tq, LANES), lambda bi, hp, i: (bi, i, hp)),
        out_shape=jax.ShapeDtypeStruct((b, s, w), BF16),
        compiler_params=_cparams(("parallel", "parallel", "arbitrary")), name="sb_attn",
    )(q, k, v)


def _compress_kernel(x_ref, pe_ref, w1_ref, w2_ref, o_ref):
    x = (x_ref[...] + pe_ref[...]).astype(BF16)
    hid = _dot(x, w1_ref[...])
    hid = hid * _sigmoid(hid)
    o_ref[...] = _dot(hid.astype(BF16), w2_ref[...]).astype(o_ref.dtype)


def _compress(x, pe, w1, w2):
    b, half, width = x.shape
    blk = width // 2
    out = pl.pallas_call(
        _compress_kernel,
        grid=(b, 2),
        in_specs=[pl.BlockSpec((None, half, blk), lambda bi, par: (bi, 0, par)),
                  pl.BlockSpec(pe.shape, lambda bi, par: (0, 0)),
                  pl.BlockSpec(w1.shape, lambda bi, par: (0, 0)),
                  pl.BlockSpec(w2.shape, lambda bi, par: (0, 0))],
        out_specs=pl.BlockSpec((None, None, half, w2.shape[1]), lambda bi, par: (bi, par, 0, 0)),
        out_shape=jax.ShapeDtypeStruct((b, 2, half, w2.shape[1]), BF16),
        compiler_params=_cparams(("parallel", "parallel")), name="compress",
    )(x, pe, w1, w2)
    return out.reshape(b, 2 * half, w2.shape[1])


def _stack_group(q, tq):
    return jnp.concatenate([_stack_heads(q[:, :LANES], tq), _stack_heads(q[:, LANES:], tq)], axis=0)


def _unstack_group(o, tq):
    return jnp.concatenate([_unstack_heads(o[:2 * tq], tq), _unstack_heads(o[2 * tq:], tq)], axis=1)


def _alibi_slope(head_idx):
    return lax.bitcast_convert_type(lax.shift_left(126 - head_idx, 23), F32)


def _expand_gate(gate_sig, g, branch):
    c_i = lax.broadcasted_iota(jnp.int32, (GATE_PAD, 2 * LANES), 0)
    l_i = lax.broadcasted_iota(jnp.int32, (GATE_PAD, 2 * LANES), 1)
    col = g * (3 * NSA_GROUP) + _div_pow2(l_i, HEAD_DIM) * 3 + branch
    onehot = jnp.where(c_i == col, 1.0, 0.0).astype(BF16)
    hi, lo = _split_bf16(gate_sig)
    return _dot(hi, onehot) + _dot(lo, onehot)


def _nsa_cmp_kernel(q_ref, kc_ref, vc_ref, gate_ref, y_ref, sel_ref, *, tq, n_sel):
    g = pl.program_id(1)
    i = pl.program_id(2)
    nc = kc_ref.shape[0]
    half = nc // 2
    qst = _stack_group(q_ref[...], tq)
    s = _dot_nt(kc_ref[...], qst)
    r_i = lax.broadcasted_iota(jnp.int32, (nc, 4 * tq), 0)
    l_i = lax.broadcasted_iota(jnp.int32, (nc, 4 * tq), 1)
    cblk = 2 * _mod_pow2(r_i, half) + _div_pow2(r_i, half)
    c_end = cblk * CMP_BLOCK + (CMP_BLOCK - 1)
    t = i * tq + _mod_pow2(l_i, tq)
    dist = t - c_end
    slope = _alibi_slope(g * NSA_GROUP + _div_pow2(l_i, tq))
    s = s - slope * dist.astype(F32)
    mask = dist >= 0
    m = jnp.max(jnp.where(mask, s, NEG_INF), axis=0, keepdims=True)
    p = jnp.where(mask, jnp.exp(s - m), 0.0)
    l = jnp.sum(p, axis=0, keepdims=True)
    p = p * jnp.where(l > 0.0, 1.0 / l, 0.0)

    o = _dot_tn(p.astype(BF16), vc_ref[...])
    gate = _expand_gate(_sigmoid(gate_ref[...]), g, 0)
    y_ref[...] = (_unstack_group(o, tq) * gate).astype(y_ref.dtype)

    imp_c = p[:, 0:tq] + p[:, tq:2 * tq] + p[:, 2 * tq:3 * tq] + p[:, 3 * tq:4 * tq]
    imp = imp_c[:half] + imp_c[half:]
    n_blk = half
    blk = lax.broadcasted_iota(jnp.int32, (n_blk, tq), 0)
    tt = i * tq + lax.broadcasted_iota(jnp.int32, (n_blk, tq), 1)
    cur = _div_pow2(tt, SEL_BLOCK)
    score = jnp.where(blk * SEL_BLOCK <= tt, imp, NEG_INF)
    for forced_blk in (0, cur, cur - 1):
        score = jnp.where(blk == forced_blk, FORCED_SCORE, score)

    taken = -jnp.inf
    sel = jnp.zeros((n_blk, tq), F32)
    for _ in range(n_sel):
        mx = jnp.max(score, axis=0, keepdims=True)
        first = jnp.min(jnp.where(score == mx, blk, n_blk), axis=0, keepdims=True)
        hit = blk == first
        sel = jnp.where(hit, 1.0, sel)
        score = jnp.where(hit, taken, score)
    pad = jnp.zeros((LANES - n_blk, tq), F32)
    sel_ref[...] = jnp.concatenate([sel, pad], axis=0).T


def _nsa_cmp(q, kc, vc, gate):
    b, s, _ = q.shape
    tq = ATTN_TQ
    nc = kc.shape[1]
    n_blk = nc // 2
    assert n_blk <= LANES
    n_sel = min(N_SEL, n_blk)
    return pl.pallas_call(
        functools.partial(_nsa_cmp_kernel, tq=tq, n_sel=n_sel),
        grid=(b, NSA_KV_GROUPS, s // tq),
        in_specs=[pl.BlockSpec((None, tq, 2 * LANES), lambda bi, g, i: (bi, i, g)),
                  pl.BlockSpec((None, nc, LANES), lambda bi, g, i: (bi, 0, g)),
                  pl.BlockSpec((None, nc, LANES), lambda bi, g, i: (bi, 0, g)),
                  pl.BlockSpec((None, tq, GATE_PAD), lambda bi, g, i: (bi, i, 0))],
        out_specs=[pl.BlockSpec((None, tq, 2 * LANES), lambda bi, g, i: (bi, i, g)),
                   pl.BlockSpec((None, None, tq, LANES), lambda bi, g, i: (bi, g, i, 0))],
        out_shape=[jax.ShapeDtypeStruct((b, s, 4 * LANES), F32),
                   jax.ShapeDtypeStruct((b, NSA_KV_GROUPS, s, LANES), F32)],
        compiler_params=_cparams(("parallel", "parallel", "parallel")), name="nsa_cmp",
    )(q, kc, vc, gate)


def _nsa_main_kernel(q_ref, ks_ref, vs_ref, kw_ref, vw_ref, sel_ref, gate_ref, ycmp_ref, o_ref,
                     *, tq, tk):
    g = pl.program_id(1)
    i = pl.program_id(2)
    rows = 4 * tq
    qst = _stack_group(q_ref[...], tq)
    row = lax.broadcasted_iota(jnp.int32, (rows, 1), 0)
    t_col = i * tq + _mod_pow2(row, tq)
    slope = _alibi_slope(g * NSA_GROUP + _div_pow2(row, tq))
    sel = sel_ref[...].astype(BF16)

    blk_of_key = _div_pow2(lax.broadcasted_iota(jnp.int32, (LANES, tk), 1), SEL_BLOCK)
    blk_row = lax.broadcasted_iota(jnp.int32, (LANES, tk), 0)
    rel = blk_row - blk_of_key
    lane_k = lax.broadcasted_iota(jnp.int32, (1, tk), 1)

    def sel_tile(kt, c):
        m, l, acc = c
        start = pl.multiple_of(kt * tk, tk)
        k = ks_ref[pl.ds(start, tk), :]
        v = vs_ref[pl.ds(start, tk), :]
        dist = t_col - (start + lane_k)
        s = _dot_nt(qst, k) - slope * dist.astype(F32)
        expand = jnp.where(rel == kt * (tk // SEL_BLOCK), 1.0, 0.0).astype(BF16)
        chosen = _dot(sel, expand)
        chosen = jnp.concatenate([chosen] * NSA_GROUP, axis=0)
        ok = jnp.where(dist >= 0, chosen, 0.0) > 0.5
        m_new = jnp.maximum(m, jnp.max(jnp.where(ok, s, NEG_INF), axis=1, keepdims=True))
        alpha = jnp.exp(m - m_new)
        p = jnp.where(ok, jnp.exp(s - m_new), 0.0)
        l = alpha * l + jnp.sum(p, axis=1, keepdims=True)
        acc = alpha * acc + _dot(p.astype(BF16), v)
        return m_new, l, acc

    n_tiles = (i * tq + tq + tk - 1) // tk
    init = (jnp.full((rows, 1), NEG_INF, F32), jnp.zeros((rows, 1), F32), jnp.zeros((rows, LANES), F32))
    _, l_s, acc_s = lax.fori_loop(0, n_tiles, sel_tile, init)
    o_sel = acc_s * (1.0 / l_s)

    wk = WINDOW + tq
    start = pl.multiple_of(jnp.maximum(i * tq - WINDOW, 0), tq)
    k = kw_ref[pl.ds(start, wk), :]
    v = vw_ref[pl.ds(start, wk), :]
    dist = t_col - (start + lax.broadcasted_iota(jnp.int32, (1, wk), 1))
    s = _dot_nt(qst, k) - slope * dist.astype(F32)
    ok = jnp.where(dist >= 0, dist, WINDOW) < WINDOW
    m = jnp.max(jnp.where(ok, s, NEG_INF), axis=1, keepdims=True)
    p = jnp.where(ok, jnp.exp(s - m), 0.0)
    l_w = jnp.sum(p, axis=1, keepdims=True)
    o_win = _dot(p.astype(BF16), v) * (1.0 / l_w)

    gate = _sigmoid(gate_ref[...])
    y = (ycmp_ref[...] + _expand_gate(gate, g, 1) * _unstack_group(o_sel, tq)
         + _expand_gate(gate, g, 2) * _unstack_group(o_win, tq))
    o_ref[...] = y.astype(o_ref.dtype)


def _nsa_main(q, kvsw, sel, gate, ycmp):
    b, s, _ = q.shape
    tq = ATTN_TQ
    tk = min(SEL_TK, s)
    assert s >= WINDOW + tq and s % tk == 0
    kv_spec = lambda off: pl.BlockSpec((None, s, LANES), lambda bi, g, i: (bi, 0, off + g))
    return pl.pallas_call(
        functools.partial(_nsa_main_kernel, tq=tq, tk=tk),
        grid=(b, NSA_KV_GROUPS, s // tq),
        in_specs=[pl.BlockSpec((None, tq, 2 * LANES), lambda bi, g, i: (bi, i, g)),
                  kv_spec(0), kv_spec(2), kv_spec(4), kv_spec(6),
                  pl.BlockSpec((None, None, tq, LANES), lambda bi, g, i: (bi, g, i, 0)),
                  pl.BlockSpec((None, tq, GATE_PAD), lambda bi, g, i: (bi, i, 0)),
                  pl.BlockSpec((None, tq, 2 * LANES), lambda bi, g, i: (bi, i, g))],
        out_specs=pl.BlockSpec((None, tq, 2 * LANES), lambda bi, g, i: (bi, i, g)),
        out_shape=jax.ShapeDtypeStruct((b, s, 4 * LANES), BF16),
        compiler_params=_cparams(("parallel", "parallel", "arbitrary")), name="nsa_main",
    )(q, kvsw, kvsw, kvsw, kvsw, sel, gate, ycmp)


def _merge_kernel(ysb_ref, ynsa_ref, g_ref, h_ref, wsb_ref, wnsa_ref, wo_ref, lng_ref, lnb_ref,
                  h1_ref, h1b_ref, *, alpha):
    d = h_ref.shape[1]
    g = g_ref[...]
    merged = (_sigmoid(g[:, :d]) * _dot(ysb_ref[...], wsb_ref[...])
              + _sigmoid(g[:, d:]) * _dot(ynsa_ref[...], wnsa_ref[...]))
    u = alpha * h_ref[...] + _dot(merged.astype(BF16), wo_ref[...])
    h1 = _layer_norm(u, lng_ref[...], lnb_ref[...])
    h1_ref[...] = h1
    h1b_ref[...] = h1.astype(BF16)


def _merge_ln1(ysb, ynsa, g, h, wsb, wnsa, wo, lng, lnb, alpha):
    t, d = h.shape
    tm = MERGE_TM
    row = lambda w: pl.BlockSpec((tm, w), lambda i: (i, 0))
    full = lambda a: pl.BlockSpec(a.shape, lambda i: (0, 0))
    return pl.pallas_call(
        functools.partial(_merge_kernel, alpha=alpha),
        grid=(t // tm,),
        in_specs=[row(ysb.shape[1]), row(ynsa.shape[1]), row(2 * d), row(d),
                  full(wsb), full(wnsa), full(wo), full(lng), full(lnb)],
        out_specs=[row(d), row(d)],
        out_shape=[jax.ShapeDtypeStruct((t, d), F32), jax.ShapeDtypeStruct((t, d), BF16)],
        compiler_params=_cparams(("parallel",)), name="merge_ln1",
    )(ysb, ynsa, g, h, wsb, wnsa, wo, lng, lnb)


def _router_kernel(x_ref, w_ref, b_ref, idx_ref, wt_ref):
    logits = _dot_nt(w_ref[...], x_ref[...]) + b_ref[...]
    n_e, tm = logits.shape
    e_i = lax.broadcasted_iota(jnp.int32, (n_e, tm), 0)
    vals, idxs = [], []
    for _ in range(TOP_K):
        mx = jnp.max(logits, axis=0, keepdims=True)
        first = jnp.min(jnp.where(logits == mx, e_i, n_e), axis=0, keepdims=True)
        vals.append(mx)
        idxs.append(first)
        logits = jnp.where(e_i == first, -jnp.inf, logits)
    ex = [jnp.exp(v - vals[0]) for v in vals]
    inv = 1.0 / (ex[0] + ex[1] + ex[2] + ex[3])
    idx_ref[...] = jnp.concatenate(idxs, axis=0)
    wt_ref[...] = jnp.concatenate([e * inv for e in ex], axis=0)


def _router(h1b, w_t, b_col):
    t, d = h1b.shape
    tm = ROUTER_TM
    return pl.pallas_call(
        _router_kernel,
        grid=(t // tm,),
        in_specs=[pl.BlockSpec((tm, d), lambda i: (i, 0)),
                  pl.BlockSpec(w_t.shape, lambda i: (0, 0)),
                  pl.BlockSpec(b_col.shape, lambda i: (0, 0))],
        out_specs=[pl.BlockSpec((TOP_K, tm), lambda i: (0, i)),
                   pl.BlockSpec((TOP_K, tm), lambda i: (0, i))],
        out_shape=[jax.ShapeDtypeStruct((TOP_K, t), jnp.int32),
                   jax.ShapeDtypeStruct((TOP_K, t), F32)],
        compiler_params=_cparams(("parallel",)), name="router",
    )(h1b, w_t, b_col)


def _expert_kernel(te_ref, nv_ref, x_ref, wgu_ref, bgu_ref, wd_ref, bd_ref, cw_ref, y_ref):
    i = pl.program_id(0)

    @pl.when(i < nv_ref[0])
    def _():
        gu = _dot(x_ref[...], wgu_ref[...]) + bgu_ref[...]
        gate = jnp.minimum(gu[:, :D_FF], SWIGLU_LIMIT)
        up = jnp.clip(gu[:, D_FF:], -SWIGLU_LIMIT, SWIGLU_LIMIT)
        act = (up + 1.0) * gate * _sigmoid(SWIGLU_ALPHA * gate)
        y = _dot(act.astype(BF16), wd_ref[...]) + bd_ref[...]
        y_ref[...] = (cw_ref[...] * y).astype(y_ref.dtype)

    @pl.when(i >= nv_ref[0])
    def _():
        y_ref[...] = jnp.zeros_like(y_ref)


def _experts(tile_expert, n_valid, xs, wgu, bgu, wd, bd, cw):
    p, d = xs.shape
    tm = EXPERT_TM
    n_e, _, two_ff = wgu.shape
    grid_spec = pltpu.PrefetchScalarGridSpec(
        num_scalar_prefetch=2, grid=(p // tm,),
        in_specs=[pl.BlockSpec((tm, d), lambda i, te, nv: (i, 0)),
                  pl.BlockSpec((None, d, two_ff), lambda i, te, nv: (te[i], 0, 0)),
                  pl.BlockSpec((None, 1, two_ff), lambda i, te, nv: (te[i], 0, 0)),
                  pl.BlockSpec((None, two_ff // 2, d), lambda i, te, nv: (te[i], 0, 0)),
                  pl.BlockSpec((None, 1, d), lambda i, te, nv: (te[i], 0, 0)),
                  pl.BlockSpec((tm, 1), lambda i, te, nv: (i, 0))],
        out_specs=pl.BlockSpec((tm, d), lambda i, te, nv: (i, 0)))
    return pl.pallas_call(
        _expert_kernel, grid_spec=grid_spec,
        out_shape=jax.ShapeDtypeStruct((p, d), F32),
        compiler_params=_cparams(("arbitrary",)), name="experts",
    )(tile_expert, n_valid, xs, wgu, bgu, wd, bd, cw)


def _out_kernel(h1_ref, h1b_ref, yg_ref, p_ref, wpg_ref, wpp_ref, lng_ref, lnb_ref, o_ref, *, alpha):
    moe = yg_ref[0] + yg_ref[1] + yg_ref[2] + yg_ref[3]
    ple = _sigmoid(_dot(h1b_ref[...], wpg_ref[...])) * _dot(p_ref[...].astype(BF16), wpp_ref[...])
    u = alpha * h1_ref[...] + moe + ple
    o_ref[...] = _layer_norm(u, lng_ref[...], lnb_ref[...])


def _out_ln2(h1, h1b, yg, p, wpg, wpp, lng, lnb, alpha):
    t, d = h1.shape
    tm = OUT_TM
    row = lambda w: pl.BlockSpec((tm, w), lambda i: (i, 0))
    full = lambda a: pl.BlockSpec(a.shape, lambda i: (0, 0))
    return pl.pallas_call(
        functools.partial(_out_kernel, alpha=alpha),
        grid=(t // tm,),
        in_specs=[row(d), row(d), pl.BlockSpec((TOP_K, tm, d), lambda i: (0, i, 0)), row(p.shape[1]),
                  full(wpg), full(wpp), full(lng), full(lnb)],
        out_specs=row(d),
        out_shape=jax.ShapeDtypeStruct((t, d), F32),
        compiler_params=_cparams(("parallel",)), name="out_ln2",
    )(h1, h1b, yg, p, wpg, wpp, lng, lnb)


def _dup_groups(w):
    d = w.shape[0]
    w = w.reshape(d, NSA_KV_GROUPS, 1, HEAD_DIM)
    return jnp.broadcast_to(w, (d, NSA_KV_GROUPS, 2, HEAD_DIM)).reshape(d, NSA_KV_GROUPS * LANES)


def _in_proj_weights(w_in):
    sbw = SB_HEADS * HEAD_DIM
    qw = NSA_HEADS * HEAD_DIM
    kvw = NSA_KV_GROUPS * HEAD_DIM
    d = w_in.shape[0]
    widths = (sbw, sbw, sbw, qw) + (kvw,) * 6 + (3 * NSA_HEADS, d, d)
    parts, start = [], 0
    for wd in widths:
        parts.append(w_in[:, start:start + wd])
        start += wd
    (sb_q, sb_k, sb_v, nsa_q, k_cmp, v_cmp, k_sel, v_sel, k_win, v_win, gate, g_sb, g_nsa) = parts
    kvsw = jnp.concatenate([_dup_groups(k_sel), _dup_groups(v_sel), _dup_groups(k_win), _dup_groups(v_win)], axis=1)
    gate = jnp.pad(gate, ((0, 0), (0, GATE_PAD - gate.shape[1])))
    g_both = jnp.concatenate([g_sb, g_nsa], axis=1)
    ws = [sb_q, sb_k, sb_v, nsa_q, k_cmp, v_cmp, kvsw, gate, g_both]
    dtypes = [BF16, BF16, BF16, BF16, F32, F32, BF16, F32, F32]
    scale = HEAD_DIM ** -0.5
    scales = [scale, 1.0, 1.0, scale, 1.0, 1.0, 1.0, 1.0, 1.0]
    return [w.astype(BF16) for w in ws], dtypes, scales


def _compress_weights(w1, w2, pe):
    eye = jnp.eye(NSA_KV_GROUPS, dtype=w1.dtype)
    w1b = jnp.einsum('ldh,pg->lpdgh', w1, eye).reshape(CMP_BLOCK * NSA_KV_GROUPS * HEAD_DIM,
                                                       NSA_KV_GROUPS * CMP_HIDDEN)
    w2b = jnp.einsum('hd,pg,r->phgrd', w2, eye, jnp.ones((2,), w2.dtype)).reshape(
        NSA_KV_GROUPS * CMP_HIDDEN, NSA_KV_GROUPS * LANES)
    peb = jnp.broadcast_to(pe[:, None, :], (CMP_BLOCK, NSA_KV_GROUPS, HEAD_DIM)).reshape(1, -1)
    return peb.astype(F32), w1b.astype(BF16), w2b.astype(BF16)


def _moe_plan(idx_t, wt_t, tm):
    k, t = idx_t.shape
    n = k * t
    e_flat = idx_t.reshape(n)
    order = jnp.argsort(e_flat, stable=True).astype(jnp.int32)
    sorted_e = e_flat[order]
    counts = jnp.zeros((N_EXPERTS,), jnp.int32).at[e_flat].add(1)
    padded = ((counts + tm - 1) // tm) * tm
    pad_end = jnp.cumsum(padded)
    pad_off = pad_end - padded
    grp_off = jnp.cumsum(counts) - counts
    dest = pad_off[sorted_e] + (jnp.arange(n, dtype=jnp.int32) - grp_off[sorted_e])
    p_rows = n + N_EXPERTS * tm
    src_tok = jnp.zeros((p_rows,), jnp.int32).at[dest].set(order % t)
    cw = jnp.zeros((p_rows,), F32).at[dest].set(wt_t.reshape(n)[order])
    pos = jnp.zeros((n,), jnp.int32).at[order].set(dest)
    tile_start = jnp.arange(p_rows // tm, dtype=jnp.int32) * tm
    tile_expert = jnp.minimum(jnp.searchsorted(pad_end, tile_start, side='right'), N_EXPERTS - 1).astype(jnp.int32)
    n_valid = (pad_end[-1] // tm).astype(jnp.int32).reshape(1)
    return src_tok, cw.reshape(p_rows, 1), pos, tile_expert, n_valid


def _layer(h, p_i, w_in, w_cmp1, w_cmp2, pe_cmp, w_br_sb, w_br_nsa, w_o, ln1_g, ln1_b, ln2_g, ln2_b,
           w_router, b_router, w_gu, b_gu, w_down, b_down, w_ple_gate, w_ple_proj, *, batch, alpha):
    t, d = h.shape
    s = t // batch
    ws, dtypes, scales = _in_proj_weights(w_in)
    sb_q, sb_k, sb_v, nsa_q, k_cmp, v_cmp, kvsw, gate, g_both = _in_proj(h, ws, dtypes, scales)
    b3 = lambda a: a.reshape(batch, s, a.shape[1])

    y_sb = _sb_attention(b3(sb_q), b3(sb_k), b3(sb_v)).reshape(t, -1)

    nc = s // CMP_BLOCK
    cmp_rows = lambda a: a.reshape(batch, nc // 2, 2 * CMP_BLOCK * a.shape[1])
    kc = _compress(cmp_rows(k_cmp), *_compress_weights(w_cmp1[0], w_cmp2[0], pe_cmp[0]))
    vc = _compress(cmp_rows(v_cmp), *_compress_weights(w_cmp1[1], w_cmp2[1], pe_cmp[1]))
    y_cmp, sel = _nsa_cmp(b3(nsa_q), kc, vc, b3(gate))
    y_nsa = _nsa_main(b3(nsa_q), b3(kvsw), sel, b3(gate), y_cmp).reshape(t, -1)

    row = lambda v: v.reshape(1, -1).astype(F32)
    h1, h1b = _merge_ln1(y_sb, y_nsa, g_both, h, w_br_sb.astype(BF16), w_br_nsa.astype(BF16),
                         w_o.astype(BF16), row(ln1_g), row(ln1_b), alpha)

    idx_t, wt_t = _router(h1b, w_router.T.astype(BF16), b_router.reshape(-1, 1).astype(F32))
    src_tok, cw, pos, tile_expert, n_valid = _moe_plan(idx_t, wt_t, EXPERT_TM)
    xs = jnp.take(h1b, src_tok, axis=0)
    y = _experts(tile_expert, n_valid, xs, w_gu.astype(BF16), b_gu.reshape(N_EXPERTS, 1, -1),
                 w_down.astype(BF16), b_down.reshape(N_EXPERTS, 1, -1), cw)
    yg = jnp.take(y, pos, axis=0).reshape(TOP_K, t, d)
    return _out_ln2(h1, h1b, yg, p_i, w_ple_gate.astype(BF16), w_ple_proj.astype(BF16),
                    row(ln2_g), row(ln2_b), alpha)


def kernel(x, p, w_in, w_cmp1, w_cmp2, pe_cmp, w_br_sb, w_br_nsa, w_o, ln1_g, ln1_b, ln2_g, ln2_b,
           w_router, b_router, w_gu, b_gu, w_down, b_down, w_ple_gate, w_ple_proj):
    batch, s, d = x.shape
    depth = w_in.shape[0]
    alpha = (2 * depth) ** 0.25
    h = x.reshape(batch * s, d)
    for i in range(depth):
        h = _layer(h, p[i].reshape(batch * s, -1), w_in[i], w_cmp1[i], w_cmp2[i], pe_cmp[i],
                   w_br_sb[i], w_br_nsa[i], w_o[i], ln1_g[i], ln1_b[i], ln2_g[i], ln2_b[i],
                   w_router[i], b_router[i], w_gu[i], b_gu[i], w_down[i], b_down[i],
                   w_ple_gate[i], w_ple_proj[i], batch=batch, alpha=alpha)
    return h.reshape(batch, s, d)
```

```python
import functools

import jax
import jax.numpy as jnp
from jax import lax
from jax.experimental import pallas as pl
from jax.experimental.pallas import tpu as pltpu

F32 = jnp.float32
BF16 = jnp.bfloat16

HEAD_DIM = 64
LANES = 128
SB_HEADS = 8
NSA_HEADS = 8
NSA_KV_GROUPS = 2
NSA_GROUP = NSA_HEADS // NSA_KV_GROUPS
CMP_BLOCK = 32
CMP_HIDDEN = 128
SEL_BLOCK = 64
N_SEL = 16
WINDOW = 512
N_EXPERTS = 32
TOP_K = 4
D_FF = 1024
SWIGLU_ALPHA = 1.702
SWIGLU_LIMIT = 7.0
LN_EPS = 1e-5
NEG_INF = -1e30
FORCED_SCORE = 1e6
GATE_PAD = 128
SB_DEAD_LOG = -104.0

VMEM_LIMIT = 56 * 1024 * 1024

ATTN_TQ = 128
SEL_TK = 512
PROJ_TM = 256
MERGE_TM = 256
ROUTER_TM = 512
EXPERT_TM = 512
OUT_TM = 256


def _cparams(sem):
    return pltpu.CompilerParams(dimension_semantics=sem, vmem_limit_bytes=VMEM_LIMIT)


def _sigmoid(x):
    return 1.0 / (1.0 + jnp.exp(-x))


def _div_pow2(x, n):
    assert n & (n - 1) == 0
    return lax.shift_right_logical(x, n.bit_length() - 1)


def _mod_pow2(x, n):
    assert n & (n - 1) == 0
    return x & (n - 1)


def _split_bf16(x):
    hi = x.astype(BF16)
    lo = (x - hi.astype(F32)).astype(BF16)
    return hi, lo


def _dot(a, b):
    return jnp.dot(a, b, preferred_element_type=F32)


def _dot_nt(a, b):
    return lax.dot_general(a, b, (((1,), (1,)), ((), ())), preferred_element_type=F32)


def _dot_tn(a, b):
    return lax.dot_general(a, b, (((0,), (0,)), ((), ())), preferred_element_type=F32)


def _layer_norm(u, g, b):
    mu = jnp.mean(u, axis=-1, keepdims=True)
    d = u - mu
    var = jnp.mean(d * d, axis=-1, keepdims=True)
    return d * lax.rsqrt(var + LN_EPS) * g + b


def _in_proj_kernel(x_ref, *refs, scales):
    n = len(scales)
    x = x_ref[...].astype(BF16)
    for w_ref, o_ref, s in zip(refs[:n], refs[n:], scales):
        acc = _dot(x, w_ref[...])
        if s != 1.0:
            acc = acc * s
        o_ref[...] = acc.astype(o_ref.dtype)


def _in_proj(h, weights, dtypes, scales):
    t, d = h.shape
    tm = PROJ_TM
    in_specs = [pl.BlockSpec((tm, d), lambda i: (i, 0))]
    in_specs += [pl.BlockSpec(w.shape, lambda i: (0, 0)) for w in weights]
    out_specs = [pl.BlockSpec((tm, w.shape[1]), lambda i: (i, 0)) for w in weights]
    out_shape = [jax.ShapeDtypeStruct((t, w.shape[1]), dt) for w, dt in zip(weights, dtypes)]
    return pl.pallas_call(
        functools.partial(_in_proj_kernel, scales=tuple(scales)),
        grid=(t // tm,), in_specs=in_specs, out_specs=out_specs, out_shape=out_shape,
        compiler_params=_cparams(("parallel",)), name="in_proj",
    )(h, *weights)


def _stack_heads(q2, tq):
    lane = lax.broadcasted_iota(jnp.int32, (tq, LANES), 1)
    zero = jnp.zeros_like(q2)
    return jnp.concatenate([jnp.where(lane < HEAD_DIM, q2, zero),
                            jnp.where(lane >= HEAD_DIM, q2, zero)], axis=0)


def _unstack_heads(o, tq):
    lane = lax.broadcasted_iota(jnp.int32, (tq, LANES), 1)
    return jnp.where(lane < HEAD_DIM, o[:tq], o[tq:])


def _sb_kernel(q_ref, k_ref, v_ref, o_ref, *, tq, n_pairs):
    i = pl.program_id(1)
    r_i = lax.broadcasted_iota(jnp.int32, (tq, tq), 0)
    c_i = lax.broadcasted_iota(jnp.int32, (tq, tq), 1)
    tri = jnp.where(r_i > c_i, 1.0, 0.0).astype(BF16)
    rhs = jnp.concatenate([tri, jnp.ones((tq, tq), BF16)], axis=1)
    rhs = jnp.concatenate([rhs, rhs], axis=0)
    r2 = lax.broadcasted_iota(jnp.int32, (2 * tq, tq), 0) & (tq - 1)
    causal = lax.broadcasted_iota(jnp.int32, (2 * tq, tq), 1) < r2
    qs = [_stack_heads(q_ref[:, p * LANES:(p + 1) * LANES], tq) for p in range(n_pairs)]

    def tile(p, j, carry, acc, diag):
        start = pl.multiple_of(j * tq, tq)
        kj = k_ref[pl.ds(start, tq), p * LANES:(p + 1) * LANES]
        vj = v_ref[pl.ds(start, tq), p * LANES:(p + 1) * LANES]
        z = _dot_nt(qs[p], kj)
        ls = -(jnp.maximum(z, 0.0) + jnp.log(1.0 + jnp.exp(-jnp.abs(z))))
        if diag:
            ls = jnp.where(causal, ls, 0.0)
        hi, lo = _split_bf16(ls)
        sums = _dot(jnp.concatenate([hi, lo], axis=1), rhs)
        w = jnp.exp(z + ls + sums[:, :tq] + carry)
        if diag:
            w = jnp.where(causal, w, 0.0)
        return carry + sums[:, tq:], acc + _dot(w.astype(BF16), vj)

    def live(carries):
        worst = functools.reduce(jnp.maximum, carries)
        return jnp.max(worst) > SB_DEAD_LOG

    zeros = jnp.zeros((2 * tq, LANES), F32)
    state = [tile(p, i, zeros, zeros, True) for p in range(n_pairs)]
    carries = tuple(s[0] for s in state)
    accs = tuple(s[1] for s in state)

    def cond(c):
        return jnp.logical_and(c[0] <= i, c[1])

    def body(c):
        step, _, carries, accs = c
        state = [tile(p, i - step, carries[p], accs[p], False) for p in range(n_pairs)]
        carries = tuple(s[0] for s in state)
        return step + 1, live(carries), carries, tuple(s[1] for s in state)

    _, _, _, accs = lax.while_loop(cond, body, (jnp.int32(1), live(carries), carries, accs))
    for p in range(n_pairs):
        o_ref[:, p * LANES:(p + 1) * LANES] = _unstack_heads(accs[p], tq).astype(o_ref.dtype)


def _sb_attention(q, k, v):
    b, s, w = q.shape
    tq = ATTN_TQ
    return pl.pallas_call(
        functools.partial(_sb_kernel, tq=tq, n_pairs=w // LANES),
        grid=(b, s // tq),
        in_specs=[pl.BlockSpec((None, tq, w), lambda bi, i: (bi, i, 0)),
                  pl.BlockSpec((None, s, w), lambda bi, i: (bi, 0, 0)),
                  pl.BlockSpec((None, s, w), lambda bi, i: (bi, 0, 0))],
        out_specs=pl.BlockSpec((None, tq, w), lambda bi, i: (bi, i, 0)),
        out_shape=jax.ShapeDtypeStruct((b, s, w), BF16),
        compiler_params=_cparams(("parallel", "arbitrary")), name="sb_attn",
    )(q, k, v)


def _compress_kernel(x_ref, pe_ref, w1_ref, w2_ref, o_ref):
    x = (x_ref[...] + pe_ref[...]).astype(BF16)
    hid = _dot(x, w1_ref[...])
    hid = hid * _sigmoid(hid)
    o_ref[...] = _dot(hid.astype(BF16), w2_ref[...]).astype(o_ref.dtype)


def _compress(x, pe, w1, w2):
    b, half, width = x.shape
    blk = width // 2
    out = pl.pallas_call(
        _compress_kernel,
        grid=(b, 2),
        in_specs=[pl.BlockSpec((None, half, blk), lambda bi, par: (bi, 0, par)),
                  pl.BlockSpec(pe.shape, lambda bi, par: (0, 0)),
                  pl.BlockSpec(w1.shape, lambda bi, par: (0, 0)),
                  pl.BlockSpec(w2.shape, lambda bi, par: (0, 0))],
        out_specs=pl.BlockSpec((None, None, half, w2.shape[1]), lambda bi, par: (bi, par, 0, 0)),
        out_shape=jax.ShapeDtypeStruct((b, 2, half, w2.shape[1]), BF16),
        compiler_params=_cparams(("parallel", "parallel")), name="compress",
    )(x, pe, w1, w2)
    return out.reshape(b, 2 * half, w2.shape[1])


def _stack_group(q, tq):
    return jnp.concatenate([_stack_heads(q[:, :LANES], tq), _stack_heads(q[:, LANES:], tq)], axis=0)


def _unstack_group(o, tq):
    return jnp.concatenate([_unstack_heads(o[:2 * tq], tq), _unstack_heads(o[2 * tq:], tq)], axis=1)


def _alibi_slope(head_idx):
    return lax.bitcast_convert_type(lax.shift_left(126 - head_idx, 23), F32)


def _expand_gate(gate_sig, g, branch):
    c_i = lax.broadcasted_iota(jnp.int32, (GATE_PAD, 2 * LANES), 0)
    l_i = lax.broadcasted_iota(jnp.int32, (GATE_PAD, 2 * LANES), 1)
    col = g * (3 * NSA_GROUP) + _div_pow2(l_i, HEAD_DIM) * 3 + branch
    onehot = jnp.where(c_i == col, 1.0, 0.0).astype(BF16)
    hi, lo = _split_bf16(gate_sig)
    return _dot(hi, onehot) + _dot(lo, onehot)


def _nsa_cmp_kernel(q_ref, kc_ref, vc_ref, gate_ref, y_ref, sel_ref, *, tq, n_sel):
    g = pl.program_id(1)
    i = pl.program_id(2)
    nc = kc_ref.shape[0]
    half = nc // 2
    qst = _stack_group(q_ref[...], tq)
    s = _dot_nt(kc_ref[...], qst)
    r_i = lax.broadcasted_iota(jnp.int32, (nc, 4 * tq), 0)
    l_i = lax.broadcasted_iota(jnp.int32, (nc, 4 * tq), 1)
    cblk = 2 * _mod_pow2(r_i, half) + _div_pow2(r_i, half)
    c_end = cblk * CMP_BLOCK + (CMP_BLOCK - 1)
    t = i * tq + _mod_pow2(l_i, tq)
    dist = t - c_end
    slope = _alibi_slope(g * NSA_GROUP + _div_pow2(l_i, tq))
    s = s - slope * dist.astype(F32)
    mask = dist >= 0
    m = jnp.max(jnp.where(mask, s, NEG_INF), axis=0, keepdims=True)
    p = jnp.where(mask, jnp.exp(s - m), 0.0)
    l = jnp.sum(p, axis=0, keepdims=True)
    p = p * jnp.where(l > 0.0, 1.0 / l, 0.0)

    o = _dot_tn(p.astype(BF16), vc_ref[...])
    gate = _expand_gate(_sigmoid(gate_ref[...]), g, 0)
    y_ref[...] = (_unstack_group(o, tq) * gate).astype(y_ref.dtype)

    imp_c = p[:, 0:tq] + p[:, tq:2 * tq] + p[:, 2 * tq:3 * tq] + p[:, 3 * tq:4 * tq]
    imp = imp_c[:half] + imp_c[half:]
    n_blk = half
    blk = lax.broadcasted_iota(jnp.int32, (n_blk, tq), 0)
    tt = i * tq + lax.broadcasted_iota(jnp.int32, (n_blk, tq), 1)
    cur = _div_pow2(tt, SEL_BLOCK)
    score = jnp.where(blk * SEL_BLOCK <= tt, imp, NEG_INF)
    for forced_blk in (0, cur, cur - 1):
        score = jnp.where(blk == forced_blk, FORCED_SCORE, score)

    taken = -jnp.inf
    sel = jnp.zeros((n_blk, tq), F32)
    for _ in range(n_sel):
        mx = jnp.max(score, axis=0, keepdims=True)
        first = jnp.min(jnp.where(score == mx, blk, n_blk), axis=0, keepdims=True)
        hit = blk == first
        sel = jnp.where(hit, 1.0, sel)
        score = jnp.where(hit, taken, score)
    pad = jnp.zeros((LANES - n_blk, tq), F32)
    sel_ref[...] = jnp.concatenate([sel, pad], axis=0).T


def _nsa_cmp(q, kc, vc, gate):
    b, s, _ = q.shape
    tq = ATTN_TQ
    nc = kc.shape[1]
    n_blk = nc // 2
    assert n_blk <= LANES
    n_sel = min(N_SEL, n_blk)
    return pl.pallas_call(
        functools.partial(_nsa_cmp_kernel, tq=tq, n_sel=n_sel),
        grid=(b, NSA_KV_GROUPS, s // tq),
        in_specs=[pl.BlockSpec((None, tq, 2 * LANES), lambda bi, g, i: (bi, i, g)),
                  pl.BlockSpec((None, nc, LANES), lambda bi, g, i: (bi, 0, g)),
                  pl.BlockSpec((None, nc, LANES), lambda bi, g, i: (bi, 0, g)),
                  pl.BlockSpec((None, tq, GATE_PAD), lambda bi, g, i: (bi, i, 0))],
        out_specs=[pl.BlockSpec((None, tq, 2 * LANES), lambda bi, g, i: (bi, i, g)),
                   pl.BlockSpec((None, None, tq, LANES), lambda bi, g, i: (bi, g, i, 0))],
        out_shape=[jax.ShapeDtypeStruct((b, s, 4 * LANES), F32),
                   jax.ShapeDtypeStruct((b, NSA_KV_GROUPS, s, LANES), F32)],
        compiler_params=_cparams(("parallel", "parallel", "parallel")), name="nsa_cmp",
    )(q, kc, vc, gate)


def _nsa_main_kernel(q_ref, ks_ref, vs_ref, kw_ref, vw_ref, sel_ref, gate_ref, ycmp_ref, o_ref,
                     *, tq, tk):
    g = pl.program_id(1)
    i = pl.program_id(2)
    rows = 4 * tq
    qst = _stack_group(q_ref[...], tq)
    row = lax.broadcasted_iota(jnp.int32, (rows, 1), 0)
    t_col = i * tq + _mod_pow2(row, tq)
    slope = _alibi_slope(g * NSA_GROUP + _div_pow2(row, tq))
    sel = sel_ref[...].astype(BF16)

    blk_of_key = _div_pow2(lax.broadcasted_iota(jnp.int32, (LANES, tk), 1), SEL_BLOCK)
    blk_row = lax.broadcasted_iota(jnp.int32, (LANES, tk), 0)
    rel = blk_row - blk_of_key
    lane_k = lax.broadcasted_iota(jnp.int32, (1, tk), 1)

    def sel_tile(kt, c):
        m, l, acc = c
        start = pl.multiple_of(kt * tk, tk)
        k = ks_ref[pl.ds(start, tk), :]
        v = vs_ref[pl.ds(start, tk), :]
        dist = t_col - (start + lane_k)
        s = _dot_nt(qst, k) - slope * dist.astype(F32)
        expand = jnp.where(rel == kt * (tk // SEL_BLOCK), 1.0, 0.0).astype(BF16)
        chosen = _dot(sel, expand)
        chosen = jnp.concatenate([chosen] * NSA_GROUP, axis=0)
        ok = jnp.where(dist >= 0, chosen, 0.0) > 0.5
        m_new = jnp.maximum(m, jnp.max(jnp.where(ok, s, NEG_INF), axis=1, keepdims=True))
        alpha = jnp.exp(m - m_new)
        p = jnp.where(ok, jnp.exp(s - m_new), 0.0)
        l = alpha * l + jnp.sum(p, axis=1, keepdims=True)
        acc = alpha * acc + _dot(p.astype(BF16), v)
        return m_new, l, acc

    n_tiles = (i * tq + tq + tk - 1) // tk
    init = (jnp.full((rows, 1), NEG_INF, F32), jnp.zeros((rows, 1), F32), jnp.zeros((rows, LANES), F32))
    _, l_s, acc_s = lax.fori_loop(0, n_tiles, sel_tile, init)
    o_sel = acc_s * (1.0 / l_s)

    wk = WINDOW + tq
    start = pl.multiple_of(jnp.maximum(i * tq - WINDOW, 0), tq)
    k = kw_ref[pl.ds(start, wk), :]
    v = vw_ref[pl.ds(start, wk), :]
    dist = t_col - (start + lax.broadcasted_iota(jnp.int32, (1, wk), 1))
    s = _dot_nt(qst, k) - slope * dist.astype(F32)
    ok = jnp.where(dist >= 0, dist, WINDOW) < WINDOW
    m = jnp.max(jnp.where(ok, s, NEG_INF), axis=1, keepdims=True)
    p = jnp.where(ok, jnp.exp(s - m), 0.0)
    l_w = jnp.sum(p, axis=1, keepdims=True)
    o_win = _dot(p.astype(BF16), v) * (1.0 / l_w)

    gate = _sigmoid(gate_ref[...])
    y = (ycmp_ref[...] + _expand_gate(gate, g, 1) * _unstack_group(o_sel, tq)
         + _expand_gate(gate, g, 2) * _unstack_group(o_win, tq))
    o_ref[...] = y.astype(o_ref.dtype)


def _nsa_main(q, kvsw, sel, gate, ycmp):
    b, s, _ = q.shape
    tq = ATTN_TQ
    tk = min(SEL_TK, s)
    assert s >= WINDOW + tq and s % tk == 0
    kv_spec = lambda off: pl.BlockSpec((None, s, LANES), lambda bi, g, i: (bi, 0, off + g))
    return pl.pallas_call(
        functools.partial(_nsa_main_kernel, tq=tq, tk=tk),
        grid=(b, NSA_KV_GROUPS, s // tq),
        in_specs=[pl.BlockSpec((None, tq, 2 * LANES), lambda bi, g, i: (bi, i, g)),
                  kv_spec(0), kv_spec(2), kv_spec(4), kv_spec(6),
                  pl.BlockSpec((None, None, tq, LANES), lambda bi, g, i: (bi, g, i, 0)),
                  pl.BlockSpec((None, tq, GATE_PAD), lambda bi, g, i: (bi, i, 0)),
                  pl.BlockSpec((None, tq, 2 * LANES), lambda bi, g, i: (bi, i, g))],
        out_specs=pl.BlockSpec((None, tq, 2 * LANES), lambda bi, g, i: (bi, i, g)),
        out_shape=jax.ShapeDtypeStruct((b, s, 4 * LANES), BF16),
        compiler_params=_cparams(("parallel", "parallel", "arbitrary")), name="nsa_main",
    )(q, kvsw, kvsw, kvsw, kvsw, sel, gate, ycmp)


def _merge_kernel(ysb_ref, ynsa_ref, g_ref, h_ref, wsb_ref, wnsa_ref, wo_ref, lng_ref, lnb_ref,
                  h1_ref, h1b_ref, *, alpha):
    d = h_ref.shape[1]
    g = g_ref[...]
    merged = (_sigmoid(g[:, :d]) * _dot(ysb_ref[...], wsb_ref[...])
              + _sigmoid(g[:, d:]) * _dot(ynsa_ref[...], wnsa_ref[...]))
    u = alpha * h_ref[...] + _dot(merged.astype(BF16), wo_ref[...])
    h1 = _layer_norm(u, lng_ref[...], lnb_ref[...])
    h1_ref[...] = h1
    h1b_ref[...] = h1.astype(BF16)


def _merge_ln1(ysb, ynsa, g, h, wsb, wnsa, wo, lng, lnb, alpha):
    t, d = h.shape
    tm = MERGE_TM
    row = lambda w: pl.BlockSpec((tm, w), lambda i: (i, 0))
    full = lambda a: pl.BlockSpec(a.shape, lambda i: (0, 0))
    return pl.pallas_call(
        functools.partial(_merge_kernel, alpha=alpha),
        grid=(t // tm,),
        in_specs=[row(ysb.shape[1]), row(ynsa.shape[1]), row(2 * d), row(d),
                  full(wsb), full(wnsa), full(wo), full(lng), full(lnb)],
        out_specs=[row(d), row(d)],
        out_shape=[jax.ShapeDtypeStruct((t, d), F32), jax.ShapeDtypeStruct((t, d), BF16)],
        compiler_params=_cparams(("parallel",)), name="merge_ln1",
    )(ysb, ynsa, g, h, wsb, wnsa, wo, lng, lnb)


def _router_kernel(x_ref, w_ref, b_ref, idx_ref, wt_ref):
    logits = _dot_nt(w_ref[...], x_ref[...]) + b_ref[...]
    n_e, tm = logits.shape
    e_i = lax.broadcasted_iota(jnp.int32, (n_e, tm), 0)
    vals, idxs = [], []
    for _ in range(TOP_K):
        mx = jnp.max(logits, axis=0, keepdims=True)
        first = jnp.min(jnp.where(logits == mx, e_i, n_e), axis=0, keepdims=True)
        vals.append(mx)
        idxs.append(first)
        logits = jnp.where(e_i == first, -jnp.inf, logits)
    ex = [jnp.exp(v - vals[0]) for v in vals]
    inv = 1.0 / (ex[0] + ex[1] + ex[2] + ex[3])
    idx_ref[...] = jnp.concatenate(idxs, axis=0)
    wt_ref[...] = jnp.concatenate([e * inv for e in ex], axis=0)


def _router(h1b, w_t, b_col):
    t, d = h1b.shape
    tm = ROUTER_TM
    return pl.pallas_call(
        _router_kernel,
        grid=(t // tm,),
        in_specs=[pl.BlockSpec((tm, d), lambda i: (i, 0)),
                  pl.BlockSpec(w_t.shape, lambda i: (0, 0)),
                  pl.BlockSpec(b_col.shape, lambda i: (0, 0))],
        out_specs=[pl.BlockSpec((TOP_K, tm), lambda i: (0, i)),
                   pl.BlockSpec((TOP_K, tm), lambda i: (0, i))],
        out_shape=[jax.ShapeDtypeStruct((TOP_K, t), jnp.int32),
                   jax.ShapeDtypeStruct((TOP_K, t), F32)],
        compiler_params=_cparams(("parallel",)), name="router",
    )(h1b, w_t, b_col)


def _expert_kernel(te_ref, nv_ref, x_ref, wgu_ref, bgu_ref, wd_ref, bd_ref, cw_ref, y_ref, wgu_bf, wd_bf):
    i = pl.program_id(0)

    @pl.when(jnp.logical_or(i == 0, te_ref[i] != te_ref[jnp.maximum(i - 1, 0)]))
    def _():
        wgu_bf[...] = wgu_ref[...].astype(BF16)
        wd_bf[...] = wd_ref[...].astype(BF16)

    @pl.when(i < nv_ref[0])
    def _():
        gu = _dot(x_ref[...], wgu_bf[...]) + bgu_ref[...]
        gate = jnp.minimum(gu[:, :D_FF], SWIGLU_LIMIT)
        up = jnp.clip(gu[:, D_FF:], -SWIGLU_LIMIT, SWIGLU_LIMIT)
        act = (up + 1.0) * gate * _sigmoid(SWIGLU_ALPHA * gate)
        y = _dot(act.astype(BF16), wd_bf[...]) + bd_ref[...]
        y_ref[...] = (cw_ref[...] * y).astype(y_ref.dtype)

    @pl.when(i >= nv_ref[0])
    def _():
        y_ref[...] = jnp.zeros_like(y_ref)


def _experts(tile_expert, n_valid, xs, wgu, bgu, wd, bd, cw):
    p, d = xs.shape
    tm = EXPERT_TM
    n_e, _, two_ff = wgu.shape
    grid_spec = pltpu.PrefetchScalarGridSpec(
        num_scalar_prefetch=2, grid=(p // tm,),
        in_specs=[pl.BlockSpec((tm, d), lambda i, te, nv: (i, 0)),
                  pl.BlockSpec((None, d, two_ff), lambda i, te, nv: (te[i], 0, 0)),
                  pl.BlockSpec((None, 1, two_ff), lambda i, te, nv: (te[i], 0, 0)),
                  pl.BlockSpec((None, two_ff // 2, d), lambda i, te, nv: (te[i], 0, 0)),
                  pl.BlockSpec((None, 1, d), lambda i, te, nv: (te[i], 0, 0)),
                  pl.BlockSpec((tm, 1), lambda i, te, nv: (i, 0))],
        out_specs=pl.BlockSpec((tm, d), lambda i, te, nv: (i, 0)),
        scratch_shapes=[pltpu.VMEM((d, two_ff), BF16), pltpu.VMEM((two_ff // 2, d), BF16)])
    return pl.pallas_call(
        _expert_kernel, grid_spec=grid_spec,
        out_shape=jax.ShapeDtypeStruct((p, d), BF16),
        compiler_params=_cparams(("arbitrary",)), name="experts",
    )(tile_expert, n_valid, xs, wgu, bgu, wd, bd, cw)


def _out_kernel(h1_ref, h1b_ref, yg_ref, p_ref, wpg_ref, wpp_ref, lng_ref, lnb_ref, o_ref, *, alpha):
    moe = (yg_ref[0].astype(F32) + yg_ref[1].astype(F32) + yg_ref[2].astype(F32) + yg_ref[3].astype(F32))
    ple = _sigmoid(_dot(h1b_ref[...], wpg_ref[...])) * _dot(p_ref[...].astype(BF16), wpp_ref[...])
    u = alpha * h1_ref[...] + moe + ple
    o_ref[...] = _layer_norm(u, lng_ref[...], lnb_ref[...])


def _out_ln2(h1, h1b, yg, p, wpg, wpp, lng, lnb, alpha):
    t, d = h1.shape
    tm = OUT_TM
    row = lambda w: pl.BlockSpec((tm, w), lambda i: (i, 0))
    full = lambda a: pl.BlockSpec(a.shape, lambda i: (0, 0))
    return pl.pallas_call(
        functools.partial(_out_kernel, alpha=alpha),
        grid=(t // tm,),
        in_specs=[row(d), row(d), pl.BlockSpec((TOP_K, tm, d), lambda i: (0, i, 0)), row(p.shape[1]),
                  full(wpg), full(wpp), full(lng), full(lnb)],
        out_specs=row(d),
        out_shape=jax.ShapeDtypeStruct((t, d), F32),
        compiler_params=_cparams(("parallel",)), name="out_ln2",
    )(h1, h1b, yg, p, wpg, wpp, lng, lnb)


def _dup_groups(w):
    d = w.shape[0]
    w = w.reshape(d, NSA_KV_GROUPS, 1, HEAD_DIM)
    return jnp.broadcast_to(w, (d, NSA_KV_GROUPS, 2, HEAD_DIM)).reshape(d, NSA_KV_GROUPS * LANES)


def _in_proj_weights(w_in):
    sbw = SB_HEADS * HEAD_DIM
    qw = NSA_HEADS * HEAD_DIM
    kvw = NSA_KV_GROUPS * HEAD_DIM
    d = w_in.shape[0]
    widths = (sbw, sbw, sbw, qw) + (kvw,) * 6 + (3 * NSA_HEADS, d, d)
    parts, start = [], 0
    for wd in widths:
        parts.append(w_in[:, start:start + wd])
        start += wd
    (sb_q, sb_k, sb_v, nsa_q, k_cmp, v_cmp, k_sel, v_sel, k_win, v_win, gate, g_sb, g_nsa) = parts
    kvsw = jnp.concatenate([_dup_groups(k_sel), _dup_groups(v_sel), _dup_groups(k_win), _dup_groups(v_win)], axis=1)
    gate = jnp.pad(gate, ((0, 0), (0, GATE_PAD - gate.shape[1])))
    g_both = jnp.concatenate([g_sb, g_nsa], axis=1)
    ws = [sb_q, sb_k, sb_v, nsa_q, k_cmp, v_cmp, kvsw, gate, g_both]
    dtypes = [BF16, BF16, BF16, BF16, F32, F32, BF16, F32, F32]
    scale = HEAD_DIM ** -0.5
    scales = [scale, 1.0, 1.0, scale, 1.0, 1.0, 1.0, 1.0, 1.0]
    return [w.astype(BF16) for w in ws], dtypes, scales


def _compress_weights(w1, w2, pe):
    eye = jnp.eye(NSA_KV_GROUPS, dtype=w1.dtype)
    w1b = jnp.einsum('ldh,pg->lpdgh', w1, eye).reshape(CMP_BLOCK * NSA_KV_GROUPS * HEAD_DIM,
                                                       NSA_KV_GROUPS * CMP_HIDDEN)
    w2b = jnp.einsum('hd,pg,r->phgrd', w2, eye, jnp.ones((2,), w2.dtype)).reshape(
        NSA_KV_GROUPS * CMP_HIDDEN, NSA_KV_GROUPS * LANES)
    peb = jnp.broadcast_to(pe[:, None, :], (CMP_BLOCK, NSA_KV_GROUPS, HEAD_DIM)).reshape(1, -1)
    return peb.astype(F32), w1b.astype(BF16), w2b.astype(BF16)


def _moe_plan(idx_t, wt_t, tm):
    k, t = idx_t.shape
    n = k * t
    i32 = jnp.int32
    experts = jnp.arange(N_EXPERTS, dtype=i32)
    a_ids = jnp.arange(n, dtype=i32)
    sorted_e, order, sorted_w = lax.sort((idx_t.reshape(n), a_ids, wt_t.reshape(n)), num_keys=1, is_stable=True)
    grp_end = jnp.sum((sorted_e[None, :] <= experts[:, None]).astype(i32), axis=1)
    counts = grp_end - jnp.concatenate([jnp.zeros((1,), i32), grp_end[:-1]])
    padded = ((counts + tm - 1) // tm) * tm
    pad_end = jnp.cumsum(padded)
    pad_off = pad_end - padded
    gap = padded - counts
    row_of_sorted = a_ids + jnp.sum(jnp.where(a_ids[:, None] >= grp_end[None, :], gap[None, :], 0), axis=1)
    _, pos = lax.sort((order, row_of_sorted), num_keys=1)
    p_rows = n + N_EXPERTS * tm
    rows = jnp.arange(p_rows, dtype=i32)
    past = rows[:, None] >= pad_end[None, :]
    sorted_of_row = rows - jnp.sum(jnp.where(past, gap[None, :], 0), axis=1)
    row_grp_end = jnp.sum(jnp.where(past, counts[None, :], 0), axis=1) + jnp.sum(
        jnp.where((rows[:, None] >= pad_off[None, :]) & ~past, counts[None, :], 0), axis=1)
    real = sorted_of_row < row_grp_end
    src = jnp.minimum(sorted_of_row, n - 1)
    take = lambda a: a.at[src].get(mode='promise_in_bounds')
    src_tok = jnp.where(real, take(order) % t, 0)
    cw = jnp.where(real, take(sorted_w), 0.0)
    tile_start = jnp.arange(p_rows // tm, dtype=i32) * tm
    tile_expert = jnp.minimum(jnp.sum((tile_start[:, None] >= pad_end[None, :]).astype(i32), axis=1), N_EXPERTS - 1)
    n_valid = (pad_end[-1] // tm).astype(i32).reshape(1)
    return src_tok, cw.reshape(p_rows, 1), pos, tile_expert, n_valid


def _layer(h, p_i, w_in, w_cmp1, w_cmp2, pe_cmp, w_br_sb, w_br_nsa, w_o, ln1_g, ln1_b, ln2_g, ln2_b,
           w_router, b_router, w_gu, b_gu, w_down, b_down, w_ple_gate, w_ple_proj, *, batch, alpha):
    t, d = h.shape
    s = t // batch
    ws, dtypes, scales = _in_proj_weights(w_in)
    sb_q, sb_k, sb_v, nsa_q, k_cmp, v_cmp, kvsw, gate, g_both = _in_proj(h, ws, dtypes, scales)
    b3 = lambda a: a.reshape(batch, s, a.shape[1])

    y_sb = _sb_attention(b3(sb_q), b3(sb_k), b3(sb_v)).reshape(t, -1)

    nc = s // CMP_BLOCK
    cmp_rows = lambda a: a.reshape(batch, nc // 2, 2 * CMP_BLOCK * a.shape[1])
    kc = _compress(cmp_rows(k_cmp), *_compress_weights(w_cmp1[0], w_cmp2[0], pe_cmp[0]))
    vc = _compress(cmp_rows(v_cmp), *_compress_weights(w_cmp1[1], w_cmp2[1], pe_cmp[1]))
    y_cmp, sel = _nsa_cmp(b3(nsa_q), kc, vc, b3(gate))
    y_nsa = _nsa_main(b3(nsa_q), b3(kvsw), sel, b3(gate), y_cmp).reshape(t, -1)

    row = lambda v: v.reshape(1, -1).astype(F32)
    h1, h1b = _merge_ln1(y_sb, y_nsa, g_both, h, w_br_sb.astype(BF16), w_br_nsa.astype(BF16),
                         w_o.astype(BF16), row(ln1_g), row(ln1_b), alpha)

    idx_t, wt_t = _router(h1b, w_router.T.astype(BF16), b_router.reshape(-1, 1).astype(F32))
    src_tok, cw, pos, tile_expert, n_valid = _moe_plan(idx_t, wt_t, EXPERT_TM)
    xs = h1b.at[src_tok].get(mode='promise_in_bounds')
    y = _experts(tile_expert, n_valid, xs, w_gu, b_gu.reshape(N_EXPERTS, 1, -1),
                 w_down, b_down.reshape(N_EXPERTS, 1, -1), cw)
    yg = y.at[pos].get(mode='promise_in_bounds').reshape(TOP_K, t, d)
    return _out_ln2(h1, h1b, yg, p_i, w_ple_gate.astype(BF16), w_ple_proj.astype(BF16),
                    row(ln2_g), row(ln2_b), alpha)


def kernel(x, p, w_in, w_cmp1, w_cmp2, pe_cmp, w_br_sb, w_br_nsa, w_o, ln1_g, ln1_b, ln2_g, ln2_b,
           w_router, b_router, w_gu, b_gu, w_down, b_down, w_ple_gate, w_ple_proj):
    batch, s, d = x.shape
    depth = w_in.shape[0]
    alpha = (2 * depth) ** 0.25
    h = x.reshape(batch * s, d)
    for i in range(depth):
        h = _layer(h, p[i].reshape(batch * s, -1), w_in[i], w_cmp1[i], w_cmp2[i], pe_cmp[i],
                   w_br_sb[i], w_br_nsa[i], w_o[i], ln1_g[i], ln1_b[i], ln2_g[i], ln2_b[i],
                   w_router[i], b_router[i], w_gu[i], b_gu[i], w_down[i], b_down[i],
                   w_ple_gate[i], w_ple_proj[i], batch=batch, alpha=alpha)
    return h.reshape(batch, s, d)
```

```python
import functools

import jax
import jax.numpy as jnp
from jax import lax
from jax.experimental import pallas as pl
from jax.experimental.pallas import tpu as pltpu

F32 = jnp.float32
BF16 = jnp.bfloat16

HEAD_DIM = 64
LANES = 128
SB_HEADS = 8
NSA_HEADS = 8
NSA_KV_GROUPS = 2
NSA_GROUP = NSA_HEADS // NSA_KV_GROUPS
CMP_BLOCK = 32
CMP_HIDDEN = 128
SEL_BLOCK = 64
N_SEL = 16
WINDOW = 512
N_EXPERTS = 32
TOP_K = 4
D_FF = 1024
SWIGLU_ALPHA = 1.702
SWIGLU_LIMIT = 7.0
LN_EPS = 1e-5
NEG_INF = -1e30
FORCED_SCORE = 1e6
GATE_PAD = 128
MASK_BIG = 2.0 ** 100
FEAT_HI, FEAT_LO = 64, 65
SB_DEAD_LOG = -104.0

VMEM_LIMIT = 56 * 1024 * 1024

ATTN_TQ = 128
SEL_TK = 512
PROJ_TM = 256
MERGE_TM = 256
ROUTER_TM = 512
EXPERT_TM = 512
OUT_TM = 256


def _cparams(sem):
    return pltpu.CompilerParams(dimension_semantics=sem, vmem_limit_bytes=VMEM_LIMIT)


def _sigmoid(x):
    return 1.0 / (1.0 + jnp.exp(-x))


def _div_pow2(x, n):
    assert n & (n - 1) == 0
    return lax.shift_right_logical(x, n.bit_length() - 1)


def _mod_pow2(x, n):
    assert n & (n - 1) == 0
    return x & (n - 1)


def _split_bf16(x):
    hi = x.astype(BF16)
    lo = (x - hi.astype(F32)).astype(BF16)
    return hi, lo


def _dot(a, b):
    return jnp.dot(a, b, preferred_element_type=F32)


def _dot_nt(a, b):
    return lax.dot_general(a, b, (((1,), (1,)), ((), ())), preferred_element_type=F32)


def _dot_tn(a, b):
    return lax.dot_general(a, b, (((0,), (0,)), ((), ())), preferred_element_type=F32)


def _layer_norm(u, g, b):
    mu = jnp.mean(u, axis=-1, keepdims=True)
    d = u - mu
    var = jnp.mean(d * d, axis=-1, keepdims=True)
    return d * lax.rsqrt(var + LN_EPS) * g + b


def _in_proj_kernel(x_ref, *refs, scales):
    n = len(scales)
    x = x_ref[...].astype(BF16)
    for w_ref, o_ref, s in zip(refs[:n], refs[n:], scales):
        acc = _dot(x, w_ref[...])
        if s != 1.0:
            acc = acc * s
        o_ref[...] = acc.astype(o_ref.dtype)


def _in_proj(h, weights, dtypes, scales):
    t, d = h.shape
    tm = PROJ_TM
    in_specs = [pl.BlockSpec((tm, d), lambda i: (i, 0))]
    in_specs += [pl.BlockSpec(w.shape, lambda i: (0, 0)) for w in weights]
    out_specs = [pl.BlockSpec((tm, w.shape[1]), lambda i: (i, 0)) for w in weights]
    out_shape = [jax.ShapeDtypeStruct((t, w.shape[1]), dt) for w, dt in zip(weights, dtypes)]
    return pl.pallas_call(
        functools.partial(_in_proj_kernel, scales=tuple(scales)),
        grid=(t // tm,), in_specs=in_specs, out_specs=out_specs, out_shape=out_shape,
        compiler_params=_cparams(("parallel",)), name="in_proj",
    )(h, *weights)


def _stack_heads(q2, tq):
    lane = lax.broadcasted_iota(jnp.int32, (tq, LANES), 1)
    zero = jnp.zeros_like(q2)
    return jnp.concatenate([jnp.where(lane < HEAD_DIM, q2, zero),
                            jnp.where(lane >= HEAD_DIM, q2, zero)], axis=0)


def _unstack_heads(o, tq):
    lane = lax.broadcasted_iota(jnp.int32, (tq, LANES), 1)
    return jnp.where(lane < HEAD_DIM, o[:tq], o[tq:])


def _sb_kernel(q_ref, k_ref, v_ref, o_ref, *, tq, n_pairs):
    i = pl.program_id(1)
    r_i = lax.broadcasted_iota(jnp.int32, (tq, tq), 0)
    c_i = lax.broadcasted_iota(jnp.int32, (tq, tq), 1)
    tri = jnp.where(r_i > c_i, 1.0, 0.0).astype(BF16)
    rhs = jnp.concatenate([tri, jnp.ones((tq, tq), BF16)], axis=1)
    rhs = jnp.concatenate([rhs, rhs], axis=0)
    r2 = lax.broadcasted_iota(jnp.int32, (2 * tq, tq), 0) & (tq - 1)
    causal = lax.broadcasted_iota(jnp.int32, (2 * tq, tq), 1) < r2
    qs = [_stack_heads(q_ref[:, p * LANES:(p + 1) * LANES], tq) for p in range(n_pairs)]

    def tile(p, j, carry, acc, diag):
        start = pl.multiple_of(j * tq, tq)
        kj = k_ref[pl.ds(start, tq), p * LANES:(p + 1) * LANES]
        vj = v_ref[pl.ds(start, tq), p * LANES:(p + 1) * LANES]
        z = _dot_nt(qs[p], kj)
        ls = -(jnp.maximum(z, 0.0) + jnp.log(1.0 + jnp.exp(-jnp.abs(z))))
        if diag:
            ls = jnp.where(causal, ls, 0.0)
        hi, lo = _split_bf16(ls)
        sums = _dot(jnp.concatenate([hi, lo], axis=1), rhs)
        w = jnp.exp(z + ls + sums[:, :tq] + carry)
        if diag:
            w = jnp.where(causal, w, 0.0)
        return carry + sums[:, tq:], acc + _dot(w.astype(BF16), vj)

    def live(carries):
        worst = functools.reduce(jnp.maximum, carries)
        return jnp.max(worst) > SB_DEAD_LOG

    zeros = jnp.zeros((2 * tq, LANES), F32)
    state = [tile(p, i, zeros, zeros, True) for p in range(n_pairs)]
    carries = tuple(s[0] for s in state)
    accs = tuple(s[1] for s in state)

    def cond(c):
        return jnp.logical_and(c[0] <= i, c[1])

    def body(c):
        step, _, carries, accs = c
        state = [tile(p, i - step, carries[p], accs[p], False) for p in range(n_pairs)]
        carries = tuple(s[0] for s in state)
        return step + 1, live(carries), carries, tuple(s[1] for s in state)

    _, _, _, accs = lax.while_loop(cond, body, (jnp.int32(1), live(carries), carries, accs))
    for p in range(n_pairs):
        o_ref[:, p * LANES:(p + 1) * LANES] = _unstack_heads(accs[p], tq).astype(o_ref.dtype)


def _sb_attention(q, k, v):
    b, s, w = q.shape
    tq = ATTN_TQ
    return pl.pallas_call(
        functools.partial(_sb_kernel, tq=tq, n_pairs=w // LANES),
        grid=(b, s // tq),
        in_specs=[pl.BlockSpec((None, tq, w), lambda bi, i: (bi, i, 0)),
                  pl.BlockSpec((None, s, w), lambda bi, i: (bi, 0, 0)),
                  pl.BlockSpec((None, s, w), lambda bi, i: (bi, 0, 0))],
        out_specs=pl.BlockSpec((None, tq, w), lambda bi, i: (bi, i, 0)),
        out_shape=jax.ShapeDtypeStruct((b, s, w), BF16),
        compiler_params=_cparams(("parallel", "arbitrary")), name="sb_attn",
    )(q, k, v)


def _compress_kernel(x_ref, pe_ref, w1_ref, w2_ref, o_ref):
    x = (x_ref[...] + pe_ref[...]).astype(BF16)
    hid = _dot(x, w1_ref[...])
    hid = hid * _sigmoid(hid)
    o_ref[...] = _dot(hid.astype(BF16), w2_ref[...]).astype(o_ref.dtype)


def _compress(x, pe, w1, w2):
    b, half, width = x.shape
    blk = width // 2
    out = pl.pallas_call(
        _compress_kernel,
        grid=(b, 2),
        in_specs=[pl.BlockSpec((None, half, blk), lambda bi, par: (bi, 0, par)),
                  pl.BlockSpec(pe.shape, lambda bi, par: (0, 0)),
                  pl.BlockSpec(w1.shape, lambda bi, par: (0, 0)),
                  pl.BlockSpec(w2.shape, lambda bi, par: (0, 0))],
        out_specs=pl.BlockSpec((None, None, half, w2.shape[1]), lambda bi, par: (bi, par, 0, 0)),
        out_shape=jax.ShapeDtypeStruct((b, 2, half, w2.shape[1]), BF16),
        compiler_params=_cparams(("parallel", "parallel")), name="compress",
    )(x, pe, w1, w2)
    return out.reshape(b, 2 * half, w2.shape[1])


def _stack_group(q, tq):
    return jnp.concatenate([_stack_heads(q[:, :LANES], tq), _stack_heads(q[:, LANES:], tq)], axis=0)


def _unstack_group(o, tq):
    return jnp.concatenate([_unstack_heads(o[:2 * tq], tq), _unstack_heads(o[2 * tq:], tq)], axis=1)


def _alibi_slope(head_idx):
    return lax.bitcast_convert_type(lax.shift_left(126 - head_idx, 23), F32)


def _expand_gate(gate_sig, g, branch):
    c_i = lax.broadcasted_iota(jnp.int32, (GATE_PAD, 2 * LANES), 0)
    l_i = lax.broadcasted_iota(jnp.int32, (GATE_PAD, 2 * LANES), 1)
    col = g * (3 * NSA_GROUP) + _div_pow2(l_i, HEAD_DIM) * 3 + branch
    onehot = jnp.where(c_i == col, 1.0, 0.0).astype(BF16)
    hi, lo = _split_bf16(gate_sig)
    return _dot(hi, onehot) + _dot(lo, onehot)


def _nsa_cmp_kernel(q_ref, kc_ref, vc_ref, gate_ref, y_ref, sel_ref, flag_ref, *, tq, n_sel):
    g = pl.program_id(1)
    i = pl.program_id(2)
    nc = kc_ref.shape[0]
    half = nc // 2
    qst = _stack_group(q_ref[...], tq)
    s = _dot_nt(kc_ref[...], qst)
    r_i = lax.broadcasted_iota(jnp.int32, (nc, 4 * tq), 0)
    l_i = lax.broadcasted_iota(jnp.int32, (nc, 4 * tq), 1)
    cblk = 2 * _mod_pow2(r_i, half) + _div_pow2(r_i, half)
    c_end = cblk * CMP_BLOCK + (CMP_BLOCK - 1)
    t = i * tq + _mod_pow2(l_i, tq)
    dist = t - c_end
    slope = _alibi_slope(g * NSA_GROUP + _div_pow2(l_i, tq))
    s = s - slope * dist.astype(F32)
    mask = dist >= 0
    m = jnp.max(jnp.where(mask, s, NEG_INF), axis=0, keepdims=True)
    p = jnp.where(mask, jnp.exp(s - m), 0.0)
    l = jnp.sum(p, axis=0, keepdims=True)
    p = p * jnp.where(l > 0.0, 1.0 / l, 0.0)

    o = _dot_tn(p.astype(BF16), vc_ref[...])
    gate = _expand_gate(_sigmoid(gate_ref[...]), g, 0)
    y_ref[...] = (_unstack_group(o, tq) * gate).astype(y_ref.dtype)

    imp_c = p[:, 0:tq] + p[:, tq:2 * tq] + p[:, 2 * tq:3 * tq] + p[:, 3 * tq:4 * tq]
    imp = imp_c[:half] + imp_c[half:]
    n_blk = half
    blk = lax.broadcasted_iota(jnp.int32, (n_blk, tq), 0)
    tt = i * tq + lax.broadcasted_iota(jnp.int32, (n_blk, tq), 1)
    cur = _div_pow2(tt, SEL_BLOCK)
    score = jnp.where(blk * SEL_BLOCK <= tt, imp, NEG_INF)
    for forced_blk in (0, cur, cur - 1):
        score = jnp.where(blk == forced_blk, FORCED_SCORE, score)

    taken = -jnp.inf
    sel = jnp.zeros((n_blk, tq), F32)
    for _ in range(n_sel):
        mx = jnp.max(score, axis=0, keepdims=True)
        first = jnp.min(jnp.where(score == mx, blk, n_blk), axis=0, keepdims=True)
        hit = blk == first
        sel = jnp.where(hit, 1.0, sel)
        score = jnp.where(hit, taken, score)
    sel = jnp.concatenate([sel, jnp.zeros((LANES - n_blk, tq), F32)], axis=0)
    sel_ref[...] = sel.T
    picks = _dot_nt(jnp.ones((8, tq), BF16), sel.astype(BF16))
    flag_ref[...] = picks[0:1].astype(jnp.int32)


def _nsa_cmp(q, kc, vc, gate):
    b, s, _ = q.shape
    tq = ATTN_TQ
    nc = kc.shape[1]
    n_blk = nc // 2
    assert n_blk <= LANES
    n_sel = min(N_SEL, n_blk)
    return pl.pallas_call(
        functools.partial(_nsa_cmp_kernel, tq=tq, n_sel=n_sel),
        grid=(b, NSA_KV_GROUPS, s // tq),
        in_specs=[pl.BlockSpec((None, tq, 2 * LANES), lambda bi, g, i: (bi, i, g)),
                  pl.BlockSpec((None, nc, LANES), lambda bi, g, i: (bi, 0, g)),
                  pl.BlockSpec((None, nc, LANES), lambda bi, g, i: (bi, 0, g)),
                  pl.BlockSpec((None, tq, GATE_PAD), lambda bi, g, i: (bi, i, 0))],
        out_specs=[pl.BlockSpec((None, tq, 2 * LANES), lambda bi, g, i: (bi, i, g)),
                   pl.BlockSpec((None, None, tq, LANES), lambda bi, g, i: (bi, g, i, 0)),
                   pl.BlockSpec((None, None, None, 1, LANES), lambda bi, g, i: (bi, g, i, 0, 0))],
        out_shape=[jax.ShapeDtypeStruct((b, s, 4 * LANES), F32),
                   jax.ShapeDtypeStruct((b, NSA_KV_GROUPS, s, LANES), F32),
                   jax.ShapeDtypeStruct((b, NSA_KV_GROUPS, s // tq, 1, LANES), jnp.int32)],
        compiler_params=_cparams(("parallel", "parallel", "parallel")), name="nsa_cmp",
    )(q, kc, vc, gate)


def _key_features(s, tk, wk):
    def table(n, period, with_blocks):
        key = lax.broadcasted_iota(jnp.int32, (n, LANES), 0)
        lane = lax.broadcasted_iota(jnp.int32, (n, LANES), 1)
        r = key % period
        pos = jnp.where(lane == FEAT_HI, r // SEL_BLOCK, jnp.where(lane == FEAT_LO, r % SEL_BLOCK, 0))
        if with_blocks:
            pos = jnp.where(lane < SEL_BLOCK, (key // SEL_BLOCK == lane).astype(jnp.int32), pos)
        return pos.astype(BF16)
    return table(s, tk, True), table(wk, wk, False)


def _nsa_main_kernel(flag_ref, q_ref, ks_ref, vs_ref, kw_ref, vw_ref, kfeat_ref, rfeat_ref, sel_ref,
                     gate_ref, ycmp_ref, o_ref, *, tq, tk):
    g = pl.program_id(1)
    i = pl.program_id(2)
    rows = NSA_GROUP * tq
    qs = pl.multiple_of(i * tq, tq)
    qst = _stack_group(q_ref[...], tq)
    row = lax.broadcasted_iota(jnp.int32, (rows, 1), 0)
    t_loc = _mod_pow2(row, tq)
    slope_col = _alibi_slope(g * NSA_GROUP + _div_pow2(row, tq))

    lane = lax.broadcasted_iota(jnp.int32, (tq, LANES), 1)
    first_blk = i * (tq // SEL_BLOCK)
    open_blk = jnp.where(lane < first_blk, sel_ref[...], 0.0) > 0.5
    blk_add = jnp.where(open_blk, 0.0, -MASK_BIG)
    f_sel, f_win = [], []
    for h in range(NSA_GROUP):
        slope = _alibi_slope(jnp.full((tq, LANES), g * NSA_GROUP + h, jnp.int32))
        pos = jnp.where(lane == FEAT_HI, slope * SEL_BLOCK, jnp.where(lane == FEAT_LO, slope, 0.0))
        f_win.append(pos)
        f_sel.append(jnp.where(lane < SEL_BLOCK, blk_add, pos))
    q_sel = jnp.concatenate([qst, jnp.concatenate(f_sel, axis=0).astype(BF16)], axis=1)
    q_win = jnp.concatenate([qst, jnp.concatenate(f_win, axis=0).astype(BF16)], axis=1)

    def online(c, s, bias, v):
        m, l, acc = c
        m_new = jnp.maximum(m, jnp.max(s, axis=1, keepdims=True) + bias)
        alpha = jnp.exp(m - m_new)
        p = jnp.exp(s + (bias - m_new))
        l = alpha * l + jnp.sum(p, axis=1, keepdims=True)
        acc = alpha * acc + _dot(p.astype(BF16), v)
        return m_new, l, acc

    r_i = lax.broadcasted_iota(jnp.int32, (tq, tq), 0)
    c_i = lax.broadcasted_iota(jnp.int32, (tq, tq), 1)
    causal_add = jnp.where(c_i <= r_i, 0.0, -MASK_BIG)
    causal_add = jnp.concatenate([causal_add] * NSA_GROUP, axis=0)
    k = jnp.concatenate([ks_ref[pl.ds(qs, tq), :], rfeat_ref[0:tq, :]], axis=1)
    s = _dot_nt(q_sel, k) + causal_add
    init = (jnp.full((rows, 1), NEG_INF, F32), jnp.zeros((rows, 1), F32), jnp.zeros((rows, LANES), F32))
    state = online(init, s, -slope_col * t_loc.astype(F32), vs_ref[pl.ds(qs, tq), :])

    def bulk(kt, c):
        start = pl.multiple_of(kt * tk, tk)
        picked = flag_ref[0, kt * (tk // SEL_BLOCK)]
        for j in range(1, tk // SEL_BLOCK):
            picked = picked + flag_ref[0, kt * (tk // SEL_BLOCK) + j]

        def run(c):
            k = jnp.concatenate([ks_ref[pl.ds(start, tk), :], kfeat_ref[pl.ds(start, tk), :]], axis=1)
            bias = -slope_col * (qs - start + t_loc).astype(F32)
            return online(c, _dot_nt(q_sel, k), bias, vs_ref[pl.ds(start, tk), :])

        return lax.cond(picked > 0, run, lambda c: c, c)

    _, l_s, acc_s = lax.fori_loop(0, (qs + tk - 1) // tk, bulk, state)
    o_sel = acc_s * (1.0 / l_s)

    wk = WINDOW + tq
    start = pl.multiple_of(jnp.maximum(qs - WINDOW, 0), tq)
    w_r = lax.broadcasted_iota(jnp.int32, (tq, wk), 0)
    w_c = lax.broadcasted_iota(jnp.int32, (tq, wk), 1)
    dist = (qs - start) + w_r - w_c
    band_add = jnp.where(jnp.where(dist >= 0, dist, WINDOW) < WINDOW, 0.0, -MASK_BIG)
    band_add = jnp.concatenate([band_add] * NSA_GROUP, axis=0)
    k = jnp.concatenate([kw_ref[pl.ds(start, wk), :], rfeat_ref[...]], axis=1)
    s = _dot_nt(q_win, k) + band_add
    p = jnp.exp(s - jnp.max(s, axis=1, keepdims=True))
    l_w = jnp.sum(p, axis=1, keepdims=True)
    o_win = _dot(p.astype(BF16), vw_ref[pl.ds(start, wk), :]) * (1.0 / l_w)

    gate = _sigmoid(gate_ref[...])
    y = (ycmp_ref[...] + _expand_gate(gate, g, 1) * _unstack_group(o_sel, tq)
         + _expand_gate(gate, g, 2) * _unstack_group(o_win, tq))
    o_ref[...] = y.astype(o_ref.dtype)


def _nsa_main(q, kvsw, sel, flags, gate, ycmp):
    b, s, _ = q.shape
    tq = ATTN_TQ
    tk = min(SEL_TK, s)
    wk = WINDOW + tq
    assert s >= wk and s % tk == 0 and s // SEL_BLOCK <= SEL_BLOCK
    kfeat, rfeat = _key_features(s, tk, wk)
    kv_spec = lambda off: pl.BlockSpec((None, s, LANES), lambda bi, g, i: (bi, 0, off + g))
    return pl.pallas_call(
        functools.partial(_nsa_main_kernel, tq=tq, tk=tk),
        grid=(b, NSA_KV_GROUPS, s // tq),
        in_specs=[pl.BlockSpec((None, None, None, 1, LANES), lambda bi, g, i: (bi, g, i, 0, 0),
                               memory_space=pltpu.SMEM),
                  pl.BlockSpec((None, tq, 2 * LANES), lambda bi, g, i: (bi, i, g)),
                  kv_spec(0), kv_spec(2), kv_spec(4), kv_spec(6),
                  pl.BlockSpec((s, LANES), lambda bi, g, i: (0, 0)),
                  pl.BlockSpec((wk, LANES), lambda bi, g, i: (0, 0)),
                  pl.BlockSpec((None, None, tq, LANES), lambda bi, g, i: (bi, g, i, 0)),
                  pl.BlockSpec((None, tq, GATE_PAD), lambda bi, g, i: (bi, i, 0)),
                  pl.BlockSpec((None, tq, 2 * LANES), lambda bi, g, i: (bi, i, g))],
        out_specs=pl.BlockSpec((None, tq, 2 * LANES), lambda bi, g, i: (bi, i, g)),
        out_shape=jax.ShapeDtypeStruct((b, s, 4 * LANES), BF16),
        compiler_params=_cparams(("parallel", "parallel", "arbitrary")), name="nsa_main",
    )(flags, q, kvsw, kvsw, kvsw, kvsw, kfeat, rfeat, sel, gate, ycmp)


def _merge_kernel(ysb_ref, ynsa_ref, g_ref, h_ref, wsb_ref, wnsa_ref, wo_ref, lng_ref, lnb_ref,
                  h1_ref, h1b_ref, *, alpha):
    d = h_ref.shape[1]
    g = g_ref[...]
    merged = (_sigmoid(g[:, :d]) * _dot(ysb_ref[...], wsb_ref[...])
              + _sigmoid(g[:, d:]) * _dot(ynsa_ref[...], wnsa_ref[...]))
    u = alpha * h_ref[...] + _dot(merged.astype(BF16), wo_ref[...])
    h1 = _layer_norm(u, lng_ref[...], lnb_ref[...])
    h1_ref[...] = h1
    h1b_ref[...] = h1.astype(BF16)


def _merge_ln1(ysb, ynsa, g, h, wsb, wnsa, wo, lng, lnb, alpha):
    t, d = h.shape
    tm = MERGE_TM
    row = lambda w: pl.BlockSpec((tm, w), lambda i: (i, 0))
    full = lambda a: pl.BlockSpec(a.shape, lambda i: (0, 0))
    return pl.pallas_call(
        functools.partial(_merge_kernel, alpha=alpha),
        grid=(t // tm,),
        in_specs=[row(ysb.shape[1]), row(ynsa.shape[1]), row(2 * d), row(d),
                  full(wsb), full(wnsa), full(wo), full(lng), full(lnb)],
        out_specs=[row(d), row(d)],
        out_shape=[jax.ShapeDtypeStruct((t, d), F32), jax.ShapeDtypeStruct((t, d), BF16)],
        compiler_params=_cparams(("parallel",)), name="merge_ln1",
    )(ysb, ynsa, g, h, wsb, wnsa, wo, lng, lnb)


def _router_kernel(x_ref, w_ref, b_ref, idx_ref, wt_ref):
    logits = _dot_nt(w_ref[...], x_ref[...]) + b_ref[...]
    n_e, tm = logits.shape
    e_i = lax.broadcasted_iota(jnp.int32, (n_e, tm), 0)
    vals, idxs = [], []
    for _ in range(TOP_K):
        mx = jnp.max(logits, axis=0, keepdims=True)
        first = jnp.min(jnp.where(logits == mx, e_i, n_e), axis=0, keepdims=True)
        vals.append(mx)
        idxs.append(first)
        logits = jnp.where(e_i == first, -jnp.inf, logits)
    ex = [jnp.exp(v - vals[0]) for v in vals]
    inv = 1.0 / (ex[0] + ex[1] + ex[2] + ex[3])
    idx_ref[...] = jnp.concatenate(idxs, axis=0)
    wt_ref[...] = jnp.concatenate([e * inv for e in ex], axis=0)


def _router(h1b, w_t, b_col):
    t, d = h1b.shape
    tm = ROUTER_TM
    return pl.pallas_call(
        _router_kernel,
        grid=(t // tm,),
        in_specs=[pl.BlockSpec((tm, d), lambda i: (i, 0)),
                  pl.BlockSpec(w_t.shape, lambda i: (0, 0)),
                  pl.BlockSpec(b_col.shape, lambda i: (0, 0))],
        out_specs=[pl.BlockSpec((TOP_K, tm), lambda i: (0, i)),
                   pl.BlockSpec((TOP_K, tm), lambda i: (0, i))],
        out_shape=[jax.ShapeDtypeStruct((TOP_K, t), jnp.int32),
                   jax.ShapeDtypeStruct((TOP_K, t), F32)],
        compiler_params=_cparams(("parallel",)), name="router",
    )(h1b, w_t, b_col)


def _expert_kernel(te_ref, nv_ref, x_ref, wgu_ref, bgu_ref, wd_ref, bd_ref, cw_ref, y_ref, wgu_bf, wd_bf):
    i = pl.program_id(0)

    @pl.when(jnp.logical_or(i == 0, te_ref[i] != te_ref[jnp.maximum(i - 1, 0)]))
    def _():
        wgu_bf[...] = wgu_ref[...].astype(BF16)
        wd_bf[...] = wd_ref[...].astype(BF16)

    @pl.when(i < nv_ref[0])
    def _():
        gu = _dot(x_ref[...], wgu_bf[...]) + bgu_ref[...]
        gate = jnp.minimum(gu[:, :D_FF], SWIGLU_LIMIT)
        up = jnp.clip(gu[:, D_FF:], -SWIGLU_LIMIT, SWIGLU_LIMIT)
        act = (up + 1.0) * gate * _sigmoid(SWIGLU_ALPHA * gate)
        y = _dot(act.astype(BF16), wd_bf[...]) + bd_ref[...]
        y_ref[...] = (cw_ref[...] * y).astype(y_ref.dtype)

    @pl.when(i >= nv_ref[0])
    def _():
        y_ref[...] = jnp.zeros_like(y_ref)


def _experts(tile_expert, n_valid, xs, wgu, bgu, wd, bd, cw, layer):
    p, d = xs.shape
    tm = EXPERT_TM
    two_ff = wgu.shape[-1]
    grid_spec = pltpu.PrefetchScalarGridSpec(
        num_scalar_prefetch=2, grid=(p // tm,),
        in_specs=[pl.BlockSpec((tm, d), lambda i, te, nv: (i, 0)),
                  pl.BlockSpec((None, None, d, two_ff), lambda i, te, nv: (layer, te[i], 0, 0)),
                  pl.BlockSpec((None, 1, two_ff), lambda i, te, nv: (te[i], 0, 0)),
                  pl.BlockSpec((None, None, two_ff // 2, d), lambda i, te, nv: (layer, te[i], 0, 0)),
                  pl.BlockSpec((None, 1, d), lambda i, te, nv: (te[i], 0, 0)),
                  pl.BlockSpec((tm, 1), lambda i, te, nv: (i, 0))],
        out_specs=pl.BlockSpec((tm, d), lambda i, te, nv: (i, 0)),
        scratch_shapes=[pltpu.VMEM((d, two_ff), BF16), pltpu.VMEM((two_ff // 2, d), BF16)])
    return pl.pallas_call(
        _expert_kernel, grid_spec=grid_spec,
        out_shape=jax.ShapeDtypeStruct((p, d), F32),
        compiler_params=_cparams(("arbitrary",)), name="experts",
    )(tile_expert, n_valid, xs, wgu, bgu, wd, bd, cw)


def _out_kernel(h1_ref, h1b_ref, yg_ref, p_ref, wpg_ref, wpp_ref, lng_ref, lnb_ref, o_ref, *, alpha):
    moe = (yg_ref[0].astype(F32) + yg_ref[1].astype(F32) + yg_ref[2].astype(F32) + yg_ref[3].astype(F32))
    ple = _sigmoid(_dot(h1b_ref[...], wpg_ref[...])) * _dot(p_ref[...].astype(BF16), wpp_ref[...])
    u = alpha * h1_ref[...] + moe + ple
    o_ref[...] = _layer_norm(u, lng_ref[...], lnb_ref[...])


def _out_ln2(h1, h1b, yg, p, wpg, wpp, lng, lnb, alpha):
    t, d = h1.shape
    tm = OUT_TM
    row = lambda w: pl.BlockSpec((tm, w), lambda i: (i, 0))
    full = lambda a: pl.BlockSpec(a.shape, lambda i: (0, 0))
    return pl.pallas_call(
        functools.partial(_out_kernel, alpha=alpha),
        grid=(t // tm,),
        in_specs=[row(d), row(d), pl.BlockSpec((TOP_K, tm, d), lambda i: (0, i, 0)), row(p.shape[1]),
                  full(wpg), full(wpp), full(lng), full(lnb)],
        out_specs=row(d),
        out_shape=jax.ShapeDtypeStruct((t, d), F32),
        compiler_params=_cparams(("parallel",)), name="out_ln2",
    )(h1, h1b, yg, p, wpg, wpp, lng, lnb)


def _dup_groups(w):
    d = w.shape[0]
    w = w.reshape(d, NSA_KV_GROUPS, 1, HEAD_DIM)
    return jnp.broadcast_to(w, (d, NSA_KV_GROUPS, 2, HEAD_DIM)).reshape(d, NSA_KV_GROUPS * LANES)


def _in_proj_weights(w_in):
    sbw = SB_HEADS * HEAD_DIM
    qw = NSA_HEADS * HEAD_DIM
    kvw = NSA_KV_GROUPS * HEAD_DIM
    d = w_in.shape[0]
    widths = (sbw, sbw, sbw, qw) + (kvw,) * 6 + (3 * NSA_HEADS, d, d)
    parts, start = [], 0
    for wd in widths:
        parts.append(w_in[:, start:start + wd])
        start += wd
    (sb_q, sb_k, sb_v, nsa_q, k_cmp, v_cmp, k_sel, v_sel, k_win, v_win, gate, g_sb, g_nsa) = parts
    kvsw = jnp.concatenate([_dup_groups(k_sel), _dup_groups(v_sel), _dup_groups(k_win), _dup_groups(v_win)], axis=1)
    gate = jnp.pad(gate, ((0, 0), (0, GATE_PAD - gate.shape[1])))
    g_both = jnp.concatenate([g_sb, g_nsa], axis=1)
    ws = [sb_q, sb_k, sb_v, nsa_q, k_cmp, v_cmp, kvsw, gate, g_both]
    dtypes = [BF16, BF16, BF16, BF16, F32, F32, BF16, F32, F32]
    scale = HEAD_DIM ** -0.5
    scales = [scale, 1.0, 1.0, scale, 1.0, 1.0, 1.0, 1.0, 1.0]
    return [w.astype(BF16) for w in ws], dtypes, scales


def _compress_weights(w1, w2, pe):
    eye = jnp.eye(NSA_KV_GROUPS, dtype=w1.dtype)
    w1b = jnp.einsum('ldh,pg->lpdgh', w1, eye).reshape(CMP_BLOCK * NSA_KV_GROUPS * HEAD_DIM,
                                                       NSA_KV_GROUPS * CMP_HIDDEN)
    w2b = jnp.einsum('hd,pg,r->phgrd', w2, eye, jnp.ones((2,), w2.dtype)).reshape(
        NSA_KV_GROUPS * CMP_HIDDEN, NSA_KV_GROUPS * LANES)
    peb = jnp.broadcast_to(pe[:, None, :], (CMP_BLOCK, NSA_KV_GROUPS, HEAD_DIM)).reshape(1, -1)
    return peb.astype(F32), w1b.astype(BF16), w2b.astype(BF16)


def _moe_plan(idx_t, wt_t, tm):
    k, t = idx_t.shape
    n = k * t
    i32 = jnp.int32
    experts = jnp.arange(N_EXPERTS, dtype=i32)
    a_ids = jnp.arange(n, dtype=i32)
    sorted_e, order, sorted_w = lax.sort((idx_t.reshape(n), a_ids, wt_t.reshape(n)), num_keys=1, is_stable=True)
    grp_end = jnp.sum((sorted_e[None, :] <= experts[:, None]).astype(i32), axis=1)
    counts = grp_end - jnp.concatenate([jnp.zeros((1,), i32), grp_end[:-1]])
    padded = ((counts + tm - 1) // tm) * tm
    pad_end = jnp.cumsum(padded)
    pad_off = pad_end - padded
    gap = padded - counts
    row_of_sorted = a_ids + jnp.sum(jnp.where(a_ids[:, None] >= grp_end[None, :], gap[None, :], 0), axis=1)
    _, pos = lax.sort((order, row_of_sorted), num_keys=1)
    p_rows = n + N_EXPERTS * tm
    rows = jnp.arange(p_rows, dtype=i32)
    past = rows[:, None] >= pad_end[None, :]
    sorted_of_row = rows - jnp.sum(jnp.where(past, gap[None, :], 0), axis=1)
    row_grp_end = jnp.sum(jnp.where(past, counts[None, :], 0), axis=1) + jnp.sum(
        jnp.where((rows[:, None] >= pad_off[None, :]) & ~past, counts[None, :], 0), axis=1)
    real = sorted_of_row < row_grp_end
    src = jnp.minimum(sorted_of_row, n - 1)
    take = lambda a: a.at[src].get(mode='promise_in_bounds')
    src_tok = jnp.where(real, take(order) % t, 0)
    cw = jnp.where(real, take(sorted_w), 0.0)
    tile_start = jnp.arange(p_rows // tm, dtype=i32) * tm
    tile_expert = jnp.minimum(jnp.sum((tile_start[:, None] >= pad_end[None, :]).astype(i32), axis=1), N_EXPERTS - 1)
    n_valid = (pad_end[-1] // tm).astype(i32).reshape(1)
    return src_tok, cw.reshape(p_rows, 1), pos, tile_expert, n_valid


def _layer(h, p_i, w_in, w_cmp1, w_cmp2, pe_cmp, w_br_sb, w_br_nsa, w_o, ln1_g, ln1_b, ln2_g, ln2_b,
           w_router, b_router, w_gu_all, b_gu, w_down_all, b_down, w_ple_gate, w_ple_proj,
           *, batch, alpha, layer):
    t, d = h.shape
    s = t // batch
    ws, dtypes, scales = _in_proj_weights(w_in)
    sb_q, sb_k, sb_v, nsa_q, k_cmp, v_cmp, kvsw, gate, g_both = _in_proj(h, ws, dtypes, scales)
    b3 = lambda a: a.reshape(batch, s, a.shape[1])

    y_sb = _sb_attention(b3(sb_q), b3(sb_k), b3(sb_v)).reshape(t, -1)

    nc = s // CMP_BLOCK
    cmp_rows = lambda a: a.reshape(batch, nc // 2, 2 * CMP_BLOCK * a.shape[1])
    kc = _compress(cmp_rows(k_cmp), *_compress_weights(w_cmp1[0], w_cmp2[0], pe_cmp[0]))
    vc = _compress(cmp_rows(v_cmp), *_compress_weights(w_cmp1[1], w_cmp2[1], pe_cmp[1]))
    y_cmp, sel, flags = _nsa_cmp(b3(nsa_q), kc, vc, b3(gate))
    y_nsa = _nsa_main(b3(nsa_q), b3(kvsw), sel, flags, b3(gate), y_cmp).reshape(t, -1)

    row = lambda v: v.reshape(1, -1).astype(F32)
    h1, h1b = _merge_ln1(y_sb, y_nsa, g_both, h, w_br_sb.astype(BF16), w_br_nsa.astype(BF16),
                         w_o.astype(BF16), row(ln1_g), row(ln1_b), alpha)

    idx_t, wt_t = _router(h1b, w_router.T.astype(BF16), b_router.reshape(-1, 1).astype(F32))
    src_tok, cw, pos, tile_expert, n_valid = _moe_plan(idx_t, wt_t, EXPERT_TM)
    xs = h1b.at[src_tok].get(mode='promise_in_bounds')
    y = _experts(tile_expert, n_valid, xs, w_gu_all, b_gu.reshape(N_EXPERTS, 1, -1),
                 w_down_all, b_down.reshape(N_EXPERTS, 1, -1), cw, layer)
    yg = y.at[pos].get(mode='promise_in_bounds').reshape(TOP_K, t, d)
    return _out_ln2(h1, h1b, yg, p_i, w_ple_gate.astype(BF16), w_ple_proj.astype(BF16),
                    row(ln2_g), row(ln2_b), alpha)


def kernel(x, p, w_in, w_cmp1, w_cmp2, pe_cmp, w_br_sb, w_br_nsa, w_o, ln1_g, ln1_b, ln2_g, ln2_b,
           w_router, b_router, w_gu, b_gu, w_down, b_down, w_ple_gate, w_ple_proj):
    batch, s, d = x.shape
    depth = w_in.shape[0]
    alpha = (2 * depth) ** 0.25
    h = x.reshape(batch * s, d)
    for i in range(depth):
        h = _layer(h, p[i].reshape(batch * s, -1), w_in[i], w_cmp1[i], w_cmp2[i], pe_cmp[i],
                   w_br_sb[i], w_br_nsa[i], w_o[i], ln1_g[i], ln1_b[i], ln2_g[i], ln2_b[i],
                   w_router[i], b_router[i], w_gu, b_gu[i], w_down, b_down[i],
                   w_ple_gate[i], w_ple_proj[i], batch=batch, alpha=alpha, layer=i)
    return h.reshape(batch, s, d)
```

```python
import functools

import jax
import jax.numpy as jnp
from jax import lax
from jax.experimental import pallas as pl
from jax.experimental.pallas import tpu as pltpu

F32 = jnp.float32
BF16 = jnp.bfloat16

HEAD_DIM = 64
LANES = 128
SB_HEADS = 8
NSA_HEADS = 8
NSA_KV_GROUPS = 2
NSA_GROUP = NSA_HEADS // NSA_KV_GROUPS
CMP_BLOCK = 32
CMP_HIDDEN = 128
SEL_BLOCK = 64
N_SEL = 16
WINDOW = 512
N_EXPERTS = 32
TOP_K = 4
D_FF = 1024
SWIGLU_ALPHA = 1.702
SWIGLU_LIMIT = 7.0
LN_EPS = 1e-5
NEG_INF = -1e30
FORCED_SCORE = 1e6
GATE_PAD = 128
MASK_BIG = 2.0 ** 100
FEAT_HI, FEAT_LO = 64, 65
SB_DEAD_LOG = -104.0

VMEM_LIMIT = 56 * 1024 * 1024

ATTN_TQ = 128
SEL_TK = 512
PROJ_TM = 256
MERGE_TM = 256
ROUTER_TM = 512
EXPERT_TM = 512
OUT_TM = 256


def _cparams(sem):
    return pltpu.CompilerParams(dimension_semantics=sem, vmem_limit_bytes=VMEM_LIMIT)


def _sigmoid(x):
    return 1.0 / (1.0 + jnp.exp(-x))


def _div_pow2(x, n):
    assert n & (n - 1) == 0
    return lax.shift_right_logical(x, n.bit_length() - 1)


def _mod_pow2(x, n):
    assert n & (n - 1) == 0
    return x & (n - 1)


def _split_bf16(x):
    hi = x.astype(BF16)
    lo = (x - hi.astype(F32)).astype(BF16)
    return hi, lo


def _dot(a, b):
    return jnp.dot(a, b, preferred_element_type=F32)


def _dot_nt(a, b):
    return lax.dot_general(a, b, (((1,), (1,)), ((), ())), preferred_element_type=F32)


def _dot_tn(a, b):
    return lax.dot_general(a, b, (((0,), (0,)), ((), ())), preferred_element_type=F32)


def _layer_norm(u, g, b):
    mu = jnp.mean(u, axis=-1, keepdims=True)
    d = u - mu
    var = jnp.mean(d * d, axis=-1, keepdims=True)
    return d * lax.rsqrt(var + LN_EPS) * g + b


def _in_proj_kernel(x_ref, *refs, scales, transposed):
    n = len(scales)
    x = x_ref[...].astype(BF16)
    for w_ref, o_ref, s, tr in zip(refs[:n], refs[n:], scales, transposed):
        acc = _dot_nt(w_ref[...], x) if tr else _dot(x, w_ref[...])
        if s != 1.0:
            acc = acc * s
        o_ref[...] = acc.astype(o_ref.dtype)


def _in_proj(h, weights, dtypes, scales, transposed):
    t, d = h.shape
    tm = PROJ_TM
    in_specs = [pl.BlockSpec((tm, d), lambda i: (i, 0))]
    in_specs += [pl.BlockSpec(w.shape, lambda i: (0, 0)) for w in weights]
    out_specs, out_shape = [], []
    for w, dt, tr in zip(weights, dtypes, transposed):
        if tr:
            out_specs.append(pl.BlockSpec((w.shape[0], tm), lambda i: (0, i)))
            out_shape.append(jax.ShapeDtypeStruct((w.shape[0], t), dt))
        else:
            out_specs.append(pl.BlockSpec((tm, w.shape[1]), lambda i: (i, 0)))
            out_shape.append(jax.ShapeDtypeStruct((t, w.shape[1]), dt))
    return pl.pallas_call(
        functools.partial(_in_proj_kernel, scales=tuple(scales), transposed=tuple(transposed)),
        grid=(t // tm,), in_specs=in_specs, out_specs=out_specs, out_shape=out_shape,
        compiler_params=_cparams(("parallel",)), name="in_proj",
    )(h, *weights)


def _stack_heads(q2, tq):
    lane = lax.broadcasted_iota(jnp.int32, (tq, LANES), 1)
    zero = jnp.zeros_like(q2)
    return jnp.concatenate([jnp.where(lane < HEAD_DIM, q2, zero),
                            jnp.where(lane >= HEAD_DIM, q2, zero)], axis=0)


def _unstack_heads(o, tq):
    lane = lax.broadcasted_iota(jnp.int32, (tq, LANES), 1)
    return jnp.where(lane < HEAD_DIM, o[:tq], o[tq:])


def _sb_kernel(q_ref, k_ref, v_ref, o_ref, *, tq, n_pairs):
    i = pl.program_id(1)
    r_i = lax.broadcasted_iota(jnp.int32, (tq, tq), 0)
    c_i = lax.broadcasted_iota(jnp.int32, (tq, tq), 1)
    tri = jnp.where(r_i > c_i, 1.0, 0.0).astype(BF16)
    rhs = jnp.concatenate([tri, jnp.ones((tq, tq), BF16)], axis=1)
    rhs = jnp.concatenate([rhs, rhs], axis=0)
    r2 = lax.broadcasted_iota(jnp.int32, (2 * tq, tq), 0) & (tq - 1)
    causal = lax.broadcasted_iota(jnp.int32, (2 * tq, tq), 1) < r2
    qs = [_stack_heads(q_ref[:, p * LANES:(p + 1) * LANES], tq) for p in range(n_pairs)]

    def tile(p, j, carry, acc, diag):
        start = pl.multiple_of(j * tq, tq)
        kj = k_ref[pl.ds(start, tq), p * LANES:(p + 1) * LANES]
        vj = v_ref[pl.ds(start, tq), p * LANES:(p + 1) * LANES]
        z = _dot_nt(qs[p], kj)
        ls = -(jnp.maximum(z, 0.0) + jnp.log(1.0 + jnp.exp(-jnp.abs(z))))
        if diag:
            ls = jnp.where(causal, ls, 0.0)
        hi, lo = _split_bf16(ls)
        sums = _dot(jnp.concatenate([hi, lo], axis=1), rhs)
        w = jnp.exp(z + ls + sums[:, :tq] + carry)
        if diag:
            w = jnp.where(causal, w, 0.0)
        return carry + sums[:, tq:], acc + _dot(w.astype(BF16), vj)

    def live(carries):
        worst = functools.reduce(jnp.maximum, carries)
        return jnp.max(worst) > SB_DEAD_LOG

    zeros = jnp.zeros((2 * tq, LANES), F32)
    state = [tile(p, i, zeros, zeros, True) for p in range(n_pairs)]
    carries = tuple(s[0] for s in state)
    accs = tuple(s[1] for s in state)

    def cond(c):
        return jnp.logical_and(c[0] <= i, c[1])

    def body(c):
        step, _, carries, accs = c
        state = [tile(p, i - step, carries[p], accs[p], False) for p in range(n_pairs)]
        carries = tuple(s[0] for s in state)
        return step + 1, live(carries), carries, tuple(s[1] for s in state)

    _, _, _, accs = lax.while_loop(cond, body, (jnp.int32(1), live(carries), carries, accs))
    for p in range(n_pairs):
        o_ref[:, p * LANES:(p + 1) * LANES] = _unstack_heads(accs[p], tq).astype(o_ref.dtype)


def _sb_attention(q, k, v):
    b, s, w = q.shape
    tq = ATTN_TQ
    return pl.pallas_call(
        functools.partial(_sb_kernel, tq=tq, n_pairs=w // LANES),
        grid=(b, s // tq),
        in_specs=[pl.BlockSpec((None, tq, w), lambda bi, i: (bi, i, 0)),
                  pl.BlockSpec((None, s, w), lambda bi, i: (bi, 0, 0)),
                  pl.BlockSpec((None, s, w), lambda bi, i: (bi, 0, 0))],
        out_specs=pl.BlockSpec((None, tq, w), lambda bi, i: (bi, i, 0)),
        out_shape=jax.ShapeDtypeStruct((b, s, w), BF16),
        compiler_params=_cparams(("parallel", "arbitrary")), name="sb_attn",
    )(q, k, v)


def _compress_kernel(x_ref, pe_ref, w1_ref, w2_ref, o_ref):
    x = (x_ref[...] + pe_ref[...]).astype(BF16)
    hid = _dot(x, w1_ref[...])
    hid = hid * _sigmoid(hid)
    o_ref[...] = _dot(hid.astype(BF16), w2_ref[...]).astype(o_ref.dtype)


def _compress(x, pe, w1, w2):
    b, half, width = x.shape
    blk = width // 2
    out = pl.pallas_call(
        _compress_kernel,
        grid=(b, 2),
        in_specs=[pl.BlockSpec((None, half, blk), lambda bi, par: (bi, 0, par)),
                  pl.BlockSpec(pe.shape, lambda bi, par: (0, 0)),
                  pl.BlockSpec(w1.shape, lambda bi, par: (0, 0)),
                  pl.BlockSpec(w2.shape, lambda bi, par: (0, 0))],
        out_specs=pl.BlockSpec((None, None, half, w2.shape[1]), lambda bi, par: (bi, par, 0, 0)),
        out_shape=jax.ShapeDtypeStruct((b, 2, half, w2.shape[1]), BF16),
        compiler_params=_cparams(("parallel", "parallel")), name="compress",
    )(x, pe, w1, w2)
    return out.reshape(b, 2 * half, w2.shape[1])


def _stack_group(q, tq):
    return jnp.concatenate([_stack_heads(q[:, :LANES], tq), _stack_heads(q[:, LANES:], tq)], axis=0)


def _unstack_group(o, tq):
    return jnp.concatenate([_unstack_heads(o[:2 * tq], tq), _unstack_heads(o[2 * tq:], tq)], axis=1)


def _alibi_slope(head_idx):
    return lax.bitcast_convert_type(lax.shift_left(126 - head_idx, 23), F32)


def _expand_gate(gate_sig, g, branch):
    c_i = lax.broadcasted_iota(jnp.int32, (GATE_PAD, 2 * LANES), 0)
    l_i = lax.broadcasted_iota(jnp.int32, (GATE_PAD, 2 * LANES), 1)
    col = g * (3 * NSA_GROUP) + _div_pow2(l_i, HEAD_DIM) * 3 + branch
    onehot = jnp.where(c_i == col, 1.0, 0.0).astype(BF16)
    hi, lo = _split_bf16(gate_sig)
    return _dot(hi, onehot) + _dot(lo, onehot)


def _nsa_cmp_kernel(q_ref, kc_ref, vc_ref, gate_ref, y_ref, sel_ref, flag_ref, *, tq, n_sel):
    g = pl.program_id(1)
    i = pl.program_id(2)
    nc = kc_ref.shape[0]
    half = nc // 2
    qst = _stack_group(q_ref[...], tq)
    s = _dot_nt(kc_ref[...], qst)
    r_i = lax.broadcasted_iota(jnp.int32, (nc, 4 * tq), 0)
    l_i = lax.broadcasted_iota(jnp.int32, (nc, 4 * tq), 1)
    cblk = 2 * _mod_pow2(r_i, half) + _div_pow2(r_i, half)
    c_end = cblk * CMP_BLOCK + (CMP_BLOCK - 1)
    t = i * tq + _mod_pow2(l_i, tq)
    dist = t - c_end
    slope = _alibi_slope(g * NSA_GROUP + _div_pow2(l_i, tq))
    s = s - slope * dist.astype(F32)
    mask = dist >= 0
    m = jnp.max(jnp.where(mask, s, NEG_INF), axis=0, keepdims=True)
    p = jnp.where(mask, jnp.exp(s - m), 0.0)
    l = jnp.sum(p, axis=0, keepdims=True)
    p = p * jnp.where(l > 0.0, 1.0 / l, 0.0)

    o = _dot_tn(p.astype(BF16), vc_ref[...])
    gate = _expand_gate(_sigmoid(gate_ref[...]), g, 0)
    y_ref[...] = (_unstack_group(o, tq) * gate).astype(y_ref.dtype)

    imp_c = p[:, 0:tq] + p[:, tq:2 * tq] + p[:, 2 * tq:3 * tq] + p[:, 3 * tq:4 * tq]
    imp = imp_c[:half] + imp_c[half:]
    n_blk = half
    blk = lax.broadcasted_iota(jnp.int32, (n_blk, tq), 0)
    tt = i * tq + lax.broadcasted_iota(jnp.int32, (n_blk, tq), 1)
    cur = _div_pow2(tt, SEL_BLOCK)
    score = jnp.where(blk * SEL_BLOCK <= tt, imp, NEG_INF)
    for forced_blk in (0, cur, cur - 1):
        score = jnp.where(blk == forced_blk, FORCED_SCORE, score)

    taken = -jnp.inf
    sel = jnp.zeros((n_blk, tq), F32)
    for _ in range(n_sel):
        mx = jnp.max(score, axis=0, keepdims=True)
        first = jnp.min(jnp.where(score == mx, blk, n_blk), axis=0, keepdims=True)
        hit = blk == first
        sel = jnp.where(hit, 1.0, sel)
        score = jnp.where(hit, taken, score)
    sel = jnp.concatenate([sel, jnp.zeros((LANES - n_blk, tq), F32)], axis=0)
    sel_ref[...] = sel.T
    picks = _dot_nt(jnp.ones((8, tq), BF16), sel.astype(BF16))
    flag_ref[...] = picks[0:1].astype(jnp.int32)


def _nsa_cmp(q, kc, vc, gate):
    b, s, _ = q.shape
    tq = ATTN_TQ
    nc = kc.shape[1]
    n_blk = nc // 2
    assert n_blk <= LANES
    n_sel = min(N_SEL, n_blk)
    return pl.pallas_call(
        functools.partial(_nsa_cmp_kernel, tq=tq, n_sel=n_sel),
        grid=(b, NSA_KV_GROUPS, s // tq),
        in_specs=[pl.BlockSpec((None, tq, 2 * LANES), lambda bi, g, i: (bi, i, g)),
                  pl.BlockSpec((None, nc, LANES), lambda bi, g, i: (bi, 0, g)),
                  pl.BlockSpec((None, nc, LANES), lambda bi, g, i: (bi, 0, g)),
                  pl.BlockSpec((None, tq, GATE_PAD), lambda bi, g, i: (bi, i, 0))],
        out_specs=[pl.BlockSpec((None, tq, 2 * LANES), lambda bi, g, i: (bi, i, g)),
                   pl.BlockSpec((None, None, tq, LANES), lambda bi, g, i: (bi, g, i, 0)),
                   pl.BlockSpec((None, None, None, 1, LANES), lambda bi, g, i: (bi, g, i, 0, 0))],
        out_shape=[jax.ShapeDtypeStruct((b, s, 4 * LANES), F32),
                   jax.ShapeDtypeStruct((b, NSA_KV_GROUPS, s, LANES), F32),
                   jax.ShapeDtypeStruct((b, NSA_KV_GROUPS, s // tq, 1, LANES), jnp.int32)],
        compiler_params=_cparams(("parallel", "parallel", "parallel")), name="nsa_cmp",
    )(q, kc, vc, gate)


def _key_features(s, tk, wk):
    def table(n, period, with_blocks):
        key = lax.broadcasted_iota(jnp.int32, (n, LANES), 0)
        lane = lax.broadcasted_iota(jnp.int32, (n, LANES), 1)
        r = key % period
        pos = jnp.where(lane == FEAT_HI, r // SEL_BLOCK, jnp.where(lane == FEAT_LO, r % SEL_BLOCK, 0))
        if with_blocks:
            pos = jnp.where(lane < SEL_BLOCK, (key // SEL_BLOCK == lane).astype(jnp.int32), pos)
        return pos.astype(BF16)
    return table(s, tk, True), table(wk, wk, False)


def _nsa_main_kernel(flag_ref, q_ref, ks_ref, kw_ref, vs_ref, vw_ref, kfeat_ref, rfeat_ref, sel_ref,
                     gate_ref, ycmp_ref, o_ref, m_ref, l_ref, acc_ref, *, tq, tk):
    g = pl.program_id(1)
    i = pl.program_id(2)
    cols = NSA_GROUP * tq
    qs = pl.multiple_of(i * tq, tq)
    qst = _stack_group(q_ref[...], tq)
    col = lax.broadcasted_iota(jnp.int32, (1, cols), 1)
    t_loc = _mod_pow2(col, tq)
    slope_row = _alibi_slope(g * NSA_GROUP + _div_pow2(col, tq))

    lane = lax.broadcasted_iota(jnp.int32, (tq, LANES), 1)
    first_blk = i * (tq // SEL_BLOCK)
    open_blk = jnp.where(lane < first_blk, sel_ref[...], 0.0) > 0.5
    blk_add = jnp.where(open_blk, 0.0, -MASK_BIG)
    f_sel, f_win = [], []
    for h in range(NSA_GROUP):
        slope = _alibi_slope(jnp.full((tq, LANES), g * NSA_GROUP + h, jnp.int32))
        pos = jnp.where(lane == FEAT_HI, slope * SEL_BLOCK, jnp.where(lane == FEAT_LO, slope, 0.0))
        f_win.append(pos)
        f_sel.append(jnp.where(lane < SEL_BLOCK, blk_add, pos))
    q_sel = jnp.concatenate([qst, jnp.concatenate(f_sel, axis=0).astype(BF16)], axis=1)
    q_win = jnp.concatenate([qst, jnp.concatenate(f_win, axis=0).astype(BF16)], axis=1)

    def online(k, q_aug, mask_add, bias, v_t):
        s = _dot_nt(k, q_aug)
        if mask_add is not None:
            s = s + mask_add
        m = m_ref[...]
        m_new = jnp.maximum(m, jnp.max(s, axis=0, keepdims=True) + bias)
        alpha = jnp.exp(m - m_new)
        p = jnp.exp(s + (bias - m_new))
        m_ref[...] = m_new
        l_ref[...] = alpha * l_ref[...] + jnp.sum(p, axis=0, keepdims=True)
        acc_ref[...] = alpha * acc_ref[...] + _dot(v_t, p.astype(BF16))

    def untranspose(o_t):
        return jnp.concatenate([o_t[:, h * tq:(h + 1) * tq].T for h in range(NSA_GROUP)], axis=0)

    r_i = lax.broadcasted_iota(jnp.int32, (tq, tq), 0)
    c_i = lax.broadcasted_iota(jnp.int32, (tq, tq), 1)
    causal_add = jnp.where(r_i <= c_i, 0.0, -MASK_BIG)
    causal_add = jnp.concatenate([causal_add] * NSA_GROUP, axis=1)
    m_ref[...] = jnp.full((1, cols), NEG_INF, F32)
    l_ref[...] = jnp.zeros((1, cols), F32)
    acc_ref[...] = jnp.zeros((LANES, cols), F32)
    k = jnp.concatenate([ks_ref[pl.ds(qs, tq), :], rfeat_ref[0:tq, :]], axis=1)
    online(k, q_sel, causal_add, -slope_row * t_loc.astype(F32), vs_ref[:, pl.ds(qs, tq)])

    def bulk(kt, _):
        start = pl.multiple_of(kt * tk, tk)
        picked = flag_ref[0, kt * (tk // SEL_BLOCK)]
        for j in range(1, tk // SEL_BLOCK):
            picked = picked + flag_ref[0, kt * (tk // SEL_BLOCK) + j]

        @pl.when(picked > 0)
        def _():
            k = jnp.concatenate([ks_ref[pl.ds(start, tk), :], kfeat_ref[pl.ds(start, tk), :]], axis=1)
            bias = -slope_row * (qs - start + t_loc).astype(F32)
            online(k, q_sel, None, bias, vs_ref[:, pl.ds(start, tk)])

        return 0

    lax.fori_loop(0, (qs + tk - 1) // tk, bulk, 0)
    o_sel = untranspose(acc_ref[...] * (1.0 / l_ref[...]))

    wk = WINDOW + tq
    start = pl.multiple_of(jnp.maximum(qs - WINDOW, 0), tq)
    w_r = lax.broadcasted_iota(jnp.int32, (wk, tq), 0)
    w_c = lax.broadcasted_iota(jnp.int32, (wk, tq), 1)
    dist = (qs - start) + w_c - w_r
    band_add = jnp.where(jnp.where(dist >= 0, dist, WINDOW) < WINDOW, 0.0, -MASK_BIG)
    band_add = jnp.concatenate([band_add] * NSA_GROUP, axis=1)
    k = jnp.concatenate([kw_ref[pl.ds(start, wk), :], rfeat_ref[...]], axis=1)
    s = _dot_nt(k, q_win) + band_add
    p = jnp.exp(s - jnp.max(s, axis=0, keepdims=True))
    l_w = jnp.sum(p, axis=0, keepdims=True)
    o_win = untranspose(_dot(vw_ref[:, pl.ds(start, wk)], p.astype(BF16)) * (1.0 / l_w))

    gate = _sigmoid(gate_ref[...])
    y = (ycmp_ref[...] + _expand_gate(gate, g, 1) * _unstack_group(o_sel, tq)
         + _expand_gate(gate, g, 2) * _unstack_group(o_win, tq))
    o_ref[...] = y.astype(o_ref.dtype)


def _nsa_main(q, k_sw, v_sw_t, sel, flags, gate, ycmp):
    b, s, _ = q.shape
    tq = ATTN_TQ
    tk = min(SEL_TK, s)
    wk = WINDOW + tq
    cols = NSA_GROUP * tq
    assert s >= wk and s % tk == 0 and s // SEL_BLOCK <= SEL_BLOCK
    kfeat, rfeat = _key_features(s, tk, wk)
    k_spec = lambda off: pl.BlockSpec((None, s, LANES), lambda bi, g, i: (bi, 0, off + g))
    v_spec = lambda off: pl.BlockSpec((LANES, s), lambda bi, g, i: (off + g, bi))
    return pl.pallas_call(
        functools.partial(_nsa_main_kernel, tq=tq, tk=tk),
        grid=(b, NSA_KV_GROUPS, s // tq),
        in_specs=[pl.BlockSpec((None, None, None, 1, LANES), lambda bi, g, i: (bi, g, i, 0, 0),
                               memory_space=pltpu.SMEM),
                  pl.BlockSpec((None, tq, 2 * LANES), lambda bi, g, i: (bi, i, g)),
                  k_spec(0), k_spec(2), v_spec(0), v_spec(2),
                  pl.BlockSpec((s, LANES), lambda bi, g, i: (0, 0)),
                  pl.BlockSpec((wk, LANES), lambda bi, g, i: (0, 0)),
                  pl.BlockSpec((None, None, tq, LANES), lambda bi, g, i: (bi, g, i, 0)),
                  pl.BlockSpec((None, tq, GATE_PAD), lambda bi, g, i: (bi, i, 0)),
                  pl.BlockSpec((None, tq, 2 * LANES), lambda bi, g, i: (bi, i, g))],
        out_specs=pl.BlockSpec((None, tq, 2 * LANES), lambda bi, g, i: (bi, i, g)),
        out_shape=jax.ShapeDtypeStruct((b, s, 4 * LANES), BF16),
        scratch_shapes=[pltpu.VMEM((1, cols), F32), pltpu.VMEM((1, cols), F32), pltpu.VMEM((LANES, cols), F32)],
        compiler_params=_cparams(("parallel", "parallel", "arbitrary")), name="nsa_main",
    )(flags, q, k_sw, k_sw, v_sw_t, v_sw_t, kfeat, rfeat, sel, gate, ycmp)


def _merge_kernel(ysb_ref, ynsa_ref, g_ref, h_ref, wsb_ref, wnsa_ref, wo_ref, lng_ref, lnb_ref,
                  h1_ref, h1b_ref, *, alpha):
    d = h_ref.shape[1]
    g = g_ref[...]
    merged = (_sigmoid(g[:, :d]) * _dot(ysb_ref[...], wsb_ref[...])
              + _sigmoid(g[:, d:]) * _dot(ynsa_ref[...], wnsa_ref[...]))
    u = alpha * h_ref[...] + _dot(merged.astype(BF16), wo_ref[...])
    h1 = _layer_norm(u, lng_ref[...], lnb_ref[...])
    h1_ref[...] = h1
    h1b_ref[...] = h1.astype(BF16)


def _merge_ln1(ysb, ynsa, g, h, wsb, wnsa, wo, lng, lnb, alpha):
    t, d = h.shape
    tm = MERGE_TM
    row = lambda w: pl.BlockSpec((tm, w), lambda i: (i, 0))
    full = lambda a: pl.BlockSpec(a.shape, lambda i: (0, 0))
    return pl.pallas_call(
        functools.partial(_merge_kernel, alpha=alpha),
        grid=(t // tm,),
        in_specs=[row(ysb.shape[1]), row(ynsa.shape[1]), row(2 * d), row(d),
                  full(wsb), full(wnsa), full(wo), full(lng), full(lnb)],
        out_specs=[row(d), row(d)],
        out_shape=[jax.ShapeDtypeStruct((t, d), F32), jax.ShapeDtypeStruct((t, d), BF16)],
        compiler_params=_cparams(("parallel",)), name="merge_ln1",
    )(ysb, ynsa, g, h, wsb, wnsa, wo, lng, lnb)


def _router_kernel(x_ref, w_ref, b_ref, idx_ref, wt_ref):
    logits = _dot_nt(w_ref[...], x_ref[...]) + b_ref[...]
    n_e, tm = logits.shape
    e_i = lax.broadcasted_iota(jnp.int32, (n_e, tm), 0)
    vals, idxs = [], []
    for _ in range(TOP_K):
        mx = jnp.max(logits, axis=0, keepdims=True)
        first = jnp.min(jnp.where(logits == mx, e_i, n_e), axis=0, keepdims=True)
        vals.append(mx)
        idxs.append(first)
        logits = jnp.where(e_i == first, -jnp.inf, logits)
    ex = [jnp.exp(v - vals[0]) for v in vals]
    inv = 1.0 / (ex[0] + ex[1] + ex[2] + ex[3])
    idx_ref[...] = jnp.concatenate(idxs, axis=0)
    wt_ref[...] = jnp.concatenate([e * inv for e in ex], axis=0)


def _router(h1b, w_t, b_col):
    t, d = h1b.shape
    tm = ROUTER_TM
    return pl.pallas_call(
        _router_kernel,
        grid=(t // tm,),
        in_specs=[pl.BlockSpec((tm, d), lambda i: (i, 0)),
                  pl.BlockSpec(w_t.shape, lambda i: (0, 0)),
                  pl.BlockSpec(b_col.shape, lambda i: (0, 0))],
        out_specs=[pl.BlockSpec((TOP_K, tm), lambda i: (0, i)),
                   pl.BlockSpec((TOP_K, tm), lambda i: (0, i))],
        out_shape=[jax.ShapeDtypeStruct((TOP_K, t), jnp.int32),
                   jax.ShapeDtypeStruct((TOP_K, t), F32)],
        compiler_params=_cparams(("parallel",)), name="router",
    )(h1b, w_t, b_col)


def _expert_kernel(te_ref, nv_ref, x_ref, wgu_ref, bgu_ref, wd_ref, bd_ref, cw_ref, y_ref, wgu_bf, wd_bf):
    i = pl.program_id(0)

    @pl.when(jnp.logical_or(i == 0, te_ref[i] != te_ref[jnp.maximum(i - 1, 0)]))
    def _():
        wgu_bf[...] = wgu_ref[...].astype(BF16)
        wd_bf[...] = wd_ref[...].astype(BF16)

    @pl.when(i < nv_ref[0])
    def _():
        gu = _dot(x_ref[...], wgu_bf[...]) + bgu_ref[...]
        gate = jnp.minimum(gu[:, :D_FF], SWIGLU_LIMIT)
        up = jnp.clip(gu[:, D_FF:], -SWIGLU_LIMIT, SWIGLU_LIMIT)
        act = (up + 1.0) * gate * _sigmoid(SWIGLU_ALPHA * gate)
        y = _dot(act.astype(BF16), wd_bf[...]) + bd_ref[...]
        y_ref[...] = (cw_ref[...] * y).astype(y_ref.dtype)

    @pl.when(i >= nv_ref[0])
    def _():
        y_ref[...] = jnp.zeros_like(y_ref)


def _experts(tile_expert, n_valid, xs, wgu, bgu, wd, bd, cw, layer):
    p, d = xs.shape
    tm = EXPERT_TM
    two_ff = wgu.shape[-1]
    grid_spec = pltpu.PrefetchScalarGridSpec(
        num_scalar_prefetch=2, grid=(p // tm,),
        in_specs=[pl.BlockSpec((tm, d), lambda i, te, nv: (i, 0)),
                  pl.BlockSpec((None, None, d, two_ff), lambda i, te, nv: (layer, te[i], 0, 0)),
                  pl.BlockSpec((None, 1, two_ff), lambda i, te, nv: (te[i], 0, 0)),
                  pl.BlockSpec((None, None, two_ff // 2, d), lambda i, te, nv: (layer, te[i], 0, 0)),
                  pl.BlockSpec((None, 1, d), lambda i, te, nv: (te[i], 0, 0)),
                  pl.BlockSpec((tm, 1), lambda i, te, nv: (i, 0))],
        out_specs=pl.BlockSpec((tm, d), lambda i, te, nv: (i, 0)),
        scratch_shapes=[pltpu.VMEM((d, two_ff), BF16), pltpu.VMEM((two_ff // 2, d), BF16)])
    return pl.pallas_call(
        _expert_kernel, grid_spec=grid_spec,
        out_shape=jax.ShapeDtypeStruct((p, d), F32),
        compiler_params=_cparams(("arbitrary",)), name="experts",
    )(tile_expert, n_valid, xs, wgu, bgu, wd, bd, cw)


def _out_kernel(h1_ref, h1b_ref, yg_ref, p_ref, wpg_ref, wpp_ref, lng_ref, lnb_ref, o_ref, *, alpha):
    moe = (yg_ref[0].astype(F32) + yg_ref[1].astype(F32) + yg_ref[2].astype(F32) + yg_ref[3].astype(F32))
    ple = _sigmoid(_dot(h1b_ref[...], wpg_ref[...])) * _dot(p_ref[...].astype(BF16), wpp_ref[...])
    u = alpha * h1_ref[...] + moe + ple
    o_ref[...] = _layer_norm(u, lng_ref[...], lnb_ref[...])


def _out_ln2(h1, h1b, yg, p, wpg, wpp, lng, lnb, alpha):
    t, d = h1.shape
    tm = OUT_TM
    row = lambda w: pl.BlockSpec((tm, w), lambda i: (i, 0))
    full = lambda a: pl.BlockSpec(a.shape, lambda i: (0, 0))
    return pl.pallas_call(
        functools.partial(_out_kernel, alpha=alpha),
        grid=(t // tm,),
        in_specs=[row(d), row(d), pl.BlockSpec((TOP_K, tm, d), lambda i: (0, i, 0)), row(p.shape[1]),
                  full(wpg), full(wpp), full(lng), full(lnb)],
        out_specs=row(d),
        out_shape=jax.ShapeDtypeStruct((t, d), F32),
        compiler_params=_cparams(("parallel",)), name="out_ln2",
    )(h1, h1b, yg, p, wpg, wpp, lng, lnb)


def _dup_groups(w):
    d = w.shape[0]
    w = w.reshape(d, NSA_KV_GROUPS, 1, HEAD_DIM)
    return jnp.broadcast_to(w, (d, NSA_KV_GROUPS, 2, HEAD_DIM)).reshape(d, NSA_KV_GROUPS * LANES)


def _in_proj_weights(w_in):
    sbw = SB_HEADS * HEAD_DIM
    qw = NSA_HEADS * HEAD_DIM
    kvw = NSA_KV_GROUPS * HEAD_DIM
    d = w_in.shape[0]
    widths = (sbw, sbw, sbw, qw) + (kvw,) * 6 + (3 * NSA_HEADS, d, d)
    parts, start = [], 0
    for wd in widths:
        parts.append(w_in[:, start:start + wd])
        start += wd
    (sb_q, sb_k, sb_v, nsa_q, k_cmp, v_cmp, k_sel, v_sel, k_win, v_win, gate, g_sb, g_nsa) = parts
    k_sw = jnp.concatenate([_dup_groups(k_sel), _dup_groups(k_win)], axis=1)
    v_sw_t = jnp.concatenate([_dup_groups(v_sel), _dup_groups(v_win)], axis=1).T
    gate = jnp.pad(gate, ((0, 0), (0, GATE_PAD - gate.shape[1])))
    g_both = jnp.concatenate([g_sb, g_nsa], axis=1)
    ws = [sb_q, sb_k, sb_v, nsa_q, k_cmp, v_cmp, k_sw, v_sw_t, gate, g_both]
    dtypes = [BF16, BF16, BF16, BF16, F32, F32, BF16, BF16, F32, F32]
    transposed = [False] * 7 + [True, False, False]
    scale = HEAD_DIM ** -0.5
    scales = [scale, 1.0, 1.0, scale] + [1.0] * 6
    return [w.astype(BF16) for w in ws], dtypes, scales, transposed


def _compress_weights(w1, w2, pe):
    eye = jnp.eye(NSA_KV_GROUPS, dtype=w1.dtype)
    w1b = jnp.einsum('ldh,pg->lpdgh', w1, eye).reshape(CMP_BLOCK * NSA_KV_GROUPS * HEAD_DIM,
                                                       NSA_KV_GROUPS * CMP_HIDDEN)
    w2b = jnp.einsum('hd,pg,r->phgrd', w2, eye, jnp.ones((2,), w2.dtype)).reshape(
        NSA_KV_GROUPS * CMP_HIDDEN, NSA_KV_GROUPS * LANES)
    peb = jnp.broadcast_to(pe[:, None, :], (CMP_BLOCK, NSA_KV_GROUPS, HEAD_DIM)).reshape(1, -1)
    return peb.astype(F32), w1b.astype(BF16), w2b.astype(BF16)


def _moe_plan(idx_t, wt_t, tm):
    k, t = idx_t.shape
    n = k * t
    i32 = jnp.int32
    experts = jnp.arange(N_EXPERTS, dtype=i32)
    e_flat = idx_t.reshape(n)
    counts = jnp.sum((e_flat[None, :] == experts[:, None]).astype(i32), axis=1)
    padded = ((counts + tm - 1) // tm) * tm
    pad_end = jnp.cumsum(padded)
    gap = padded - counts
    n_pad = N_EXPERTS * tm
    p_rows = n + n_pad
    spare_e = jnp.repeat(experts, tm, total_repeat_length=n_pad)
    spare_used = jnp.tile(jnp.arange(tm, dtype=i32), N_EXPERTS) < jnp.repeat(gap, tm, total_repeat_length=n_pad)
    keys = jnp.concatenate([e_flat, jnp.where(spare_used, spare_e, N_EXPERTS)])
    toks = jnp.concatenate([jnp.arange(n, dtype=i32) % t, jnp.zeros((n_pad,), i32)])
    wts = jnp.concatenate([wt_t.reshape(n), jnp.zeros((n_pad,), F32)])
    rows = jnp.arange(p_rows, dtype=i32)
    _, src_tok, cw, entry = lax.sort((keys, toks, wts, rows), num_keys=1, is_stable=True)
    _, row_of_entry = lax.sort((entry, rows), num_keys=1)
    pos = row_of_entry[:n]
    tile_start = jnp.arange(p_rows // tm, dtype=i32) * tm
    tile_expert = jnp.minimum(jnp.sum((tile_start[:, None] >= pad_end[None, :]).astype(i32), axis=1), N_EXPERTS - 1)
    n_valid = (pad_end[-1] // tm).astype(i32).reshape(1)
    return src_tok, cw.reshape(p_rows, 1), pos, tile_expert, n_valid


def _layer(h, p_i, w_in, w_cmp1, w_cmp2, pe_cmp, w_br_sb, w_br_nsa, w_o, ln1_g, ln1_b, ln2_g, ln2_b,
           w_router, b_router, w_gu_all, b_gu, w_down_all, b_down, w_ple_gate, w_ple_proj,
           *, batch, alpha, layer):
    t, d = h.shape
    s = t // batch
    sb_q, sb_k, sb_v, nsa_q, k_cmp, v_cmp, k_sw, v_sw_t, gate, g_both = _in_proj(h, *_in_proj_weights(w_in))
    b3 = lambda a: a.reshape(batch, s, a.shape[1])

    y_sb = _sb_attention(b3(sb_q), b3(sb_k), b3(sb_v)).reshape(t, -1)

    nc = s // CMP_BLOCK
    cmp_rows = lambda a: a.reshape(batch, nc // 2, 2 * CMP_BLOCK * a.shape[1])
    kc = _compress(cmp_rows(k_cmp), *_compress_weights(w_cmp1[0], w_cmp2[0], pe_cmp[0]))
    vc = _compress(cmp_rows(v_cmp), *_compress_weights(w_cmp1[1], w_cmp2[1], pe_cmp[1]))
    y_cmp, sel, flags = _nsa_cmp(b3(nsa_q), kc, vc, b3(gate))
    y_nsa = _nsa_main(b3(nsa_q), b3(k_sw), v_sw_t, sel, flags, b3(gate), y_cmp).reshape(t, -1)

    row = lambda v: v.reshape(1, -1).astype(F32)
    h1, h1b = _merge_ln1(y_sb, y_nsa, g_both, h, w_br_sb.astype(BF16), w_br_nsa.astype(BF16),
                         w_o.astype(BF16), row(ln1_g), row(ln1_b), alpha)

    idx_t, wt_t = _router(h1b, w_router.T.astype(BF16), b_router.reshape(-1, 1).astype(F32))
    src_tok, cw, pos, tile_expert, n_valid = _moe_plan(idx_t, wt_t, EXPERT_TM)
    xs = h1b.at[src_tok].get(mode='promise_in_bounds')
    y = _experts(tile_expert, n_valid, xs, w_gu_all, b_gu.reshape(N_EXPERTS, 1, -1),
                 w_down_all, b_down.reshape(N_EXPERTS, 1, -1), cw, layer)
    yg = y.at[pos].get(mode='promise_in_bounds').reshape(TOP_K, t, d)
    return _out_ln2(h1, h1b, yg, p_i, w_ple_gate.astype(BF16), w_ple_proj.astype(BF16),
                    row(ln2_g), row(ln2_b), alpha)


def kernel(x, p, w_in, w_cmp1, w_cmp2, pe_cmp, w_br_sb, w_br_nsa, w_o, ln1_g, ln1_b, ln2_g, ln2_b,
           w_router, b_router, w_gu, b_gu, w_down, b_down, w_ple_gate, w_ple_proj):
    batch, s, d = x.shape
    depth = w_in.shape[0]
    alpha = (2 * depth) ** 0.25
    h = x.reshape(batch * s, d)
    for i in range(depth):
        h = _layer(h, p[i].reshape(batch * s, -1), w_in[i], w_cmp1[i], w_cmp2[i], pe_cmp[i],
                   w_br_sb[i], w_br_nsa[i], w_o[i], ln1_g[i], ln1_b[i], ln2_g[i], ln2_b[i],
                   w_router[i], b_router[i], w_gu, b_gu[i], w_down, b_down[i],
                   w_ple_gate[i], w_ple_proj[i], batch=batch, alpha=alpha, layer=i)
    return h.reshape(batch, s, d)
```

```python
import functools

import jax
import jax.numpy as jnp
from jax import lax
from jax.experimental import pallas as pl
from jax.experimental.pallas import tpu as pltpu

F32 = jnp.float32
BF16 = jnp.bfloat16

HEAD_DIM = 64
LANES = 128
SB_HEADS = 8
NSA_HEADS = 8
NSA_KV_GROUPS = 2
NSA_GROUP = NSA_HEADS // NSA_KV_GROUPS
CMP_BLOCK = 32
CMP_HIDDEN = 128
SEL_BLOCK = 64
N_SEL = 16
WINDOW = 512
N_EXPERTS = 32
TOP_K = 4
D_FF = 1024
SWIGLU_ALPHA = 1.702
SWIGLU_LIMIT = 7.0
LN_EPS = 1e-5
NEG_INF = -1e30
FORCED_SCORE = 1e6
GATE_PAD = 128
MASK_BIG = 2.0 ** 100
FEAT_HI, FEAT_LO = 64, 65
SB_DEAD_LOG = -104.0
SB_FIRST_TILES = 3

VMEM_LIMIT = 56 * 1024 * 1024

ATTN_TQ = 128
SEL_TK = 512
PROJ_TM = 256
MERGE_TM = 256
ROUTER_TM = 512
EXPERT_TM = 512
OUT_TM = 256


def _cparams(sem):
    return pltpu.CompilerParams(dimension_semantics=sem, vmem_limit_bytes=VMEM_LIMIT)


def _sigmoid(x):
    return 1.0 / (1.0 + jnp.exp(-x))


def _div_pow2(x, n):
    assert n & (n - 1) == 0
    return lax.shift_right_logical(x, n.bit_length() - 1)


def _mod_pow2(x, n):
    assert n & (n - 1) == 0
    return x & (n - 1)


def _split_bf16(x):
    hi = x.astype(BF16)
    lo = (x - hi.astype(F32)).astype(BF16)
    return hi, lo


def _dot(a, b):
    return jnp.dot(a, b, preferred_element_type=F32)


def _dot_nt(a, b):
    return lax.dot_general(a, b, (((1,), (1,)), ((), ())), preferred_element_type=F32)


def _dot_tn(a, b):
    return lax.dot_general(a, b, (((0,), (0,)), ((), ())), preferred_element_type=F32)


def _layer_norm(u, g, b):
    mu = jnp.mean(u, axis=-1, keepdims=True)
    d = u - mu
    var = jnp.mean(d * d, axis=-1, keepdims=True)
    return d * lax.rsqrt(var + LN_EPS) * g + b


def _in_proj_kernel(x_ref, *refs, scales, transposed):
    n = len(scales)
    x = x_ref[...].astype(BF16)
    for w_ref, o_ref, s, tr in zip(refs[:n], refs[n:], scales, transposed):
        acc = _dot_nt(w_ref[...], x) if tr else _dot(x, w_ref[...])
        if s != 1.0:
            acc = acc * s
        o_ref[...] = acc.astype(o_ref.dtype)


def _in_proj(h, weights, dtypes, scales, transposed):
    t, d = h.shape
    tm = PROJ_TM
    in_specs = [pl.BlockSpec((tm, d), lambda i: (i, 0))]
    in_specs += [pl.BlockSpec(w.shape, lambda i: (0, 0)) for w in weights]
    out_specs, out_shape = [], []
    for w, dt, tr in zip(weights, dtypes, transposed):
        if tr:
            out_specs.append(pl.BlockSpec((w.shape[0], tm), lambda i: (0, i)))
            out_shape.append(jax.ShapeDtypeStruct((w.shape[0], t), dt))
        else:
            out_specs.append(pl.BlockSpec((tm, w.shape[1]), lambda i: (i, 0)))
            out_shape.append(jax.ShapeDtypeStruct((t, w.shape[1]), dt))
    return pl.pallas_call(
        functools.partial(_in_proj_kernel, scales=tuple(scales), transposed=tuple(transposed)),
        grid=(t // tm,), in_specs=in_specs, out_specs=out_specs, out_shape=out_shape,
        compiler_params=_cparams(("parallel",)), name="in_proj",
    )(h, *weights)


def _stack_heads(q2, tq):
    lane = lax.broadcasted_iota(jnp.int32, (tq, LANES), 1)
    zero = jnp.zeros_like(q2)
    return jnp.concatenate([jnp.where(lane < HEAD_DIM, q2, zero),
                            jnp.where(lane >= HEAD_DIM, q2, zero)], axis=0)


def _unstack_heads(o, tq):
    lane = lax.broadcasted_iota(jnp.int32, (tq, LANES), 1)
    return jnp.where(lane < HEAD_DIM, o[:tq], o[tq:])


def _sb_kernel(q_ref, k_ref, v_ref, o_ref, *, tq, n_pairs):
    i = pl.program_id(1)
    r_i = lax.broadcasted_iota(jnp.int32, (tq, tq), 0)
    c_i = lax.broadcasted_iota(jnp.int32, (tq, tq), 1)
    tri = jnp.where(r_i > c_i, 1.0, 0.0).astype(BF16)
    rhs = jnp.concatenate([tri, jnp.ones((tq, tq), BF16)], axis=1)
    rhs = jnp.concatenate([rhs, rhs], axis=0)
    qs = [_stack_heads(q_ref[:, p * LANES:(p + 1) * LANES], tq) for p in range(n_pairs)]

    def log_stay(z):
        return -(jnp.maximum(z, 0.0) + jnp.log(1.0 + jnp.exp(-jnp.abs(z))))

    def tile_sums(ls):
        hi, lo = _split_bf16(ls)
        return _dot(jnp.concatenate([hi, lo], axis=1), rhs)

    wide = SB_FIRST_TILES * tq
    first = jnp.maximum(i - (SB_FIRST_TILES - 1), 0)
    start0 = pl.multiple_of(first * tq, tq)
    col_minus_row = (lax.broadcasted_iota(jnp.int32, (2 * tq, wide), 1)
                     - (lax.broadcasted_iota(jnp.int32, (2 * tq, wide), 0) & (tq - 1)))
    causal = col_minus_row < (i - first) * tq

    def first_step(p):
        kw = k_ref[pl.ds(start0, wide), p * LANES:(p + 1) * LANES]
        vw = v_ref[pl.ds(start0, wide), p * LANES:(p + 1) * LANES]
        z = _dot_nt(qs[p], kw)
        ls = jnp.where(causal, log_stay(z), 0.0)
        sums = [tile_sums(ls[:, j * tq:(j + 1) * tq]) for j in range(SB_FIRST_TILES)]
        later, run = [None] * SB_FIRST_TILES, None
        for j in reversed(range(SB_FIRST_TILES)):
            later[j] = sums[j][:, :tq] if run is None else sums[j][:, :tq] + run
            run = sums[j][:, tq:] if run is None else run + sums[j][:, tq:]
        w = jnp.where(causal, jnp.exp(z + ls + jnp.concatenate(later, axis=1)), 0.0)
        return run, _dot(w.astype(BF16), vw)

    def tile(p, j, carry, acc):
        start = pl.multiple_of(j * tq, tq)
        kj = k_ref[pl.ds(start, tq), p * LANES:(p + 1) * LANES]
        vj = v_ref[pl.ds(start, tq), p * LANES:(p + 1) * LANES]
        z = _dot_nt(qs[p], kj)
        ls = log_stay(z)
        sums = tile_sums(ls)
        w = jnp.exp(z + ls + sums[:, :tq] + carry)
        return carry + sums[:, tq:], acc + _dot(w.astype(BF16), vj)

    def live(carries):
        worst = functools.reduce(jnp.maximum, carries)
        return jnp.max(worst) > SB_DEAD_LOG

    state = [first_step(p) for p in range(n_pairs)]
    carries = tuple(s[0] for s in state)
    accs = tuple(s[1] for s in state)

    def cond(c):
        return jnp.logical_and(c[0] <= first, c[1])

    def body(c):
        step, _, carries, accs = c
        state = [tile(p, first - step, carries[p], accs[p]) for p in range(n_pairs)]
        carries = tuple(s[0] for s in state)
        return step + 1, live(carries), carries, tuple(s[1] for s in state)

    _, _, _, accs = lax.while_loop(cond, body, (jnp.int32(1), live(carries), carries, accs))
    for p in range(n_pairs):
        o_ref[:, p * LANES:(p + 1) * LANES] = _unstack_heads(accs[p], tq).astype(o_ref.dtype)


def _sb_attention(q, k, v):
    b, s, w = q.shape
    tq = ATTN_TQ
    return pl.pallas_call(
        functools.partial(_sb_kernel, tq=tq, n_pairs=w // LANES),
        grid=(b, s // tq),
        in_specs=[pl.BlockSpec((None, tq, w), lambda bi, i: (bi, i, 0)),
                  pl.BlockSpec((None, s, w), lambda bi, i: (bi, 0, 0)),
                  pl.BlockSpec((None, s, w), lambda bi, i: (bi, 0, 0))],
        out_specs=pl.BlockSpec((None, tq, w), lambda bi, i: (bi, i, 0)),
        out_shape=jax.ShapeDtypeStruct((b, s, w), BF16),
        compiler_params=_cparams(("parallel", "arbitrary")), name="sb_attn",
    )(q, k, v)


def _compress_kernel(x_ref, pe_ref, w1_ref, w2_ref, o_ref):
    x = (x_ref[...] + pe_ref[...]).astype(BF16)
    hid = _dot(x, w1_ref[...])
    hid = hid * _sigmoid(hid)
    o_ref[...] = _dot(hid.astype(BF16), w2_ref[...]).astype(o_ref.dtype)


def _compress(x, pe, w1, w2):
    b, half, width = x.shape
    blk = width // 2
    out = pl.pallas_call(
        _compress_kernel,
        grid=(b, 2),
        in_specs=[pl.BlockSpec((None, half, blk), lambda bi, par: (bi, 0, par)),
                  pl.BlockSpec(pe.shape, lambda bi, par: (0, 0)),
                  pl.BlockSpec(w1.shape, lambda bi, par: (0, 0)),
                  pl.BlockSpec(w2.shape, lambda bi, par: (0, 0))],
        out_specs=pl.BlockSpec((None, None, half, w2.shape[1]), lambda bi, par: (bi, par, 0, 0)),
        out_shape=jax.ShapeDtypeStruct((b, 2, half, w2.shape[1]), BF16),
        compiler_params=_cparams(("parallel", "parallel")), name="compress",
    )(x, pe, w1, w2)
    return out.reshape(b, 2 * half, w2.shape[1])


def _stack_group(q, tq):
    return jnp.concatenate([_stack_heads(q[:, :LANES], tq), _stack_heads(q[:, LANES:], tq)], axis=0)


def _unstack_group(o, tq):
    return jnp.concatenate([_unstack_heads(o[:2 * tq], tq), _unstack_heads(o[2 * tq:], tq)], axis=1)


def _alibi_slope(head_idx):
    return lax.bitcast_convert_type(lax.shift_left(126 - head_idx, 23), F32)


def _expand_gate(gate_sig, g, branch):
    c_i = lax.broadcasted_iota(jnp.int32, (GATE_PAD, 2 * LANES), 0)
    l_i = lax.broadcasted_iota(jnp.int32, (GATE_PAD, 2 * LANES), 1)
    col = g * (3 * NSA_GROUP) + _div_pow2(l_i, HEAD_DIM) * 3 + branch
    onehot = jnp.where(c_i == col, 1.0, 0.0).astype(BF16)
    hi, lo = _split_bf16(gate_sig)
    return _dot(hi, onehot) + _dot(lo, onehot)


def _nsa_cmp_kernel(q_ref, kc_ref, vc_ref, gate_ref, y_ref, sel_ref, flag_ref, *, tq, n_sel):
    g = pl.program_id(1)
    i = pl.program_id(2)
    nc = kc_ref.shape[0]
    half = nc // 2
    qst = _stack_group(q_ref[...], tq)
    s = _dot_nt(kc_ref[...], qst)
    r_i = lax.broadcasted_iota(jnp.int32, (nc, 4 * tq), 0)
    l_i = lax.broadcasted_iota(jnp.int32, (nc, 4 * tq), 1)
    cblk = 2 * _mod_pow2(r_i, half) + _div_pow2(r_i, half)
    c_end = cblk * CMP_BLOCK + (CMP_BLOCK - 1)
    t = i * tq + _mod_pow2(l_i, tq)
    dist = t - c_end
    slope = _alibi_slope(g * NSA_GROUP + _div_pow2(l_i, tq))
    s = s - slope * dist.astype(F32)
    mask = dist >= 0
    m = jnp.max(jnp.where(mask, s, NEG_INF), axis=0, keepdims=True)
    p = jnp.where(mask, jnp.exp(s - m), 0.0)
    l = jnp.sum(p, axis=0, keepdims=True)
    p = p * jnp.where(l > 0.0, 1.0 / l, 0.0)

    o = _dot_tn(p.astype(BF16), vc_ref[...])
    gate = _expand_gate(_sigmoid(gate_ref[...]), g, 0)
    y_ref[...] = (_unstack_group(o, tq) * gate).astype(y_ref.dtype)

    imp_c = p[:, 0:tq] + p[:, tq:2 * tq] + p[:, 2 * tq:3 * tq] + p[:, 3 * tq:4 * tq]
    imp = imp_c[:half] + imp_c[half:]
    n_blk = half
    blk = lax.broadcasted_iota(jnp.int32, (n_blk, tq), 0)
    tt = i * tq + lax.broadcasted_iota(jnp.int32, (n_blk, tq), 1)
    cur = _div_pow2(tt, SEL_BLOCK)
    score = jnp.where(blk * SEL_BLOCK <= tt, imp, NEG_INF)
    for forced_blk in (0, cur, cur - 1):
        score = jnp.where(blk == forced_blk, FORCED_SCORE, score)

    taken = -jnp.inf
    sel = jnp.zeros((n_blk, tq), F32)
    for _ in range(n_sel):
        mx = jnp.max(score, axis=0, keepdims=True)
        first = jnp.min(jnp.where(score == mx, blk, n_blk), axis=0, keepdims=True)
        hit = blk == first
        sel = jnp.where(hit, 1.0, sel)
        score = jnp.where(hit, taken, score)
    sel = jnp.concatenate([sel, jnp.zeros((LANES - n_blk, tq), F32)], axis=0)
    sel_ref[...] = sel.T
    picks = _dot_nt(jnp.ones((8, tq), BF16), sel.astype(BF16))
    flag_ref[...] = picks[0:1].astype(jnp.int32)


def _nsa_cmp(q, kc, vc, gate):
    b, s, _ = q.shape
    tq = ATTN_TQ
    nc = kc.shape[1]
    n_blk = nc // 2
    assert n_blk <= LANES
    n_sel = min(N_SEL, n_blk)
    return pl.pallas_call(
        functools.partial(_nsa_cmp_kernel, tq=tq, n_sel=n_sel),
        grid=(b, NSA_KV_GROUPS, s // tq),
        in_specs=[pl.BlockSpec((None, tq, 2 * LANES), lambda bi, g, i: (bi, i, g)),
                  pl.BlockSpec((None, nc, LANES), lambda bi, g, i: (bi, 0, g)),
                  pl.BlockSpec((None, nc, LANES), lambda bi, g, i: (bi, 0, g)),
                  pl.BlockSpec((None, tq, GATE_PAD), lambda bi, g, i: (bi, i, 0))],
        out_specs=[pl.BlockSpec((None, tq, 2 * LANES), lambda bi, g, i: (bi, i, g)),
                   pl.BlockSpec((None, None, tq, LANES), lambda bi, g, i: (bi, g, i, 0)),
                   pl.BlockSpec((None, None, None, 1, LANES), lambda bi, g, i: (bi, g, i, 0, 0))],
        out_shape=[jax.ShapeDtypeStruct((b, s, 4 * LANES), F32),
                   jax.ShapeDtypeStruct((b, NSA_KV_GROUPS, s, LANES), F32),
                   jax.ShapeDtypeStruct((b, NSA_KV_GROUPS, s // tq, 1, LANES), jnp.int32)],
        compiler_params=_cparams(("parallel", "parallel", "parallel")), name="nsa_cmp",
    )(q, kc, vc, gate)


def _key_features(s, tk, wk):
    def table(n, period, with_blocks):
        key = lax.broadcasted_iota(jnp.int32, (n, LANES), 0)
        lane = lax.broadcasted_iota(jnp.int32, (n, LANES), 1)
        r = key % period
        pos = jnp.where(lane == FEAT_HI, r // SEL_BLOCK, jnp.where(lane == FEAT_LO, r % SEL_BLOCK, 0))
        if with_blocks:
            pos = jnp.where(lane < SEL_BLOCK, (key // SEL_BLOCK == lane).astype(jnp.int32), pos)
        return pos.astype(BF16)
    return table(s, tk, True), table(wk, wk, False)


def _nsa_main_kernel(flag_ref, q_ref, ks_ref, kw_ref, vs_ref, vw_ref, kfeat_ref, rfeat_ref, sel_ref,
                     gate_ref, ycmp_ref, o_ref, m_ref, l_ref, acc_ref, *, tq, tk):
    g = pl.program_id(1)
    i = pl.program_id(2)
    cols = NSA_GROUP * tq
    qs = pl.multiple_of(i * tq, tq)
    qst = _stack_group(q_ref[...], tq)
    col = lax.broadcasted_iota(jnp.int32, (1, cols), 1)
    t_loc = _mod_pow2(col, tq)
    slope_row = _alibi_slope(g * NSA_GROUP + _div_pow2(col, tq))

    lane = lax.broadcasted_iota(jnp.int32, (tq, LANES), 1)
    first_blk = i * (tq // SEL_BLOCK)
    open_blk = jnp.where(lane < first_blk, sel_ref[...], 0.0) > 0.5
    blk_add = jnp.where(open_blk, 0.0, -MASK_BIG)
    f_sel, f_win = [], []
    for h in range(NSA_GROUP):
        slope = _alibi_slope(jnp.full((tq, LANES), g * NSA_GROUP + h, jnp.int32))
        pos = jnp.where(lane == FEAT_HI, slope * SEL_BLOCK, jnp.where(lane == FEAT_LO, slope, 0.0))
        f_win.append(pos)
        f_sel.append(jnp.where(lane < SEL_BLOCK, blk_add, pos))
    q_sel = jnp.concatenate([qst, jnp.concatenate(f_sel, axis=0).astype(BF16)], axis=1)
    q_win = jnp.concatenate([qst, jnp.concatenate(f_win, axis=0).astype(BF16)], axis=1)

    def online(k, q_aug, mask_add, bias, v_t):
        s = _dot_nt(k, q_aug)
        if mask_add is not None:
            s = s + mask_add
        m = m_ref[...]
        m_new = jnp.maximum(m, jnp.max(s, axis=0, keepdims=True) + bias)
        alpha = jnp.exp(m - m_new)
        p = jnp.exp(s + (bias - m_new))
        m_ref[...] = m_new
        l_ref[...] = alpha * l_ref[...] + jnp.sum(p, axis=0, keepdims=True)
        acc_ref[...] = alpha * acc_ref[...] + _dot(v_t, p.astype(BF16))

    def untranspose(o_t):
        return jnp.concatenate([o_t[:, h * tq:(h + 1) * tq].T for h in range(NSA_GROUP)], axis=0)

    r_i = lax.broadcasted_iota(jnp.int32, (tq, tq), 0)
    c_i = lax.broadcasted_iota(jnp.int32, (tq, tq), 1)
    causal_add = jnp.where(r_i <= c_i, 0.0, -MASK_BIG)
    causal_add = jnp.concatenate([causal_add] * NSA_GROUP, axis=1)
    m_ref[...] = jnp.full((1, cols), NEG_INF, F32)
    l_ref[...] = jnp.zeros((1, cols), F32)
    acc_ref[...] = jnp.zeros((LANES, cols), F32)
    k = jnp.concatenate([ks_ref[pl.ds(qs, tq), :], rfeat_ref[0:tq, :]], axis=1)
    online(k, q_sel, causal_add, -slope_row * t_loc.astype(F32), vs_ref[:, pl.ds(qs, tq)])

    def bulk(kt, _):
        start = pl.multiple_of(kt * tk, tk)
        picked = flag_ref[0, kt * (tk // SEL_BLOCK)]
        for j in range(1, tk // SEL_BLOCK):
            picked = picked + flag_ref[0, kt * (tk // SEL_BLOCK) + j]

        @pl.when(picked > 0)
        def _():
            k = jnp.concatenate([ks_ref[pl.ds(start, tk), :], kfeat_ref[pl.ds(start, tk), :]], axis=1)
            bias = -slope_row * (qs - start + t_loc).astype(F32)
            online(k, q_sel, None, bias, vs_ref[:, pl.ds(start, tk)])

        return 0

    lax.fori_loop(0, (qs + tk - 1) // tk, bulk, 0)
    o_sel = untranspose(acc_ref[...] * (1.0 / l_ref[...]))

    wk = WINDOW + tq
    start = pl.multiple_of(jnp.maximum(qs - WINDOW, 0), tq)
    w_r = lax.broadcasted_iota(jnp.int32, (wk, tq), 0)
    w_c = lax.broadcasted_iota(jnp.int32, (wk, tq), 1)
    dist = (qs - start) + w_c - w_r
    band_add = jnp.where(jnp.where(dist >= 0, dist, WINDOW) < WINDOW, 0.0, -MASK_BIG)
    band_add = jnp.concatenate([band_add] * NSA_GROUP, axis=1)
    k = jnp.concatenate([kw_ref[pl.ds(start, wk), :], rfeat_ref[...]], axis=1)
    s = _dot_nt(k, q_win) + band_add
    p = jnp.exp(s - jnp.max(s, axis=0, keepdims=True))
    l_w = jnp.sum(p, axis=0, keepdims=True)
    o_win = untranspose(_dot(vw_ref[:, pl.ds(start, wk)], p.astype(BF16)) * (1.0 / l_w))

    gate = _sigmoid(gate_ref[...])
    y = (ycmp_ref[...] + _expand_gate(gate, g, 1) * _unstack_group(o_sel, tq)
         + _expand_gate(gate, g, 2) * _unstack_group(o_win, tq))
    o_ref[...] = y.astype(o_ref.dtype)


def _nsa_main(q, k_sw, v_sw_t, sel, flags, gate, ycmp):
    b, s, _ = q.shape
    tq = ATTN_TQ
    tk = min(SEL_TK, s)
    wk = WINDOW + tq
    cols = NSA_GROUP * tq
    assert s >= wk and s % tk == 0 and s // SEL_BLOCK <= SEL_BLOCK
    kfeat, rfeat = _key_features(s, tk, wk)
    k_spec = lambda off: pl.BlockSpec((None, s, LANES), lambda bi, g, i: (bi, 0, off + g))
    v_spec = lambda off: pl.BlockSpec((LANES, s), lambda bi, g, i: (off + g, bi))
    return pl.pallas_call(
        functools.partial(_nsa_main_kernel, tq=tq, tk=tk),
        grid=(b, NSA_KV_GROUPS, s // tq),
        in_specs=[pl.BlockSpec((None, None, None, 1, LANES), lambda bi, g, i: (bi, g, i, 0, 0),
                               memory_space=pltpu.SMEM),
                  pl.BlockSpec((None, tq, 2 * LANES), lambda bi, g, i: (bi, i, g)),
                  k_spec(0), k_spec(2), v_spec(0), v_spec(2),
                  pl.BlockSpec((s, LANES), lambda bi, g, i: (0, 0)),
                  pl.BlockSpec((wk, LANES), lambda bi, g, i: (0, 0)),
                  pl.BlockSpec((None, None, tq, LANES), lambda bi, g, i: (bi, g, i, 0)),
                  pl.BlockSpec((None, tq, GATE_PAD), lambda bi, g, i: (bi, i, 0)),
                  pl.BlockSpec((None, tq, 2 * LANES), lambda bi, g, i: (bi, i, g))],
        out_specs=pl.BlockSpec((None, tq, 2 * LANES), lambda bi, g, i: (bi, i, g)),
        out_shape=jax.ShapeDtypeStruct((b, s, 4 * LANES), BF16),
        scratch_shapes=[pltpu.VMEM((1, cols), F32), pltpu.VMEM((1, cols), F32), pltpu.VMEM((LANES, cols), F32)],
        compiler_params=_cparams(("parallel", "parallel", "arbitrary")), name="nsa_main",
    )(flags, q, k_sw, k_sw, v_sw_t, v_sw_t, kfeat, rfeat, sel, gate, ycmp)


def _merge_kernel(ysb_ref, ynsa_ref, g_ref, h_ref, wsb_ref, wnsa_ref, wo_ref, lng_ref, lnb_ref,
                  h1_ref, h1b_ref, *, alpha):
    d = h_ref.shape[1]
    g = g_ref[...]
    merged = (_sigmoid(g[:, :d]) * _dot(ysb_ref[...], wsb_ref[...])
              + _sigmoid(g[:, d:]) * _dot(ynsa_ref[...], wnsa_ref[...]))
    u = alpha * h_ref[...] + _dot(merged.astype(BF16), wo_ref[...])
    h1 = _layer_norm(u, lng_ref[...], lnb_ref[...])
    h1_ref[...] = h1
    h1b_ref[...] = h1.astype(BF16)


def _merge_ln1(ysb, ynsa, g, h, wsb, wnsa, wo, lng, lnb, alpha):
    t, d = h.shape
    tm = MERGE_TM
    row = lambda w: pl.BlockSpec((tm, w), lambda i: (i, 0))
    full = lambda a: pl.BlockSpec(a.shape, lambda i: (0, 0))
    return pl.pallas_call(
        functools.partial(_merge_kernel, alpha=alpha),
        grid=(t // tm,),
        in_specs=[row(ysb.shape[1]), row(ynsa.shape[1]), row(2 * d), row(d),
                  full(wsb), full(wnsa), full(wo), full(lng), full(lnb)],
        out_specs=[row(d), row(d)],
        out_shape=[jax.ShapeDtypeStruct((t, d), F32), jax.ShapeDtypeStruct((t, d), BF16)],
        compiler_params=_cparams(("parallel",)), name="merge_ln1",
    )(ysb, ynsa, g, h, wsb, wnsa, wo, lng, lnb)


def _router_kernel(x_ref, w_ref, b_ref, idx_ref, wt_ref):
    logits = _dot_nt(w_ref[...], x_ref[...]) + b_ref[...]
    n_e, tm = logits.shape
    e_i = lax.broadcasted_iota(jnp.int32, (n_e, tm), 0)
    vals, idxs = [], []
    for _ in range(TOP_K):
        mx = jnp.max(logits, axis=0, keepdims=True)
        first = jnp.min(jnp.where(logits == mx, e_i, n_e), axis=0, keepdims=True)
        vals.append(mx)
        idxs.append(first)
        logits = jnp.where(e_i == first, -jnp.inf, logits)
    ex = [jnp.exp(v - vals[0]) for v in vals]
    inv = 1.0 / (ex[0] + ex[1] + ex[2] + ex[3])
    idx_ref[...] = jnp.concatenate(idxs, axis=0)
    wt_ref[...] = jnp.concatenate([e * inv for e in ex], axis=0)


def _router(h1b, w_t, b_col):
    t, d = h1b.shape
    tm = ROUTER_TM
    return pl.pallas_call(
        _router_kernel,
        grid=(t // tm,),
        in_specs=[pl.BlockSpec((tm, d), lambda i: (i, 0)),
                  pl.BlockSpec(w_t.shape, lambda i: (0, 0)),
                  pl.BlockSpec(b_col.shape, lambda i: (0, 0))],
        out_specs=[pl.BlockSpec((TOP_K, tm), lambda i: (0, i)),
                   pl.BlockSpec((TOP_K, tm), lambda i: (0, i))],
        out_shape=[jax.ShapeDtypeStruct((TOP_K, t), jnp.int32),
                   jax.ShapeDtypeStruct((TOP_K, t), F32)],
        compiler_params=_cparams(("parallel",)), name="router",
    )(h1b, w_t, b_col)


def _expert_kernel(te_ref, nv_ref, x_ref, wgu_ref, bgu_ref, wd_ref, bd_ref, y_ref, wgu_bf, wd_bf):
    i = pl.program_id(0)

    @pl.when(jnp.logical_or(i == 0, te_ref[i] != te_ref[jnp.maximum(i - 1, 0)]))
    def _():
        wgu_bf[...] = wgu_ref[...].astype(BF16)
        wd_bf[...] = wd_ref[...].astype(BF16)

    @pl.when(i < nv_ref[0])
    def _():
        gu = _dot(x_ref[...], wgu_bf[...]) + bgu_ref[...]
        gate = jnp.minimum(gu[:, :D_FF], SWIGLU_LIMIT)
        up = jnp.clip(gu[:, D_FF:], -SWIGLU_LIMIT, SWIGLU_LIMIT)
        act = (up + 1.0) * gate * _sigmoid(SWIGLU_ALPHA * gate)
        y_ref[...] = (_dot(act.astype(BF16), wd_bf[...]) + bd_ref[...]).astype(y_ref.dtype)

    @pl.when(i >= nv_ref[0])
    def _():
        y_ref[...] = jnp.zeros_like(y_ref)


def _experts(tile_expert, n_valid, xs, wgu, bgu, wd, bd, layer):
    p, d = xs.shape
    tm = EXPERT_TM
    two_ff = wgu.shape[-1]
    grid_spec = pltpu.PrefetchScalarGridSpec(
        num_scalar_prefetch=2, grid=(p // tm,),
        in_specs=[pl.BlockSpec((tm, d), lambda i, te, nv: (i, 0)),
                  pl.BlockSpec((None, None, d, two_ff), lambda i, te, nv: (layer, te[i], 0, 0)),
                  pl.BlockSpec((None, 1, two_ff), lambda i, te, nv: (te[i], 0, 0)),
                  pl.BlockSpec((None, None, two_ff // 2, d), lambda i, te, nv: (layer, te[i], 0, 0)),
                  pl.BlockSpec((None, 1, d), lambda i, te, nv: (te[i], 0, 0))],
        out_specs=pl.BlockSpec((tm, d), lambda i, te, nv: (i, 0)),
        scratch_shapes=[pltpu.VMEM((d, two_ff), BF16), pltpu.VMEM((two_ff // 2, d), BF16)])
    return pl.pallas_call(
        _expert_kernel, grid_spec=grid_spec,
        out_shape=jax.ShapeDtypeStruct((p, d), F32),
        compiler_params=_cparams(("arbitrary",)), name="experts",
    )(tile_expert, n_valid, xs, wgu, bgu, wd, bd)


def _out_kernel(h1_ref, h1b_ref, yg_ref, cw_ref, p_ref, wpg_ref, wpp_ref, lng_ref, lnb_ref, o_ref, *, alpha):
    cw = cw_ref[...]
    moe = cw[:, 0:1] * yg_ref[0]
    for k in range(1, TOP_K):
        moe = moe + cw[:, k:k + 1] * yg_ref[k]
    ple = _sigmoid(_dot(h1b_ref[...], wpg_ref[...])) * _dot(p_ref[...].astype(BF16), wpp_ref[...])
    u = alpha * h1_ref[...] + moe + ple
    o_ref[...] = _layer_norm(u, lng_ref[...], lnb_ref[...])


def _out_ln2(h1, h1b, yg, cw, p, wpg, wpp, lng, lnb, alpha):
    t, d = h1.shape
    tm = OUT_TM
    row = lambda w: pl.BlockSpec((tm, w), lambda i: (i, 0))
    full = lambda a: pl.BlockSpec(a.shape, lambda i: (0, 0))
    return pl.pallas_call(
        functools.partial(_out_kernel, alpha=alpha),
        grid=(t // tm,),
        in_specs=[row(d), row(d), pl.BlockSpec((TOP_K, tm, d), lambda i: (0, i, 0)), row(TOP_K),
                  row(p.shape[1]), full(wpg), full(wpp), full(lng), full(lnb)],
        out_specs=row(d),
        out_shape=jax.ShapeDtypeStruct((t, d), F32),
        compiler_params=_cparams(("parallel",)), name="out_ln2",
    )(h1, h1b, yg, cw, p, wpg, wpp, lng, lnb)


def _dup_groups(w):
    d = w.shape[0]
    w = w.reshape(d, NSA_KV_GROUPS, 1, HEAD_DIM)
    return jnp.broadcast_to(w, (d, NSA_KV_GROUPS, 2, HEAD_DIM)).reshape(d, NSA_KV_GROUPS * LANES)


def _in_proj_weights(w_in):
    sbw = SB_HEADS * HEAD_DIM
    qw = NSA_HEADS * HEAD_DIM
    kvw = NSA_KV_GROUPS * HEAD_DIM
    d = w_in.shape[0]
    widths = (sbw, sbw, sbw, qw) + (kvw,) * 6 + (3 * NSA_HEADS, d, d)
    parts, start = [], 0
    for wd in widths:
        parts.append(w_in[:, start:start + wd])
        start += wd
    (sb_q, sb_k, sb_v, nsa_q, k_cmp, v_cmp, k_sel, v_sel, k_win, v_win, gate, g_sb, g_nsa) = parts
    k_sw = jnp.concatenate([_dup_groups(k_sel), _dup_groups(k_win)], axis=1)
    v_sw_t = jnp.concatenate([_dup_groups(v_sel), _dup_groups(v_win)], axis=1).T
    gate = jnp.pad(gate, ((0, 0), (0, GATE_PAD - gate.shape[1])))
    g_both = jnp.concatenate([g_sb, g_nsa], axis=1)
    ws = [sb_q, sb_k, sb_v, nsa_q, k_cmp, v_cmp, k_sw, v_sw_t, gate, g_both]
    dtypes = [BF16, BF16, BF16, BF16, F32, F32, BF16, BF16, F32, F32]
    transposed = [False] * 7 + [True, False, False]
    scale = HEAD_DIM ** -0.5
    scales = [scale, 1.0, 1.0, scale] + [1.0] * 6
    return [w.astype(BF16) for w in ws], dtypes, scales, transposed


def _compress_weights(w1, w2, pe):
    eye = jnp.eye(NSA_KV_GROUPS, dtype=w1.dtype)
    w1b = jnp.einsum('ldh,pg->lpdgh', w1, eye).reshape(CMP_BLOCK * NSA_KV_GROUPS * HEAD_DIM,
                                                       NSA_KV_GROUPS * CMP_HIDDEN)
    w2b = jnp.einsum('hd,pg,r->phgrd', w2, eye, jnp.ones((2,), w2.dtype)).reshape(
        NSA_KV_GROUPS * CMP_HIDDEN, NSA_KV_GROUPS * LANES)
    peb = jnp.broadcast_to(pe[:, None, :], (CMP_BLOCK, NSA_KV_GROUPS, HEAD_DIM)).reshape(1, -1)
    return peb.astype(F32), w1b.astype(BF16), w2b.astype(BF16)


def _moe_plan(idx_t, tm):
    k, t = idx_t.shape
    n = k * t
    i32 = jnp.int32
    experts = jnp.arange(N_EXPERTS, dtype=i32)
    e_flat = idx_t.reshape(n)
    counts = jnp.sum((e_flat[None, :] == experts[:, None]).astype(i32), axis=1)
    padded = ((counts + tm - 1) // tm) * tm
    pad_end = jnp.cumsum(padded)
    gap = padded - counts
    n_pad = N_EXPERTS * tm
    p_rows = n + n_pad
    id_bits = (p_rows - 1).bit_length()
    assert (N_EXPERTS + 1) << id_bits < 2 ** 31
    spare_e = jnp.repeat(experts, tm, total_repeat_length=n_pad)
    spare_used = jnp.tile(jnp.arange(tm, dtype=i32), N_EXPERTS) < jnp.repeat(gap, tm, total_repeat_length=n_pad)
    keys = jnp.concatenate([e_flat, jnp.where(spare_used, spare_e, N_EXPERTS)])
    rows = jnp.arange(p_rows, dtype=i32)
    entry = lax.sort(keys * (1 << id_bits) + rows) & ((1 << id_bits) - 1)
    src_tok = jnp.where(entry < n, entry % t, 0)
    _, row_of_entry = lax.sort((entry, rows), num_keys=1)
    pos = row_of_entry[:n]
    tile_start = jnp.arange(p_rows // tm, dtype=i32) * tm
    tile_expert = jnp.minimum(jnp.sum((tile_start[:, None] >= pad_end[None, :]).astype(i32), axis=1), N_EXPERTS - 1)
    n_valid = (pad_end[-1] // tm).astype(i32).reshape(1)
    return src_tok, pos, tile_expert, n_valid


def _layer(h, p_i, w_in, w_cmp1, w_cmp2, pe_cmp, w_br_sb, w_br_nsa, w_o, ln1_g, ln1_b, ln2_g, ln2_b,
           w_router, b_router, w_gu_all, b_gu, w_down_all, b_down, w_ple_gate, w_ple_proj,
           *, batch, alpha, layer):
    t, d = h.shape
    s = t // batch
    sb_q, sb_k, sb_v, nsa_q, k_cmp, v_cmp, k_sw, v_sw_t, gate, g_both = _in_proj(h, *_in_proj_weights(w_in))
    b3 = lambda a: a.reshape(batch, s, a.shape[1])

    y_sb = _sb_attention(b3(sb_q), b3(sb_k), b3(sb_v)).reshape(t, -1)

    nc = s // CMP_BLOCK
    cmp_rows = lambda a: a.reshape(batch, nc // 2, 2 * CMP_BLOCK * a.shape[1])
    kc = _compress(cmp_rows(k_cmp), *_compress_weights(w_cmp1[0], w_cmp2[0], pe_cmp[0]))
    vc = _compress(cmp_rows(v_cmp), *_compress_weights(w_cmp1[1], w_cmp2[1], pe_cmp[1]))
    y_cmp, sel, flags = _nsa_cmp(b3(nsa_q), kc, vc, b3(gate))
    y_nsa = _nsa_main(b3(nsa_q), b3(k_sw), v_sw_t, sel, flags, b3(gate), y_cmp).reshape(t, -1)

    row = lambda v: v.reshape(1, -1).astype(F32)
    h1, h1b = _merge_ln1(y_sb, y_nsa, g_both, h, w_br_sb.astype(BF16), w_br_nsa.astype(BF16),
                         w_o.astype(BF16), row(ln1_g), row(ln1_b), alpha)

    idx_t, wt_t = _router(h1b, w_router.T.astype(BF16), b_router.reshape(-1, 1).astype(F32))
    src_tok, pos, tile_expert, n_valid = _moe_plan(idx_t, EXPERT_TM)
    xs = jnp.take(h1b, src_tok, axis=0, mode='clip')
    y = _experts(tile_expert, n_valid, xs, w_gu_all, b_gu.reshape(N_EXPERTS, 1, -1),
                 w_down_all, b_down.reshape(N_EXPERTS, 1, -1), layer)
    yg = y.at[pos].get(mode='promise_in_bounds').reshape(TOP_K, t, d)
    return _out_ln2(h1, h1b, yg, wt_t.T, p_i, w_ple_gate.astype(BF16), w_ple_proj.astype(BF16),
                    row(ln2_g), row(ln2_b), alpha)


def kernel(x, p, w_in, w_cmp1, w_cmp2, pe_cmp, w_br_sb, w_br_nsa, w_o, ln1_g, ln1_b, ln2_g, ln2_b,
           w_router, b_router, w_gu, b_gu, w_down, b_down, w_ple_gate, w_ple_proj):
    batch, s, d = x.shape
    depth = w_in.shape[0]
    alpha = (2 * depth) ** 0.25
    h = x.reshape(batch * s, d)
    for i in range(depth):
        h = _layer(h, p[i].reshape(batch * s, -1), w_in[i], w_cmp1[i], w_cmp2[i], pe_cmp[i],
                   w_br_sb[i], w_br_nsa[i], w_o[i], ln1_g[i], ln1_b[i], ln2_g[i], ln2_b[i],
                   w_router[i], b_router[i], w_gu, b_gu[i], w_down, b_down[i],
                   w_ple_gate[i], w_ple_proj[i], batch=batch, alpha=alpha, layer=i)
    return h.reshape(batch, s, d)
```

```python
import functools

import jax
import jax.numpy as jnp
from jax import lax
from jax.experimental import pallas as pl
from jax.experimental.pallas import tpu as pltpu

F32 = jnp.float32
BF16 = jnp.bfloat16

HEAD_DIM = 64
LANES = 128
SB_HEADS = 8
NSA_HEADS = 8
NSA_KV_GROUPS = 2
NSA_GROUP = NSA_HEADS // NSA_KV_GROUPS
CMP_BLOCK = 32
CMP_HIDDEN = 128
SEL_BLOCK = 64
N_SEL = 16
WINDOW = 512
N_EXPERTS = 32
TOP_K = 4
D_FF = 1024
SWIGLU_ALPHA = 1.702
SWIGLU_LIMIT = 7.0
LN_EPS = 1e-5
NEG_INF = -1e30
FORCED_SCORE = 1e6
GATE_PAD = 128
MASK_BIG = 2.0 ** 100
FEAT_HI, FEAT_LO = 64, 65
SB_DEAD_LOG = -104.0
SB_FIRST_TILES = 3

VMEM_LIMIT = 56 * 1024 * 1024

ATTN_TQ = 128
NSA_TQ = 256
SEL_TK = 512
PROJ_TM = 256
MERGE_TM = 256
ROUTER_TM = 512
EXPERT_TM = 512
OUT_TM = 256


def _cparams(sem):
    return pltpu.CompilerParams(dimension_semantics=sem, vmem_limit_bytes=VMEM_LIMIT)


def _sigmoid(x):
    return 1.0 / (1.0 + jnp.exp(-x))


def _div_pow2(x, n):
    assert n & (n - 1) == 0
    return lax.shift_right_logical(x, n.bit_length() - 1)


def _mod_pow2(x, n):
    assert n & (n - 1) == 0
    return x & (n - 1)


def _split_bf16(x):
    hi = x.astype(BF16)
    lo = (x - hi.astype(F32)).astype(BF16)
    return hi, lo


def _dot(a, b):
    return jnp.dot(a, b, preferred_element_type=F32)


def _dot_nt(a, b):
    return lax.dot_general(a, b, (((1,), (1,)), ((), ())), preferred_element_type=F32)


def _dot_tn(a, b):
    return lax.dot_general(a, b, (((0,), (0,)), ((), ())), preferred_element_type=F32)


def _layer_norm(u, g, b):
    mu = jnp.mean(u, axis=-1, keepdims=True)
    d = u - mu
    var = jnp.mean(d * d, axis=-1, keepdims=True)
    return d * lax.rsqrt(var + LN_EPS) * g + b


def _in_proj_kernel(x_ref, *refs, scales, transposed):
    n = len(scales)
    x = x_ref[...].astype(BF16)
    for w_ref, o_ref, s, tr in zip(refs[:n], refs[n:], scales, transposed):
        acc = _dot_nt(w_ref[...], x) if tr else _dot(x, w_ref[...])
        if s != 1.0:
            acc = acc * s
        o_ref[...] = acc.astype(o_ref.dtype)


def _in_proj(h, weights, dtypes, scales, transposed):
    t, d = h.shape
    tm = PROJ_TM
    in_specs = [pl.BlockSpec((tm, d), lambda i: (i, 0))]
    in_specs += [pl.BlockSpec(w.shape, lambda i: (0, 0)) for w in weights]
    out_specs, out_shape = [], []
    for w, dt, tr in zip(weights, dtypes, transposed):
        if tr:
            out_specs.append(pl.BlockSpec((w.shape[0], tm), lambda i: (0, i)))
            out_shape.append(jax.ShapeDtypeStruct((w.shape[0], t), dt))
        else:
            out_specs.append(pl.BlockSpec((tm, w.shape[1]), lambda i: (i, 0)))
            out_shape.append(jax.ShapeDtypeStruct((t, w.shape[1]), dt))
    return pl.pallas_call(
        functools.partial(_in_proj_kernel, scales=tuple(scales), transposed=tuple(transposed)),
        grid=(t // tm,), in_specs=in_specs, out_specs=out_specs, out_shape=out_shape,
        compiler_params=_cparams(("parallel",)), name="in_proj",
    )(h, *weights)


def _stack_heads(q2, tq):
    lane = lax.broadcasted_iota(jnp.int32, (tq, LANES), 1)
    zero = jnp.zeros_like(q2)
    return jnp.concatenate([jnp.where(lane < HEAD_DIM, q2, zero),
                            jnp.where(lane >= HEAD_DIM, q2, zero)], axis=0)


def _unstack_heads(o, tq):
    lane = lax.broadcasted_iota(jnp.int32, (tq, LANES), 1)
    return jnp.where(lane < HEAD_DIM, o[:tq], o[tq:])


def _sb_kernel(q_ref, k_ref, v_ref, o_ref, *, tq, n_pairs):
    i = pl.program_id(1)
    r_i = lax.broadcasted_iota(jnp.int32, (tq, tq), 0)
    c_i = lax.broadcasted_iota(jnp.int32, (tq, tq), 1)
    tri = jnp.where(r_i > c_i, 1.0, 0.0).astype(BF16)
    rhs = jnp.concatenate([tri, jnp.ones((tq, tq), BF16)], axis=1)
    rhs = jnp.concatenate([rhs, rhs], axis=0)
    qs = [_stack_heads(q_ref[:, p * LANES:(p + 1) * LANES], tq) for p in range(n_pairs)]

    def log_stay(z):
        return -(jnp.maximum(z, 0.0) + jnp.log(1.0 + jnp.exp(-jnp.abs(z))))

    def tile_sums(ls):
        hi, lo = _split_bf16(ls)
        return _dot(jnp.concatenate([hi, lo], axis=1), rhs)

    wide = SB_FIRST_TILES * tq
    first = jnp.maximum(i - (SB_FIRST_TILES - 1), 0)
    start0 = pl.multiple_of(first * tq, tq)
    col_minus_row = (lax.broadcasted_iota(jnp.int32, (2 * tq, wide), 1)
                     - (lax.broadcasted_iota(jnp.int32, (2 * tq, wide), 0) & (tq - 1)))
    causal = col_minus_row < (i - first) * tq

    def first_step(p):
        kw = k_ref[pl.ds(start0, wide), p * LANES:(p + 1) * LANES]
        vw = v_ref[pl.ds(start0, wide), p * LANES:(p + 1) * LANES]
        z = _dot_nt(qs[p], kw)
        ls = jnp.where(causal, log_stay(z), 0.0)
        sums = [tile_sums(ls[:, j * tq:(j + 1) * tq]) for j in range(SB_FIRST_TILES)]
        later, run = [None] * SB_FIRST_TILES, None
        for j in reversed(range(SB_FIRST_TILES)):
            later[j] = sums[j][:, :tq] if run is None else sums[j][:, :tq] + run
            run = sums[j][:, tq:] if run is None else run + sums[j][:, tq:]
        w = jnp.where(causal, jnp.exp(z + ls + jnp.concatenate(later, axis=1)), 0.0)
        return run, _dot(w.astype(BF16), vw)

    def tile(p, j, carry, acc):
        start = pl.multiple_of(j * tq, tq)
        kj = k_ref[pl.ds(start, tq), p * LANES:(p + 1) * LANES]
        vj = v_ref[pl.ds(start, tq), p * LANES:(p + 1) * LANES]
        z = _dot_nt(qs[p], kj)
        ls = log_stay(z)
        sums = tile_sums(ls)
        w = jnp.exp(z + ls + sums[:, :tq] + carry)
        return carry + sums[:, tq:], acc + _dot(w.astype(BF16), vj)

    def live(carries):
        worst = functools.reduce(jnp.maximum, carries)
        return jnp.max(worst) > SB_DEAD_LOG

    state = [first_step(p) for p in range(n_pairs)]
    carries = tuple(s[0] for s in state)
    accs = tuple(s[1] for s in state)

    def cond(c):
        return jnp.logical_and(c[0] <= first, c[1])

    def body(c):
        step, _, carries, accs = c
        state = [tile(p, first - step, carries[p], accs[p]) for p in range(n_pairs)]
        carries = tuple(s[0] for s in state)
        return step + 1, live(carries), carries, tuple(s[1] for s in state)

    _, _, _, accs = lax.while_loop(cond, body, (jnp.int32(1), live(carries), carries, accs))
    for p in range(n_pairs):
        o_ref[:, p * LANES:(p + 1) * LANES] = _unstack_heads(accs[p], tq).astype(o_ref.dtype)


def _sb_attention(q, k, v):
    b, s, w = q.shape
    tq = ATTN_TQ
    return pl.pallas_call(
        functools.partial(_sb_kernel, tq=tq, n_pairs=w // LANES),
        grid=(b, s // tq),
        in_specs=[pl.BlockSpec((None, tq, w), lambda bi, i: (bi, i, 0)),
                  pl.BlockSpec((None, s, w), lambda bi, i: (bi, 0, 0)),
                  pl.BlockSpec((None, s, w), lambda bi, i: (bi, 0, 0))],
        out_specs=pl.BlockSpec((None, tq, w), lambda bi, i: (bi, i, 0)),
        out_shape=jax.ShapeDtypeStruct((b, s, w), BF16),
        compiler_params=_cparams(("parallel", "arbitrary")), name="sb_attn",
    )(q, k, v)


def _compress_kernel(x_ref, pe_ref, w1_ref, w2_ref, o_ref):
    x = (x_ref[...] + pe_ref[...]).astype(BF16)
    hid = _dot(x, w1_ref[...])
    hid = hid * _sigmoid(hid)
    o_ref[...] = _dot(hid.astype(BF16), w2_ref[...]).astype(o_ref.dtype)


def _compress(x, pe, w1, w2):
    b, half, width = x.shape
    blk = width // 2
    out = pl.pallas_call(
        _compress_kernel,
        grid=(b, 2),
        in_specs=[pl.BlockSpec((None, half, blk), lambda bi, par: (bi, 0, par)),
                  pl.BlockSpec(pe.shape, lambda bi, par: (0, 0)),
                  pl.BlockSpec(w1.shape, lambda bi, par: (0, 0)),
                  pl.BlockSpec(w2.shape, lambda bi, par: (0, 0))],
        out_specs=pl.BlockSpec((None, None, half, w2.shape[1]), lambda bi, par: (bi, par, 0, 0)),
        out_shape=jax.ShapeDtypeStruct((b, 2, half, w2.shape[1]), BF16),
        compiler_params=_cparams(("parallel", "parallel")), name="compress",
    )(x, pe, w1, w2)
    return out.reshape(b, 2 * half, w2.shape[1])


def _stack_group(q, tq):
    return jnp.concatenate([_stack_heads(q[:, :LANES], tq), _stack_heads(q[:, LANES:], tq)], axis=0)


def _unstack_group(o, tq):
    return jnp.concatenate([_unstack_heads(o[:2 * tq], tq), _unstack_heads(o[2 * tq:], tq)], axis=1)


def _alibi_slope(head_idx):
    return lax.bitcast_convert_type(lax.shift_left(126 - head_idx, 23), F32)


def _expand_gate(gate_sig, g, branch):
    c_i = lax.broadcasted_iota(jnp.int32, (GATE_PAD, 2 * LANES), 0)
    l_i = lax.broadcasted_iota(jnp.int32, (GATE_PAD, 2 * LANES), 1)
    col = g * (3 * NSA_GROUP) + _div_pow2(l_i, HEAD_DIM) * 3 + branch
    onehot = jnp.where(c_i == col, 1.0, 0.0).astype(BF16)
    hi, lo = _split_bf16(gate_sig)
    return _dot(hi, onehot) + _dot(lo, onehot)


def _nsa_cmp_kernel(q_ref, kc_ref, vc_ref, gate_ref, y_ref, sel_ref, flag_ref, *, tq, n_sel):
    g = pl.program_id(1)
    i = pl.program_id(2)
    nc = kc_ref.shape[0]
    half = nc // 2
    qst = _stack_group(q_ref[...], tq)
    s = _dot_nt(kc_ref[...], qst)
    r_i = lax.broadcasted_iota(jnp.int32, (nc, 4 * tq), 0)
    l_i = lax.broadcasted_iota(jnp.int32, (nc, 4 * tq), 1)
    cblk = 2 * _mod_pow2(r_i, half) + _div_pow2(r_i, half)
    c_end = cblk * CMP_BLOCK + (CMP_BLOCK - 1)
    t = i * tq + _mod_pow2(l_i, tq)
    dist = t - c_end
    slope = _alibi_slope(g * NSA_GROUP + _div_pow2(l_i, tq))
    s = s - slope * dist.astype(F32)
    mask = dist >= 0
    m = jnp.max(jnp.where(mask, s, NEG_INF), axis=0, keepdims=True)
    p = jnp.where(mask, jnp.exp(s - m), 0.0)
    l = jnp.sum(p, axis=0, keepdims=True)
    p = p * jnp.where(l > 0.0, 1.0 / l, 0.0)

    o = _dot_tn(p.astype(BF16), vc_ref[...])
    gate = _expand_gate(_sigmoid(gate_ref[...]), g, 0)
    y_ref[...] = (_unstack_group(o, tq) * gate).astype(y_ref.dtype)

    imp_c = p[:, 0:tq] + p[:, tq:2 * tq] + p[:, 2 * tq:3 * tq] + p[:, 3 * tq:4 * tq]
    imp = imp_c[:half] + imp_c[half:]
    n_blk = half
    blk = lax.broadcasted_iota(jnp.int32, (n_blk, tq), 0)
    tt = i * tq + lax.broadcasted_iota(jnp.int32, (n_blk, tq), 1)
    cur = _div_pow2(tt, SEL_BLOCK)
    score = jnp.where(blk * SEL_BLOCK <= tt, imp, NEG_INF)
    for forced_blk in (0, cur, cur - 1):
        score = jnp.where(blk == forced_blk, FORCED_SCORE, score)

    taken = -jnp.inf
    sel = jnp.zeros((n_blk, tq), F32)
    for _ in range(n_sel):
        mx = jnp.max(score, axis=0, keepdims=True)
        first = jnp.min(jnp.where(score == mx, blk, n_blk), axis=0, keepdims=True)
        hit = blk == first
        sel = jnp.where(hit, 1.0, sel)
        score = jnp.where(hit, taken, score)
    sel = jnp.concatenate([sel, jnp.zeros((LANES - n_blk, tq), F32)], axis=0)
    sel_ref[...] = sel.T
    picks = _dot_nt(jnp.ones((8, tq), BF16), sel.astype(BF16))
    flag_ref[...] = picks[0:1].astype(jnp.int32)


def _nsa_cmp(q, kc, vc, gate):
    b, s, _ = q.shape
    tq = ATTN_TQ
    nc = kc.shape[1]
    n_blk = nc // 2
    assert n_blk <= LANES
    n_sel = min(N_SEL, n_blk)
    return pl.pallas_call(
        functools.partial(_nsa_cmp_kernel, tq=tq, n_sel=n_sel),
        grid=(b, NSA_KV_GROUPS, s // tq),
        in_specs=[pl.BlockSpec((None, tq, 2 * LANES), lambda bi, g, i: (bi, i, g)),
                  pl.BlockSpec((None, nc, LANES), lambda bi, g, i: (bi, 0, g)),
                  pl.BlockSpec((None, nc, LANES), lambda bi, g, i: (bi, 0, g)),
                  pl.BlockSpec((None, tq, GATE_PAD), lambda bi, g, i: (bi, i, 0))],
        out_specs=[pl.BlockSpec((None, tq, 2 * LANES), lambda bi, g, i: (bi, i, g)),
                   pl.BlockSpec((None, None, tq, LANES), lambda bi, g, i: (bi, g, i, 0)),
                   pl.BlockSpec((None, None, None, 1, LANES), lambda bi, g, i: (bi, g, i, 0, 0))],
        out_shape=[jax.ShapeDtypeStruct((b, s, 4 * LANES), F32),
                   jax.ShapeDtypeStruct((b, NSA_KV_GROUPS, s, LANES), F32),
                   jax.ShapeDtypeStruct((b, NSA_KV_GROUPS, s // tq, 1, LANES), jnp.int32)],
        compiler_params=_cparams(("parallel", "parallel", "parallel")), name="nsa_cmp",
    )(q, kc, vc, gate)


def _key_features(s, tk, wk):
    def table(n, period, with_blocks):
        key = lax.broadcasted_iota(jnp.int32, (n, LANES), 0)
        lane = lax.broadcasted_iota(jnp.int32, (n, LANES), 1)
        r = key % period
        pos = jnp.where(lane == FEAT_HI, r // SEL_BLOCK, jnp.where(lane == FEAT_LO, r % SEL_BLOCK, 0))
        if with_blocks:
            pos = jnp.where(lane < SEL_BLOCK, (key // SEL_BLOCK == lane).astype(jnp.int32), pos)
        return pos.astype(BF16)
    return table(s, tk, True), table(wk, wk, False)


def _nsa_main_kernel(flag_ref, q_ref, ks_ref, kw_ref, vs_ref, vw_ref, kfeat_ref, rfeat_ref, sel_ref,
                     gate_ref, ycmp_ref, o_ref, m_ref, l_ref, acc_ref, *, tq, tk):
    g = pl.program_id(1)
    i = pl.program_id(2)
    cols = NSA_GROUP * tq
    qs = pl.multiple_of(i * tq, tq)
    qst = _stack_group(q_ref[...], tq)
    col = lax.broadcasted_iota(jnp.int32, (1, cols), 1)
    t_loc = _mod_pow2(col, tq)
    slope_row = _alibi_slope(g * NSA_GROUP + _div_pow2(col, tq))

    lane = lax.broadcasted_iota(jnp.int32, (tq, LANES), 1)
    sel = sel_ref[...]
    first_blk = i * (tq // SEL_BLOCK)
    add_diag = jnp.where(sel > 0.5, 0.0, -MASK_BIG)
    add_bulk = jnp.where(lane < first_blk, add_diag, -MASK_BIG)
    f_bulk, f_diag, f_win = [], [], []
    for h in range(NSA_GROUP):
        slope = _alibi_slope(jnp.full((tq, LANES), g * NSA_GROUP + h, jnp.int32))
        pos = jnp.where(lane == FEAT_HI, slope * SEL_BLOCK, jnp.where(lane == FEAT_LO, slope, 0.0))
        f_win.append(pos)
        f_bulk.append(jnp.where(lane < SEL_BLOCK, add_bulk, pos))
        f_diag.append(jnp.where(lane < SEL_BLOCK, add_diag, pos))
    aug = lambda f: jnp.concatenate([qst, jnp.concatenate(f, axis=0).astype(BF16)], axis=1)
    q_sel, q_diag, q_win = aug(f_bulk), aug(f_diag), aug(f_win)

    def online(k, q_aug, mask_add, bias, v_t):
        s = _dot_nt(k, q_aug)
        if mask_add is not None:
            s = s + mask_add
        m = m_ref[...]
        m_new = jnp.maximum(m, jnp.max(s, axis=0, keepdims=True) + bias)
        alpha = jnp.exp(m - m_new)
        p = jnp.exp(s + (bias - m_new))
        m_ref[...] = m_new
        l_ref[...] = alpha * l_ref[...] + jnp.sum(p, axis=0, keepdims=True)
        acc_ref[...] = alpha * acc_ref[...] + _dot(v_t, p.astype(BF16))

    def untranspose(o_t):
        return jnp.concatenate([o_t[:, h * tq:(h + 1) * tq].T for h in range(NSA_GROUP)], axis=0)

    r_i = lax.broadcasted_iota(jnp.int32, (tq, tq), 0)
    c_i = lax.broadcasted_iota(jnp.int32, (tq, tq), 1)
    causal_add = jnp.where(r_i <= c_i, 0.0, -MASK_BIG)
    causal_add = jnp.concatenate([causal_add] * NSA_GROUP, axis=1)
    m_ref[...] = jnp.full((1, cols), NEG_INF, F32)
    l_ref[...] = jnp.zeros((1, cols), F32)
    acc_ref[...] = jnp.zeros((LANES, cols), F32)
    k = jnp.concatenate([ks_ref[pl.ds(qs, tq), :], kfeat_ref[pl.ds(qs, tq), :]], axis=1)
    bias = -slope_row * ((qs & (tk - 1)) + t_loc).astype(F32)
    online(k, q_diag, causal_add, bias, vs_ref[:, pl.ds(qs, tq)])

    def bulk(kt, _):
        start = pl.multiple_of(kt * tk, tk)
        picked = 0
        for sub in range(flag_ref.shape[0]):
            for j in range(tk // SEL_BLOCK):
                picked = picked + flag_ref[sub, 0, kt * (tk // SEL_BLOCK) + j]

        @pl.when(picked > 0)
        def _():
            k = jnp.concatenate([ks_ref[pl.ds(start, tk), :], kfeat_ref[pl.ds(start, tk), :]], axis=1)
            bias = -slope_row * (qs - start + t_loc).astype(F32)
            online(k, q_sel, None, bias, vs_ref[:, pl.ds(start, tk)])

        return 0

    lax.fori_loop(0, (qs + tk - 1) // tk, bulk, 0)
    o_sel = untranspose(acc_ref[...] * (1.0 / l_ref[...]))

    wk = WINDOW + tq
    start = pl.multiple_of(jnp.maximum(qs - WINDOW, 0), tq)
    w_r = lax.broadcasted_iota(jnp.int32, (wk, tq), 0)
    w_c = lax.broadcasted_iota(jnp.int32, (wk, tq), 1)
    dist = (qs - start) + w_c - w_r
    band_add = jnp.where(jnp.where(dist >= 0, dist, WINDOW) < WINDOW, 0.0, -MASK_BIG)
    band_add = jnp.concatenate([band_add] * NSA_GROUP, axis=1)
    k = jnp.concatenate([kw_ref[pl.ds(start, wk), :], rfeat_ref[...]], axis=1)
    s = _dot_nt(k, q_win) + band_add
    p = jnp.exp(s - jnp.max(s, axis=0, keepdims=True))
    l_w = jnp.sum(p, axis=0, keepdims=True)
    o_win = untranspose(_dot(vw_ref[:, pl.ds(start, wk)], p.astype(BF16)) * (1.0 / l_w))

    gate = _sigmoid(gate_ref[...])
    y = (ycmp_ref[...] + _expand_gate(gate, g, 1) * _unstack_group(o_sel, tq)
         + _expand_gate(gate, g, 2) * _unstack_group(o_win, tq))
    o_ref[...] = y.astype(o_ref.dtype)


def _nsa_main(q, k_sw, v_sw_t, sel, flags, gate, ycmp):
    b, s, _ = q.shape
    tq = NSA_TQ
    tk = min(SEL_TK, s)
    wk = WINDOW + tq
    cols = NSA_GROUP * tq
    flag_rows = tq // ATTN_TQ
    assert s >= wk and s % tk == 0 and tk % tq == 0 and s // SEL_BLOCK <= SEL_BLOCK
    kfeat, rfeat = _key_features(s, tk, wk)
    k_spec = lambda off: pl.BlockSpec((None, s, LANES), lambda bi, g, i: (bi, 0, off + g))
    v_spec = lambda off: pl.BlockSpec((LANES, s), lambda bi, g, i: (off + g, bi))
    return pl.pallas_call(
        functools.partial(_nsa_main_kernel, tq=tq, tk=tk),
        grid=(b, NSA_KV_GROUPS, s // tq),
        in_specs=[pl.BlockSpec((None, None, flag_rows, 1, LANES), lambda bi, g, i: (bi, g, i, 0, 0),
                               memory_space=pltpu.SMEM),
                  pl.BlockSpec((None, tq, 2 * LANES), lambda bi, g, i: (bi, i, g)),
                  k_spec(0), k_spec(2), v_spec(0), v_spec(2),
                  pl.BlockSpec((s, LANES), lambda bi, g, i: (0, 0)),
                  pl.BlockSpec((wk, LANES), lambda bi, g, i: (0, 0)),
                  pl.BlockSpec((None, None, tq, LANES), lambda bi, g, i: (bi, g, i, 0)),
                  pl.BlockSpec((None, tq, GATE_PAD), lambda bi, g, i: (bi, i, 0)),
                  pl.BlockSpec((None, tq, 2 * LANES), lambda bi, g, i: (bi, i, g))],
        out_specs=pl.BlockSpec((None, tq, 2 * LANES), lambda bi, g, i: (bi, i, g)),
        out_shape=jax.ShapeDtypeStruct((b, s, 4 * LANES), BF16),
        scratch_shapes=[pltpu.VMEM((1, cols), F32), pltpu.VMEM((1, cols), F32), pltpu.VMEM((LANES, cols), F32)],
        compiler_params=_cparams(("parallel", "parallel", "arbitrary")), name="nsa_main",
    )(flags, q, k_sw, k_sw, v_sw_t, v_sw_t, kfeat, rfeat, sel, gate, ycmp)


def _merge_kernel(ysb_ref, ynsa_ref, g_ref, h_ref, wsb_ref, wnsa_ref, wo_ref, lng_ref, lnb_ref,
                  h1_ref, h1b_ref, *, alpha):
    d = h_ref.shape[1]
    g = g_ref[...]
    merged = (_sigmoid(g[:, :d]) * _dot(ysb_ref[...], wsb_ref[...])
              + _sigmoid(g[:, d:]) * _dot(ynsa_ref[...], wnsa_ref[...]))
    u = alpha * h_ref[...] + _dot(merged.astype(BF16), wo_ref[...])
    h1 = _layer_norm(u, lng_ref[...], lnb_ref[...])
    h1_ref[...] = h1
    h1b_ref[...] = h1.astype(BF16)


def _merge_ln1(ysb, ynsa, g, h, wsb, wnsa, wo, lng, lnb, alpha):
    t, d = h.shape
    tm = MERGE_TM
    row = lambda w: pl.BlockSpec((tm, w), lambda i: (i, 0))
    full = lambda a: pl.BlockSpec(a.shape, lambda i: (0, 0))
    return pl.pallas_call(
        functools.partial(_merge_kernel, alpha=alpha),
        grid=(t // tm,),
        in_specs=[row(ysb.shape[1]), row(ynsa.shape[1]), row(2 * d), row(d),
                  full(wsb), full(wnsa), full(wo), full(lng), full(lnb)],
        out_specs=[row(d), row(d)],
        out_shape=[jax.ShapeDtypeStruct((t, d), F32), jax.ShapeDtypeStruct((t, d), BF16)],
        compiler_params=_cparams(("parallel",)), name="merge_ln1",
    )(ysb, ynsa, g, h, wsb, wnsa, wo, lng, lnb)


def _router_kernel(x_ref, w_ref, b_ref, idx_ref, wt_ref):
    logits = _dot_nt(w_ref[...], x_ref[...]) + b_ref[...]
    n_e, tm = logits.shape
    e_i = lax.broadcasted_iota(jnp.int32, (n_e, tm), 0)
    vals, idxs = [], []
    for _ in range(TOP_K):
        mx = jnp.max(logits, axis=0, keepdims=True)
        first = jnp.min(jnp.where(logits == mx, e_i, n_e), axis=0, keepdims=True)
        vals.append(mx)
        idxs.append(first)
        logits = jnp.where(e_i == first, -jnp.inf, logits)
    ex = [jnp.exp(v - vals[0]) for v in vals]
    inv = 1.0 / (ex[0] + ex[1] + ex[2] + ex[3])
    idx_ref[...] = jnp.concatenate(idxs, axis=0)
    wt_ref[...] = jnp.concatenate([e * inv for e in ex], axis=0)


def _router(h1b, w_t, b_col):
    t, d = h1b.shape
    tm = ROUTER_TM
    return pl.pallas_call(
        _router_kernel,
        grid=(t // tm,),
        in_specs=[pl.BlockSpec((tm, d), lambda i: (i, 0)),
                  pl.BlockSpec(w_t.shape, lambda i: (0, 0)),
                  pl.BlockSpec(b_col.shape, lambda i: (0, 0))],
        out_specs=[pl.BlockSpec((TOP_K, tm), lambda i: (0, i)),
                   pl.BlockSpec((TOP_K, tm), lambda i: (0, i))],
        out_shape=[jax.ShapeDtypeStruct((TOP_K, t), jnp.int32),
                   jax.ShapeDtypeStruct((TOP_K, t), F32)],
        compiler_params=_cparams(("parallel",)), name="router",
    )(h1b, w_t, b_col)


def _expert_kernel(te_ref, nv_ref, x_ref, wgu_ref, bgu_ref, wd_ref, bd_ref, y_ref, wgu_bf, wd_bf):
    i = pl.program_id(0)

    @pl.when(jnp.logical_or(i == 0, te_ref[i] != te_ref[jnp.maximum(i - 1, 0)]))
    def _():
        wgu_bf[...] = wgu_ref[...].astype(BF16)
        wd_bf[...] = wd_ref[...].astype(BF16)

    @pl.when(i < nv_ref[0])
    def _():
        gu = _dot(x_ref[...].astype(BF16), wgu_bf[...]) + bgu_ref[...]
        gate = jnp.minimum(gu[:, :D_FF], SWIGLU_LIMIT)
        up = jnp.clip(gu[:, D_FF:], -SWIGLU_LIMIT, SWIGLU_LIMIT)
        act = (up + 1.0) * gate * _sigmoid(SWIGLU_ALPHA * gate)
        y_ref[...] = (_dot(act.astype(BF16), wd_bf[...]) + bd_ref[...]).astype(y_ref.dtype)

    @pl.when(i >= nv_ref[0])
    def _():
        y_ref[...] = jnp.zeros_like(y_ref)


def _experts(tile_expert, n_valid, xs, wgu, bgu, wd, bd, layer):
    p, d = xs.shape
    tm = EXPERT_TM
    two_ff = wgu.shape[-1]
    grid_spec = pltpu.PrefetchScalarGridSpec(
        num_scalar_prefetch=2, grid=(p // tm,),
        in_specs=[pl.BlockSpec((tm, d), lambda i, te, nv: (i, 0)),
                  pl.BlockSpec((None, None, d, two_ff), lambda i, te, nv: (layer, te[i], 0, 0)),
                  pl.BlockSpec((None, 1, two_ff), lambda i, te, nv: (te[i], 0, 0)),
                  pl.BlockSpec((None, None, two_ff // 2, d), lambda i, te, nv: (layer, te[i], 0, 0)),
                  pl.BlockSpec((None, 1, d), lambda i, te, nv: (te[i], 0, 0))],
        out_specs=pl.BlockSpec((tm, d), lambda i, te, nv: (i, 0)),
        scratch_shapes=[pltpu.VMEM((d, two_ff), BF16), pltpu.VMEM((two_ff // 2, d), BF16)])
    return pl.pallas_call(
        _expert_kernel, grid_spec=grid_spec,
        out_shape=jax.ShapeDtypeStruct((p, d), F32),
        compiler_params=_cparams(("arbitrary",)), name="experts",
    )(tile_expert, n_valid, xs, wgu, bgu, wd, bd)


def _out_kernel(h1_ref, h1b_ref, yg_ref, cw_ref, p_ref, wpg_ref, wpp_ref, lng_ref, lnb_ref, o_ref, *, alpha):
    cw = cw_ref[...]
    moe = cw[:, 0:1] * yg_ref[0]
    for k in range(1, TOP_K):
        moe = moe + cw[:, k:k + 1] * yg_ref[k]
    ple = _sigmoid(_dot(h1b_ref[...], wpg_ref[...])) * _dot(p_ref[...].astype(BF16), wpp_ref[...])
    u = alpha * h1_ref[...] + moe + ple
    o_ref[...] = _layer_norm(u, lng_ref[...], lnb_ref[...])


def _out_ln2(h1, h1b, yg, cw, p, wpg, wpp, lng, lnb, alpha):
    t, d = h1.shape
    tm = OUT_TM
    row = lambda w: pl.BlockSpec((tm, w), lambda i: (i, 0))
    full = lambda a: pl.BlockSpec(a.shape, lambda i: (0, 0))
    return pl.pallas_call(
        functools.partial(_out_kernel, alpha=alpha),
        grid=(t // tm,),
        in_specs=[row(d), row(d), pl.BlockSpec((TOP_K, tm, d), lambda i: (0, i, 0)), row(TOP_K),
                  row(p.shape[1]), full(wpg), full(wpp), full(lng), full(lnb)],
        out_specs=row(d),
        out_shape=jax.ShapeDtypeStruct((t, d), F32),
        compiler_params=_cparams(("parallel",)), name="out_ln2",
    )(h1, h1b, yg, cw, p, wpg, wpp, lng, lnb)


def _dup_groups(w):
    d = w.shape[0]
    w = w.reshape(d, NSA_KV_GROUPS, 1, HEAD_DIM)
    return jnp.broadcast_to(w, (d, NSA_KV_GROUPS, 2, HEAD_DIM)).reshape(d, NSA_KV_GROUPS * LANES)


def _in_proj_weights(w_in):
    sbw = SB_HEADS * HEAD_DIM
    qw = NSA_HEADS * HEAD_DIM
    kvw = NSA_KV_GROUPS * HEAD_DIM
    d = w_in.shape[0]
    widths = (sbw, sbw, sbw, qw) + (kvw,) * 6 + (3 * NSA_HEADS, d, d)
    parts, start = [], 0
    for wd in widths:
        parts.append(w_in[:, start:start + wd])
        start += wd
    (sb_q, sb_k, sb_v, nsa_q, k_cmp, v_cmp, k_sel, v_sel, k_win, v_win, gate, g_sb, g_nsa) = parts
    k_sw = jnp.concatenate([_dup_groups(k_sel), _dup_groups(k_win)], axis=1)
    v_sw_t = jnp.concatenate([_dup_groups(v_sel), _dup_groups(v_win)], axis=1).T
    gate = jnp.pad(gate, ((0, 0), (0, GATE_PAD - gate.shape[1])))
    g_both = jnp.concatenate([g_sb, g_nsa], axis=1)
    ws = [sb_q, sb_k, sb_v, nsa_q, k_cmp, v_cmp, k_sw, v_sw_t, gate, g_both]
    dtypes = [BF16, BF16, BF16, BF16, F32, F32, BF16, BF16, F32, F32]
    transposed = [False] * 7 + [True, False, False]
    scale = HEAD_DIM ** -0.5
    scales = [scale, 1.0, 1.0, scale] + [1.0] * 6
    return [w.astype(BF16) for w in ws], dtypes, scales, transposed


def _compress_weights(w1, w2, pe):
    eye = jnp.eye(NSA_KV_GROUPS, dtype=w1.dtype)
    w1b = jnp.einsum('ldh,pg->lpdgh', w1, eye).reshape(CMP_BLOCK * NSA_KV_GROUPS * HEAD_DIM,
                                                       NSA_KV_GROUPS * CMP_HIDDEN)
    w2b = jnp.einsum('hd,pg,r->phgrd', w2, eye, jnp.ones((2,), w2.dtype)).reshape(
        NSA_KV_GROUPS * CMP_HIDDEN, NSA_KV_GROUPS * LANES)
    peb = jnp.broadcast_to(pe[:, None, :], (CMP_BLOCK, NSA_KV_GROUPS, HEAD_DIM)).reshape(1, -1)
    return peb.astype(F32), w1b.astype(BF16), w2b.astype(BF16)


def _moe_plan(idx_t, tm):
    k, t = idx_t.shape
    n = k * t
    i32 = jnp.int32
    experts = jnp.arange(N_EXPERTS, dtype=i32)
    e_flat = idx_t.reshape(n)
    counts = jnp.sum((e_flat[None, :] == experts[:, None]).astype(i32), axis=1)
    padded = ((counts + tm - 1) // tm) * tm
    pad_end = jnp.cumsum(padded)
    gap = padded - counts
    n_pad = N_EXPERTS * tm
    p_rows = n + n_pad
    id_bits = (p_rows - 1).bit_length()
    assert (N_EXPERTS + 1) << id_bits < 2 ** 31
    spare_e = jnp.repeat(experts, tm, total_repeat_length=n_pad)
    spare_used = jnp.tile(jnp.arange(tm, dtype=i32), N_EXPERTS) < jnp.repeat(gap, tm, total_repeat_length=n_pad)
    keys = jnp.concatenate([e_flat, jnp.where(spare_used, spare_e, N_EXPERTS)])
    rows = jnp.arange(p_rows, dtype=i32)
    entry = lax.sort(keys * (1 << id_bits) + rows) & ((1 << id_bits) - 1)
    src_tok = jnp.where(entry < n, entry % t, 0)
    _, row_of_entry = lax.sort((entry, rows), num_keys=1)
    pos = row_of_entry[:n]
    tile_start = jnp.arange(p_rows // tm, dtype=i32) * tm
    tile_expert = jnp.minimum(jnp.sum((tile_start[:, None] >= pad_end[None, :]).astype(i32), axis=1), N_EXPERTS - 1)
    n_valid = (pad_end[-1] // tm).astype(i32).reshape(1)
    return src_tok, pos, tile_expert, n_valid


def _layer(h, p_i, w_in, w_cmp1, w_cmp2, pe_cmp, w_br_sb, w_br_nsa, w_o, ln1_g, ln1_b, ln2_g, ln2_b,
           w_router, b_router, w_gu_all, b_gu, w_down_all, b_down, w_ple_gate, w_ple_proj,
           *, batch, alpha, layer):
    t, d = h.shape
    s = t // batch
    sb_q, sb_k, sb_v, nsa_q, k_cmp, v_cmp, k_sw, v_sw_t, gate, g_both = _in_proj(h, *_in_proj_weights(w_in))
    b3 = lambda a: a.reshape(batch, s, a.shape[1])

    y_sb = _sb_attention(b3(sb_q), b3(sb_k), b3(sb_v)).reshape(t, -1)

    nc = s // CMP_BLOCK
    cmp_rows = lambda a: a.reshape(batch, nc // 2, 2 * CMP_BLOCK * a.shape[1])
    kc = _compress(cmp_rows(k_cmp), *_compress_weights(w_cmp1[0], w_cmp2[0], pe_cmp[0]))
    vc = _compress(cmp_rows(v_cmp), *_compress_weights(w_cmp1[1], w_cmp2[1], pe_cmp[1]))
    y_cmp, sel, flags = _nsa_cmp(b3(nsa_q), kc, vc, b3(gate))
    y_nsa = _nsa_main(b3(nsa_q), b3(k_sw), v_sw_t, sel, flags, b3(gate), y_cmp).reshape(t, -1)

    row = lambda v: v.reshape(1, -1).astype(F32)
    h1, h1b = _merge_ln1(y_sb, y_nsa, g_both, h, w_br_sb.astype(BF16), w_br_nsa.astype(BF16),
                         w_o.astype(BF16), row(ln1_g), row(ln1_b), alpha)

    idx_t, wt_t = _router(h1b, w_router.T.astype(BF16), b_router.reshape(-1, 1).astype(F32))
    src_tok, pos, tile_expert, n_valid = _moe_plan(idx_t, EXPERT_TM)
    xs = h1.at[src_tok].get(mode='promise_in_bounds')
    y = _experts(tile_expert, n_valid, xs, w_gu_all, b_gu.reshape(N_EXPERTS, 1, -1),
                 w_down_all, b_down.reshape(N_EXPERTS, 1, -1), layer)
    yg = y.at[pos].get(mode='promise_in_bounds').reshape(TOP_K, t, d)
    return _out_ln2(h1, h1b, yg, wt_t.T, p_i, w_ple_gate.astype(BF16), w_ple_proj.astype(BF16),
                    row(ln2_g), row(ln2_b), alpha)


def kernel(x, p, w_in, w_cmp1, w_cmp2, pe_cmp, w_br_sb, w_br_nsa, w_o, ln1_g, ln1_b, ln2_g, ln2_b,
           w_router, b_router, w_gu, b_gu, w_down, b_down, w_ple_gate, w_ple_proj):
    batch, s, d = x.shape
    depth = w_in.shape[0]
    alpha = (2 * depth) ** 0.25
    h = x.reshape(batch * s, d)
    for i in range(depth):
        h = _layer(h, p[i].reshape(batch * s, -1), w_in[i], w_cmp1[i], w_cmp2[i], pe_cmp[i],
                   w_br_sb[i], w_br_nsa[i], w_o[i], ln1_g[i], ln1_b[i], ln2_g[i], ln2_b[i],
                   w_router[i], b_router[i], w_gu, b_gu[i], w_down, b_down[i],
                   w_ple_gate[i], w_ple_proj[i], batch=batch, alpha=alpha, layer=i)
    return h.reshape(batch, s, d)
```

```python
import functools

import jax
import jax.numpy as jnp
from jax import lax
from jax.experimental import pallas as pl
from jax.experimental.pallas import tpu as pltpu

F32 = jnp.float32
BF16 = jnp.bfloat16

HEAD_DIM = 64
LANES = 128
SB_HEADS = 8
NSA_HEADS = 8
NSA_KV_GROUPS = 2
NSA_GROUP = NSA_HEADS // NSA_KV_GROUPS
CMP_BLOCK = 32
CMP_HIDDEN = 128
SEL_BLOCK = 64
N_SEL = 16
WINDOW = 512
N_EXPERTS = 32
TOP_K = 4
D_FF = 1024
SWIGLU_ALPHA = 1.702
SWIGLU_LIMIT = 7.0
LN_EPS = 1e-5
NEG_INF = -1e30
FORCED_SCORE = 1e6
GATE_PAD = 128
MASK_BIG = 2.0 ** 100
FEAT_HI, FEAT_LO = 64, 65
SB_DEAD_LOG = -104.0
SB_FIRST_TILES = 3

VMEM_LIMIT = 56 * 1024 * 1024

ATTN_TQ = 128
NSA_TQ = 256
SEL_TK = 512
PROJ_TM = 256
MERGE_TM = 256
ROUTER_TM = 512
EXPERT_TM = 512
OUT_TM = 256


def _cparams(sem):
    return pltpu.CompilerParams(dimension_semantics=sem, vmem_limit_bytes=VMEM_LIMIT)


def _sigmoid(x):
    return 1.0 / (1.0 + jnp.exp(-x))


def _div_pow2(x, n):
    assert n & (n - 1) == 0
    return lax.shift_right_logical(x, n.bit_length() - 1)


def _mod_pow2(x, n):
    assert n & (n - 1) == 0
    return x & (n - 1)


def _split_bf16(x):
    hi = x.astype(BF16)
    lo = (x - hi.astype(F32)).astype(BF16)
    return hi, lo


def _dot(a, b):
    return jnp.dot(a, b, preferred_element_type=F32)


def _dot_nt(a, b):
    return lax.dot_general(a, b, (((1,), (1,)), ((), ())), preferred_element_type=F32)


def _dot_tn(a, b):
    return lax.dot_general(a, b, (((0,), (0,)), ((), ())), preferred_element_type=F32)


def _layer_norm(u, g, b):
    mu = jnp.mean(u, axis=-1, keepdims=True)
    d = u - mu
    var = jnp.mean(d * d, axis=-1, keepdims=True)
    return d * lax.rsqrt(var + LN_EPS) * g + b


def _in_proj_kernel(x_ref, *refs, scales, transposed):
    n = len(scales)
    x = x_ref[...].astype(BF16)
    for w_ref, o_ref, s, tr in zip(refs[:n], refs[n:], scales, transposed):
        acc = _dot_nt(w_ref[...], x) if tr else _dot(x, w_ref[...])
        if s != 1.0:
            acc = acc * s
        o_ref[...] = acc.astype(o_ref.dtype)


def _in_proj(h, weights, dtypes, scales, transposed):
    t, d = h.shape
    tm = PROJ_TM
    in_specs = [pl.BlockSpec((tm, d), lambda i: (i, 0))]
    in_specs += [pl.BlockSpec(w.shape, lambda i: (0, 0)) for w in weights]
    out_specs, out_shape = [], []
    for w, dt, tr in zip(weights, dtypes, transposed):
        if tr:
            out_specs.append(pl.BlockSpec((w.shape[0], tm), lambda i: (0, i)))
            out_shape.append(jax.ShapeDtypeStruct((w.shape[0], t), dt))
        else:
            out_specs.append(pl.BlockSpec((tm, w.shape[1]), lambda i: (i, 0)))
            out_shape.append(jax.ShapeDtypeStruct((t, w.shape[1]), dt))
    return pl.pallas_call(
        functools.partial(_in_proj_kernel, scales=tuple(scales), transposed=tuple(transposed)),
        grid=(t // tm,), in_specs=in_specs, out_specs=out_specs, out_shape=out_shape,
        compiler_params=_cparams(("parallel",)), name="in_proj",
    )(h, *weights)


def _stack_heads(q2, tq):
    lane = lax.broadcasted_iota(jnp.int32, (tq, LANES), 1)
    zero = jnp.zeros_like(q2)
    return jnp.concatenate([jnp.where(lane < HEAD_DIM, q2, zero),
                            jnp.where(lane >= HEAD_DIM, q2, zero)], axis=0)


def _unstack_heads(o, tq):
    lane = lax.broadcasted_iota(jnp.int32, (tq, LANES), 1)
    return jnp.where(lane < HEAD_DIM, o[:tq], o[tq:])


def _sb_kernel(q_ref, k_ref, v_ref, o_ref, *, tq, n_pairs):
    i = pl.program_id(1)
    r_i = lax.broadcasted_iota(jnp.int32, (tq, tq), 0)
    c_i = lax.broadcasted_iota(jnp.int32, (tq, tq), 1)
    tri = jnp.where(r_i > c_i, 1.0, 0.0).astype(BF16)
    rhs = jnp.concatenate([tri, jnp.ones((tq, tq), BF16)], axis=1)
    rhs = jnp.concatenate([rhs, rhs], axis=0)
    qs = [_stack_heads(q_ref[:, p * LANES:(p + 1) * LANES], tq) for p in range(n_pairs)]

    def log_stay(z):
        return -(jnp.maximum(z, 0.0) + jnp.log(1.0 + jnp.exp(-jnp.abs(z))))

    def tile_sums(ls):
        hi, lo = _split_bf16(ls)
        return _dot(jnp.concatenate([hi, lo], axis=1), rhs)

    wide = SB_FIRST_TILES * tq
    first = jnp.maximum(i - (SB_FIRST_TILES - 1), 0)
    start0 = pl.multiple_of(first * tq, tq)
    col_minus_row = (lax.broadcasted_iota(jnp.int32, (2 * tq, wide), 1)
                     - (lax.broadcasted_iota(jnp.int32, (2 * tq, wide), 0) & (tq - 1)))
    causal = col_minus_row < (i - first) * tq

    def first_step(p):
        kw = k_ref[pl.ds(start0, wide), p * LANES:(p + 1) * LANES]
        vw = v_ref[pl.ds(start0, wide), p * LANES:(p + 1) * LANES]
        z = _dot_nt(qs[p], kw)
        ls = jnp.where(causal, log_stay(z), 0.0)
        sums = [tile_sums(ls[:, j * tq:(j + 1) * tq]) for j in range(SB_FIRST_TILES)]
        later, run = [None] * SB_FIRST_TILES, None
        for j in reversed(range(SB_FIRST_TILES)):
            later[j] = sums[j][:, :tq] if run is None else sums[j][:, :tq] + run
            run = sums[j][:, tq:] if run is None else run + sums[j][:, tq:]
        w = jnp.where(causal, jnp.exp(z + ls + jnp.concatenate(later, axis=1)), 0.0)
        return run, _dot(w.astype(BF16), vw)

    def tile(p, j, carry, acc):
        start = pl.multiple_of(j * tq, tq)
        kj = k_ref[pl.ds(start, tq), p * LANES:(p + 1) * LANES]
        vj = v_ref[pl.ds(start, tq), p * LANES:(p + 1) * LANES]
        z = _dot_nt(qs[p], kj)
        ls = log_stay(z)
        sums = tile_sums(ls)
        w = jnp.exp(z + ls + sums[:, :tq] + carry)
        return carry + sums[:, tq:], acc + _dot(w.astype(BF16), vj)

    def live(carries):
        worst = functools.reduce(jnp.maximum, carries)
        return jnp.max(worst) > SB_DEAD_LOG

    state = [first_step(p) for p in range(n_pairs)]
    carries = tuple(s[0] for s in state)
    accs = tuple(s[1] for s in state)

    def cond(c):
        return jnp.logical_and(c[0] <= first, c[1])

    def body(c):
        step, _, carries, accs = c
        state = [tile(p, first - step, carries[p], accs[p]) for p in range(n_pairs)]
        carries = tuple(s[0] for s in state)
        return step + 1, live(carries), carries, tuple(s[1] for s in state)

    _, _, _, accs = lax.while_loop(cond, body, (jnp.int32(1), live(carries), carries, accs))
    for p in range(n_pairs):
        o_ref[:, p * LANES:(p + 1) * LANES] = _unstack_heads(accs[p], tq).astype(o_ref.dtype)


def _sb_attention(q, k, v):
    b, s, w = q.shape
    tq = ATTN_TQ
    return pl.pallas_call(
        functools.partial(_sb_kernel, tq=tq, n_pairs=w // LANES),
        grid=(b, s // tq),
        in_specs=[pl.BlockSpec((None, tq, w), lambda bi, i: (bi, i, 0)),
                  pl.BlockSpec((None, s, w), lambda bi, i: (bi, 0, 0)),
                  pl.BlockSpec((None, s, w), lambda bi, i: (bi, 0, 0))],
        out_specs=pl.BlockSpec((None, tq, w), lambda bi, i: (bi, i, 0)),
        out_shape=jax.ShapeDtypeStruct((b, s, w), BF16),
        compiler_params=_cparams(("parallel", "arbitrary")), name="sb_attn",
    )(q, k, v)


def _compress_kernel(x_ref, pe_ref, w1_ref, w2_ref, o_ref):
    x = (x_ref[...] + pe_ref[...]).astype(BF16)
    hid = _dot(x, w1_ref[...])
    hid = hid * _sigmoid(hid)
    o_ref[...] = _dot(hid.astype(BF16), w2_ref[...]).astype(o_ref.dtype)


def _compress(x, pe, w1, w2):
    b, half, width = x.shape
    blk = width // 2
    out = pl.pallas_call(
        _compress_kernel,
        grid=(b, 2),
        in_specs=[pl.BlockSpec((None, half, blk), lambda bi, par: (bi, 0, par)),
                  pl.BlockSpec(pe.shape, lambda bi, par: (0, 0)),
                  pl.BlockSpec(w1.shape, lambda bi, par: (0, 0)),
                  pl.BlockSpec(w2.shape, lambda bi, par: (0, 0))],
        out_specs=pl.BlockSpec((None, None, half, w2.shape[1]), lambda bi, par: (bi, par, 0, 0)),
        out_shape=jax.ShapeDtypeStruct((b, 2, half, w2.shape[1]), BF16),
        compiler_params=_cparams(("parallel", "parallel")), name="compress",
    )(x, pe, w1, w2)
    return out.reshape(b, 2 * half, w2.shape[1])


def _stack_group(q, tq):
    return jnp.concatenate([_stack_heads(q[:, :LANES], tq), _stack_heads(q[:, LANES:], tq)], axis=0)


def _unstack_group(o, tq):
    return jnp.concatenate([_unstack_heads(o[:2 * tq], tq), _unstack_heads(o[2 * tq:], tq)], axis=1)


def _alibi_slope(head_idx):
    return lax.bitcast_convert_type(lax.shift_left(126 - head_idx, 23), F32)


def _expand_gate(gate_sig, g, branch):
    c_i = lax.broadcasted_iota(jnp.int32, (GATE_PAD, 2 * LANES), 0)
    l_i = lax.broadcasted_iota(jnp.int32, (GATE_PAD, 2 * LANES), 1)
    col = g * (3 * NSA_GROUP) + _div_pow2(l_i, HEAD_DIM) * 3 + branch
    onehot = jnp.where(c_i == col, 1.0, 0.0).astype(BF16)
    hi, lo = _split_bf16(gate_sig)
    return _dot(hi, onehot) + _dot(lo, onehot)


def _nsa_cmp_kernel(q_ref, kc_ref, vc_ref, gate_ref, y_ref, sel_ref, flag_ref, *, tq, n_sel):
    g = pl.program_id(1)
    i = pl.program_id(2)
    nc = kc_ref.shape[0]
    half = nc // 2
    qst = _stack_group(q_ref[...], tq)
    s = _dot_nt(kc_ref[...], qst)
    r_i = lax.broadcasted_iota(jnp.int32, (nc, 4 * tq), 0)
    l_i = lax.broadcasted_iota(jnp.int32, (nc, 4 * tq), 1)
    cblk = 2 * _mod_pow2(r_i, half) + _div_pow2(r_i, half)
    c_end = cblk * CMP_BLOCK + (CMP_BLOCK - 1)
    t = i * tq + _mod_pow2(l_i, tq)
    dist = t - c_end
    slope = _alibi_slope(g * NSA_GROUP + _div_pow2(l_i, tq))
    s = s - slope * dist.astype(F32)
    mask = dist >= 0
    m = jnp.max(jnp.where(mask, s, NEG_INF), axis=0, keepdims=True)
    p = jnp.where(mask, jnp.exp(s - m), 0.0)
    l = jnp.sum(p, axis=0, keepdims=True)
    p = p * jnp.where(l > 0.0, 1.0 / l, 0.0)

    o = _dot_tn(p.astype(BF16), vc_ref[...])
    gate = _expand_gate(_sigmoid(gate_ref[...]), g, 0)
    y_ref[...] = (_unstack_group(o, tq) * gate).astype(y_ref.dtype)

    imp_c = p[:, 0:tq] + p[:, tq:2 * tq] + p[:, 2 * tq:3 * tq] + p[:, 3 * tq:4 * tq]
    imp = imp_c[:half] + imp_c[half:]
    n_blk = half
    blk = lax.broadcasted_iota(jnp.int32, (n_blk, tq), 0)
    tt = i * tq + lax.broadcasted_iota(jnp.int32, (n_blk, tq), 1)
    cur = _div_pow2(tt, SEL_BLOCK)
    score = jnp.where(blk * SEL_BLOCK <= tt, imp, NEG_INF)
    for forced_blk in (0, cur, cur - 1):
        score = jnp.where(blk == forced_blk, FORCED_SCORE, score)

    taken = -jnp.inf
    sel = jnp.zeros((n_blk, tq), F32)
    for _ in range(n_sel):
        mx = jnp.max(score, axis=0, keepdims=True)
        first = jnp.min(jnp.where(score == mx, blk, n_blk), axis=0, keepdims=True)
        hit = blk == first
        sel = jnp.where(hit, 1.0, sel)
        score = jnp.where(hit, taken, score)
    sel = jnp.concatenate([sel, jnp.zeros((LANES - n_blk, tq), F32)], axis=0)
    sel_ref[...] = sel.T
    picks = _dot_nt(jnp.ones((8, tq), BF16), sel.astype(BF16))
    flag_ref[...] = picks[0:1].astype(jnp.int32)


def _nsa_cmp(q, kc, vc, gate):
    b, s, _ = q.shape
    tq = ATTN_TQ
    nc = kc.shape[1]
    n_blk = nc // 2
    assert n_blk <= LANES
    n_sel = min(N_SEL, n_blk)
    return pl.pallas_call(
        functools.partial(_nsa_cmp_kernel, tq=tq, n_sel=n_sel),
        grid=(b, NSA_KV_GROUPS, s // tq),
        in_specs=[pl.BlockSpec((None, tq, 2 * LANES), lambda bi, g, i: (bi, i, g)),
                  pl.BlockSpec((None, nc, LANES), lambda bi, g, i: (bi, 0, g)),
                  pl.BlockSpec((None, nc, LANES), lambda bi, g, i: (bi, 0, g)),
                  pl.BlockSpec((None, tq, GATE_PAD), lambda bi, g, i: (bi, i, 0))],
        out_specs=[pl.BlockSpec((None, tq, 2 * LANES), lambda bi, g, i: (bi, i, g)),
                   pl.BlockSpec((None, None, tq, LANES), lambda bi, g, i: (bi, g, i, 0)),
                   pl.BlockSpec((None, None, None, 1, LANES), lambda bi, g, i: (bi, g, i, 0, 0))],
        out_shape=[jax.ShapeDtypeStruct((b, s, 4 * LANES), F32),
                   jax.ShapeDtypeStruct((b, NSA_KV_GROUPS, s, LANES), F32),
                   jax.ShapeDtypeStruct((b, NSA_KV_GROUPS, s // tq, 1, LANES), jnp.int32)],
        compiler_params=_cparams(("parallel", "parallel", "parallel")), name="nsa_cmp",
    )(q, kc, vc, gate)


def _key_features(s, tk, wk):
    def table(n, period, with_blocks):
        key = lax.broadcasted_iota(jnp.int32, (n, LANES), 0)
        lane = lax.broadcasted_iota(jnp.int32, (n, LANES), 1)
        r = key % period
        pos = jnp.where(lane == FEAT_HI, r // SEL_BLOCK, jnp.where(lane == FEAT_LO, r % SEL_BLOCK, 0))
        if with_blocks:
            pos = jnp.where(lane < SEL_BLOCK, (key // SEL_BLOCK == lane).astype(jnp.int32), pos)
        return pos.astype(BF16)
    return table(s, tk, True), table(wk, wk, False)


def _nsa_main_kernel(flag_ref, q_ref, ks_ref, kw_ref, vs_ref, vw_ref, kfeat_ref, rfeat_ref, sel_ref,
                     gate_ref, ycmp_ref, o_ref, m_ref, l_ref, acc_ref, *, tq, tk):
    g = pl.program_id(1)
    i = pl.program_id(2)
    cols = NSA_GROUP * tq
    qs = pl.multiple_of(i * tq, tq)
    qst = _stack_group(q_ref[...], tq)
    col = lax.broadcasted_iota(jnp.int32, (1, cols), 1)
    t_loc = _mod_pow2(col, tq)
    slope_row = _alibi_slope(g * NSA_GROUP + _div_pow2(col, tq))

    lane = lax.broadcasted_iota(jnp.int32, (tq, LANES), 1)
    sel = sel_ref[...]
    first_blk = i * (tq // SEL_BLOCK)
    add_diag = jnp.where(sel > 0.5, 0.0, -MASK_BIG)
    add_bulk = jnp.where(lane < first_blk, add_diag, -MASK_BIG)
    f_bulk, f_diag, f_win = [], [], []
    for h in range(NSA_GROUP):
        slope = _alibi_slope(jnp.full((tq, LANES), g * NSA_GROUP + h, jnp.int32))
        pos = jnp.where(lane == FEAT_HI, slope * SEL_BLOCK, jnp.where(lane == FEAT_LO, slope, 0.0))
        f_win.append(pos)
        f_bulk.append(jnp.where(lane < SEL_BLOCK, add_bulk, pos))
        f_diag.append(jnp.where(lane < SEL_BLOCK, add_diag, pos))
    aug = lambda f: jnp.concatenate([qst, jnp.concatenate(f, axis=0).astype(BF16)], axis=1)
    q_sel, q_diag, q_win = aug(f_bulk), aug(f_diag), aug(f_win)

    def online(k, q_aug, mask_add, bias, v_t):
        s = _dot_nt(k, q_aug)
        if mask_add is not None:
            s = s + mask_add
        m = m_ref[...]
        m_new = jnp.maximum(m, jnp.max(s, axis=0, keepdims=True) + bias)
        alpha = jnp.exp(m - m_new)
        p = jnp.exp(s + (bias - m_new))
        m_ref[...] = m_new
        l_ref[...] = alpha * l_ref[...] + jnp.sum(p, axis=0, keepdims=True)
        acc_ref[...] = alpha * acc_ref[...] + _dot(v_t, p.astype(BF16))

    def untranspose(o_t):
        return jnp.concatenate([o_t[:, h * tq:(h + 1) * tq].T for h in range(NSA_GROUP)], axis=0)

    r_i = lax.broadcasted_iota(jnp.int32, (tq, tq), 0)
    c_i = lax.broadcasted_iota(jnp.int32, (tq, tq), 1)
    causal_add = jnp.where(r_i <= c_i, 0.0, -MASK_BIG)
    causal_add = jnp.concatenate([causal_add] * NSA_GROUP, axis=1)
    m_ref[...] = jnp.full((1, cols), NEG_INF, F32)
    l_ref[...] = jnp.zeros((1, cols), F32)
    acc_ref[...] = jnp.zeros((LANES, cols), F32)
    k = jnp.concatenate([ks_ref[pl.ds(qs, tq), :], kfeat_ref[pl.ds(qs, tq), :]], axis=1)
    bias = -slope_row * ((qs & (tk - 1)) + t_loc).astype(F32)
    online(k, q_diag, causal_add, bias, vs_ref[:, pl.ds(qs, tq)])

    def bulk(kt, _):
        start = pl.multiple_of(kt * tk, tk)
        picked = 0
        for sub in range(flag_ref.shape[0]):
            for j in range(tk // SEL_BLOCK):
                picked = picked + flag_ref[sub, 0, kt * (tk // SEL_BLOCK) + j]

        @pl.when(picked > 0)
        def _():
            k = jnp.concatenate([ks_ref[pl.ds(start, tk), :], kfeat_ref[pl.ds(start, tk), :]], axis=1)
            bias = -slope_row * (qs - start + t_loc).astype(F32)
            online(k, q_sel, None, bias, vs_ref[:, pl.ds(start, tk)])

        return 0

    lax.fori_loop(0, (qs + tk - 1) // tk, bulk, 0)
    o_sel = untranspose(acc_ref[...] * (1.0 / l_ref[...]))

    wk = WINDOW + tq
    start = pl.multiple_of(jnp.maximum(qs - WINDOW, 0), tq)
    w_r = lax.broadcasted_iota(jnp.int32, (wk, tq), 0)
    w_c = lax.broadcasted_iota(jnp.int32, (wk, tq), 1)
    dist = (qs - start) + w_c - w_r
    band_add = jnp.where(jnp.where(dist >= 0, dist, WINDOW) < WINDOW, 0.0, -MASK_BIG)
    band_add = jnp.concatenate([band_add] * NSA_GROUP, axis=1)
    k = jnp.concatenate([kw_ref[pl.ds(start, wk), :], rfeat_ref[...]], axis=1)
    s = _dot_nt(k, q_win) + band_add
    p = jnp.exp(s - jnp.max(s, axis=0, keepdims=True))
    l_w = jnp.sum(p, axis=0, keepdims=True)
    o_win = untranspose(_dot(vw_ref[:, pl.ds(start, wk)], p.astype(BF16)) * (1.0 / l_w))

    gate = _sigmoid(gate_ref[...])
    y = (ycmp_ref[...] + _expand_gate(gate, g, 1) * _unstack_group(o_sel, tq)
         + _expand_gate(gate, g, 2) * _unstack_group(o_win, tq))
    o_ref[...] = y.astype(o_ref.dtype)


def _nsa_main(q, k_sw, v_sw_t, sel, flags, gate, ycmp):
    b, s, _ = q.shape
    tq = NSA_TQ
    tk = min(SEL_TK, s)
    wk = WINDOW + tq
    cols = NSA_GROUP * tq
    flag_rows = tq // ATTN_TQ
    assert s >= wk and s % tk == 0 and tk % tq == 0 and s // SEL_BLOCK <= SEL_BLOCK
    kfeat, rfeat = _key_features(s, tk, wk)
    k_spec = lambda off: pl.BlockSpec((None, s, LANES), lambda bi, g, i: (bi, 0, off + g))
    v_spec = lambda off: pl.BlockSpec((LANES, s), lambda bi, g, i: (off + g, bi))
    return pl.pallas_call(
        functools.partial(_nsa_main_kernel, tq=tq, tk=tk),
        grid=(b, NSA_KV_GROUPS, s // tq),
        in_specs=[pl.BlockSpec((None, None, flag_rows, 1, LANES), lambda bi, g, i: (bi, g, i, 0, 0),
                               memory_space=pltpu.SMEM),
                  pl.BlockSpec((None, tq, 2 * LANES), lambda bi, g, i: (bi, i, g)),
                  k_spec(0), k_spec(2), v_spec(0), v_spec(2),
                  pl.BlockSpec((s, LANES), lambda bi, g, i: (0, 0)),
                  pl.BlockSpec((wk, LANES), lambda bi, g, i: (0, 0)),
                  pl.BlockSpec((None, None, tq, LANES), lambda bi, g, i: (bi, g, i, 0)),
                  pl.BlockSpec((None, tq, GATE_PAD), lambda bi, g, i: (bi, i, 0)),
                  pl.BlockSpec((None, tq, 2 * LANES), lambda bi, g, i: (bi, i, g))],
        out_specs=pl.BlockSpec((None, tq, 2 * LANES), lambda bi, g, i: (bi, i, g)),
        out_shape=jax.ShapeDtypeStruct((b, s, 4 * LANES), BF16),
        scratch_shapes=[pltpu.VMEM((1, cols), F32), pltpu.VMEM((1, cols), F32), pltpu.VMEM((LANES, cols), F32)],
        compiler_params=_cparams(("parallel", "parallel", "arbitrary")), name="nsa_main",
    )(flags, q, k_sw, k_sw, v_sw_t, v_sw_t, kfeat, rfeat, sel, gate, ycmp)


def _merge_kernel(ysb_ref, ynsa_ref, g_ref, h_ref, wsb_ref, wnsa_ref, wo_ref, lng_ref, lnb_ref,
                  h1_ref, h1b_ref, *, alpha):
    d = h_ref.shape[1]
    g = g_ref[...]
    merged = (_sigmoid(g[:, :d]) * _dot(ysb_ref[...], wsb_ref[...])
              + _sigmoid(g[:, d:]) * _dot(ynsa_ref[...], wnsa_ref[...]))
    u = alpha * h_ref[...] + _dot(merged.astype(BF16), wo_ref[...])
    h1 = _layer_norm(u, lng_ref[...], lnb_ref[...])
    h1_ref[...] = h1
    h1b_ref[...] = h1.astype(BF16)


def _merge_ln1(ysb, ynsa, g, h, wsb, wnsa, wo, lng, lnb, alpha):
    t, d = h.shape
    tm = MERGE_TM
    row = lambda w: pl.BlockSpec((tm, w), lambda i: (i, 0))
    full = lambda a: pl.BlockSpec(a.shape, lambda i: (0, 0))
    return pl.pallas_call(
        functools.partial(_merge_kernel, alpha=alpha),
        grid=(t // tm,),
        in_specs=[row(ysb.shape[1]), row(ynsa.shape[1]), row(2 * d), row(d),
                  full(wsb), full(wnsa), full(wo), full(lng), full(lnb)],
        out_specs=[row(d), row(d)],
        out_shape=[jax.ShapeDtypeStruct((t, d), F32), jax.ShapeDtypeStruct((t, d), BF16)],
        compiler_params=_cparams(("parallel",)), name="merge_ln1",
    )(ysb, ynsa, g, h, wsb, wnsa, wo, lng, lnb)


def _router_kernel(x_ref, w_ref, b_ref, idx_ref, wt_ref, rank_ref, cnt_ref):
    logits = _dot_nt(w_ref[...], x_ref[...]) + b_ref[...]
    n_e, tm = logits.shape
    e_i = lax.broadcasted_iota(jnp.int32, (n_e, tm), 0)
    vals, idxs = [], []
    for _ in range(TOP_K):
        mx = jnp.max(logits, axis=0, keepdims=True)
        first = jnp.min(jnp.where(logits == mx, e_i, n_e), axis=0, keepdims=True)
        vals.append(mx)
        idxs.append(first)
        logits = jnp.where(e_i == first, -jnp.inf, logits)
    ex = [jnp.exp(v - vals[0]) for v in vals]
    inv = 1.0 / (ex[0] + ex[1] + ex[2] + ex[3])
    idx_ref[...] = jnp.concatenate(idxs, axis=0)
    wt_ref[...] = jnp.concatenate([e * inv for e in ex], axis=0)

    @pl.when(pl.program_id(0) == 0)
    def _():
        cnt_ref[...] = jnp.zeros_like(cnt_ref)

    r_i = lax.broadcasted_iota(jnp.int32, (tm, tm), 0)
    c_i = lax.broadcasted_iota(jnp.int32, (tm, tm), 1)
    before = jnp.where(r_i < c_i, 1.0, 0.0).astype(BF16)
    ones = jnp.ones((tm, LANES), BF16)
    seen = cnt_ref[...][:, 0:1]
    ranks = []
    for first in idxs:
        hot = jnp.where(e_i == first, 1.0, 0.0)
        hot_bf = hot.astype(BF16)
        ranks.append(jnp.sum(hot * (seen + _dot(hot_bf, before)), axis=0, keepdims=True))
        seen = seen + _dot(hot_bf, ones)[:, 0:1]
    rank_ref[...] = jnp.concatenate(ranks, axis=0).astype(jnp.int32)
    cnt_ref[...] = jnp.broadcast_to(seen, cnt_ref.shape)


def _router(h1b, w_t, b_col):
    t, d = h1b.shape
    tm = ROUTER_TM
    n_e = w_t.shape[0]
    return pl.pallas_call(
        _router_kernel,
        grid=(t // tm,),
        in_specs=[pl.BlockSpec((tm, d), lambda i: (i, 0)),
                  pl.BlockSpec(w_t.shape, lambda i: (0, 0)),
                  pl.BlockSpec(b_col.shape, lambda i: (0, 0))],
        out_specs=[pl.BlockSpec((TOP_K, tm), lambda i: (0, i)),
                   pl.BlockSpec((TOP_K, tm), lambda i: (0, i)),
                   pl.BlockSpec((TOP_K, tm), lambda i: (0, i)),
                   pl.BlockSpec((n_e, LANES), lambda i: (0, 0))],
        out_shape=[jax.ShapeDtypeStruct((TOP_K, t), jnp.int32),
                   jax.ShapeDtypeStruct((TOP_K, t), F32),
                   jax.ShapeDtypeStruct((TOP_K, t), jnp.int32),
                   jax.ShapeDtypeStruct((n_e, LANES), F32)],
        compiler_params=_cparams(("arbitrary",)), name="router",
    )(h1b, w_t, b_col)


def _expert_kernel(te_ref, nv_ref, x_ref, wgu_ref, bgu_ref, wd_ref, bd_ref, y_ref, wgu_bf, wd_bf):
    i = pl.program_id(0)

    @pl.when(jnp.logical_or(i == 0, te_ref[i] != te_ref[jnp.maximum(i - 1, 0)]))
    def _():
        wgu_bf[...] = wgu_ref[...].astype(BF16)
        wd_bf[...] = wd_ref[...].astype(BF16)

    @pl.when(i < nv_ref[0])
    def _():
        gu = _dot(x_ref[...].astype(BF16), wgu_bf[...]) + bgu_ref[...]
        gate = jnp.minimum(gu[:, :D_FF], SWIGLU_LIMIT)
        up = jnp.clip(gu[:, D_FF:], -SWIGLU_LIMIT, SWIGLU_LIMIT)
        act = (up + 1.0) * gate * _sigmoid(SWIGLU_ALPHA * gate)
        y_ref[...] = (_dot(act.astype(BF16), wd_bf[...]) + bd_ref[...]).astype(y_ref.dtype)

    @pl.when(i >= nv_ref[0])
    def _():
        y_ref[...] = jnp.zeros_like(y_ref)


def _experts(tile_expert, n_valid, xs, wgu, bgu, wd, bd, layer):
    p, d = xs.shape
    tm = EXPERT_TM
    two_ff = wgu.shape[-1]
    grid_spec = pltpu.PrefetchScalarGridSpec(
        num_scalar_prefetch=2, grid=(p // tm,),
        in_specs=[pl.BlockSpec((tm, d), lambda i, te, nv: (i, 0)),
                  pl.BlockSpec((None, None, d, two_ff), lambda i, te, nv: (layer, te[i], 0, 0)),
                  pl.BlockSpec((None, 1, two_ff), lambda i, te, nv: (te[i], 0, 0)),
                  pl.BlockSpec((None, None, two_ff // 2, d), lambda i, te, nv: (layer, te[i], 0, 0)),
                  pl.BlockSpec((None, 1, d), lambda i, te, nv: (te[i], 0, 0))],
        out_specs=pl.BlockSpec((tm, d), lambda i, te, nv: (i, 0)),
        scratch_shapes=[pltpu.VMEM((d, two_ff), BF16), pltpu.VMEM((two_ff // 2, d), BF16)])
    return pl.pallas_call(
        _expert_kernel, grid_spec=grid_spec,
        out_shape=jax.ShapeDtypeStruct((p, d), F32),
        compiler_params=_cparams(("arbitrary",)), name="experts",
    )(tile_expert, n_valid, xs, wgu, bgu, wd, bd)


def _out_kernel(h1_ref, h1b_ref, yg_ref, cw_ref, p_ref, wpg_ref, wpp_ref, lng_ref, lnb_ref, o_ref, *, alpha):
    cw = cw_ref[...]
    moe = cw[:, 0:1] * yg_ref[0]
    for k in range(1, TOP_K):
        moe = moe + cw[:, k:k + 1] * yg_ref[k]
    ple = _sigmoid(_dot(h1b_ref[...], wpg_ref[...])) * _dot(p_ref[...].astype(BF16), wpp_ref[...])
    u = alpha * h1_ref[...] + moe + ple
    o_ref[...] = _layer_norm(u, lng_ref[...], lnb_ref[...])


def _out_ln2(h1, h1b, yg, cw, p, wpg, wpp, lng, lnb, alpha):
    t, d = h1.shape
    tm = OUT_TM
    row = lambda w: pl.BlockSpec((tm, w), lambda i: (i, 0))
    full = lambda a: pl.BlockSpec(a.shape, lambda i: (0, 0))
    return pl.pallas_call(
        functools.partial(_out_kernel, alpha=alpha),
        grid=(t // tm,),
        in_specs=[row(d), row(d), pl.BlockSpec((TOP_K, tm, d), lambda i: (0, i, 0)), row(TOP_K),
                  row(p.shape[1]), full(wpg), full(wpp), full(lng), full(lnb)],
        out_specs=row(d),
        out_shape=jax.ShapeDtypeStruct((t, d), F32),
        compiler_params=_cparams(("parallel",)), name="out_ln2",
    )(h1, h1b, yg, cw, p, wpg, wpp, lng, lnb)


def _dup_groups(w):
    d = w.shape[0]
    w = w.reshape(d, NSA_KV_GROUPS, 1, HEAD_DIM)
    return jnp.broadcast_to(w, (d, NSA_KV_GROUPS, 2, HEAD_DIM)).reshape(d, NSA_KV_GROUPS * LANES)


def _in_proj_weights(w_in):
    sbw = SB_HEADS * HEAD_DIM
    qw = NSA_HEADS * HEAD_DIM
    kvw = NSA_KV_GROUPS * HEAD_DIM
    d = w_in.shape[0]
    widths = (sbw, sbw, sbw, qw) + (kvw,) * 6 + (3 * NSA_HEADS, d, d)
    parts, start = [], 0
    for wd in widths:
        parts.append(w_in[:, start:start + wd])
        start += wd
    (sb_q, sb_k, sb_v, nsa_q, k_cmp, v_cmp, k_sel, v_sel, k_win, v_win, gate, g_sb, g_nsa) = parts
    k_sw = jnp.concatenate([_dup_groups(k_sel), _dup_groups(k_win)], axis=1)
    v_sw_t = jnp.concatenate([_dup_groups(v_sel), _dup_groups(v_win)], axis=1).T
    gate = jnp.pad(gate, ((0, 0), (0, GATE_PAD - gate.shape[1])))
    g_both = jnp.concatenate([g_sb, g_nsa], axis=1)
    ws = [sb_q, sb_k, sb_v, nsa_q, k_cmp, v_cmp, k_sw, v_sw_t, gate, g_both]
    dtypes = [BF16, BF16, BF16, BF16, F32, F32, BF16, BF16, F32, F32]
    transposed = [False] * 7 + [True, False, False]
    scale = HEAD_DIM ** -0.5
    scales = [scale, 1.0, 1.0, scale] + [1.0] * 6
    return [w.astype(BF16) for w in ws], dtypes, scales, transposed


def _compress_weights(w1, w2, pe):
    eye = jnp.eye(NSA_KV_GROUPS, dtype=w1.dtype)
    w1b = jnp.einsum('ldh,pg->lpdgh', w1, eye).reshape(CMP_BLOCK * NSA_KV_GROUPS * HEAD_DIM,
                                                       NSA_KV_GROUPS * CMP_HIDDEN)
    w2b = jnp.einsum('hd,pg,r->phgrd', w2, eye, jnp.ones((2,), w2.dtype)).reshape(
        NSA_KV_GROUPS * CMP_HIDDEN, NSA_KV_GROUPS * LANES)
    peb = jnp.broadcast_to(pe[:, None, :], (CMP_BLOCK, NSA_KV_GROUPS, HEAD_DIM)).reshape(1, -1)
    return peb.astype(F32), w1b.astype(BF16), w2b.astype(BF16)


def _moe_plan(idx_t, rank_t, counts, tm):
    k, t = idx_t.shape
    n = k * t
    i32 = jnp.int32
    experts = jnp.arange(N_EXPERTS, dtype=i32)
    padded = ((counts + tm - 1) // tm) * tm
    pad_end = jnp.cumsum(padded)
    pad_off = pad_end - padded
    gap = padded - counts
    pos = rank_t + jnp.sum(jnp.where(idx_t[None] == experts[:, None, None], pad_off[:, None, None], 0), axis=0)
    tok = jnp.arange(t, dtype=i32)
    ids = ((tok // ROUTER_TM) * (k * ROUTER_TM) + tok % ROUTER_TM)[None, :] + jnp.arange(k, dtype=i32)[:, None] * ROUTER_TM
    n_pad = N_EXPERTS * tm
    p_rows = n + n_pad
    id_bits = (p_rows - 1).bit_length()
    assert (N_EXPERTS + 1) << id_bits < 2 ** 31
    spare_e = jnp.repeat(experts, tm, total_repeat_length=n_pad)
    spare_used = jnp.tile(jnp.arange(tm, dtype=i32), N_EXPERTS) < jnp.repeat(gap, tm, total_repeat_length=n_pad)
    keys = jnp.concatenate([idx_t.reshape(n), jnp.where(spare_used, spare_e, N_EXPERTS)])
    ids = jnp.concatenate([ids.reshape(n), n + jnp.arange(n_pad, dtype=i32)])
    entry = lax.sort(keys * (1 << id_bits) + ids) & ((1 << id_bits) - 1)
    tok_of_entry = (entry // (k * ROUTER_TM)) * ROUTER_TM + entry % ROUTER_TM
    src_tok = jnp.where(entry < n, tok_of_entry, 0)
    pos = pos.reshape(n)
    tile_start = jnp.arange(p_rows // tm, dtype=i32) * tm
    tile_expert = jnp.minimum(jnp.sum((tile_start[:, None] >= pad_end[None, :]).astype(i32), axis=1), N_EXPERTS - 1)
    n_valid = (pad_end[-1] // tm).astype(i32).reshape(1)
    return src_tok, pos, tile_expert, n_valid


def _layer(h, p_i, w_in, w_cmp1, w_cmp2, pe_cmp, w_br_sb, w_br_nsa, w_o, ln1_g, ln1_b, ln2_g, ln2_b,
           w_router, b_router, w_gu_all, b_gu, w_down_all, b_down, w_ple_gate, w_ple_proj,
           *, batch, alpha, layer):
    t, d = h.shape
    s = t // batch
    sb_q, sb_k, sb_v, nsa_q, k_cmp, v_cmp, k_sw, v_sw_t, gate, g_both = _in_proj(h, *_in_proj_weights(w_in))
    b3 = lambda a: a.reshape(batch, s, a.shape[1])

    y_sb = _sb_attention(b3(sb_q), b3(sb_k), b3(sb_v)).reshape(t, -1)

    nc = s // CMP_BLOCK
    cmp_rows = lambda a: a.reshape(batch, nc // 2, 2 * CMP_BLOCK * a.shape[1])
    kc = _compress(cmp_rows(k_cmp), *_compress_weights(w_cmp1[0], w_cmp2[0], pe_cmp[0]))
    vc = _compress(cmp_rows(v_cmp), *_compress_weights(w_cmp1[1], w_cmp2[1], pe_cmp[1]))
    y_cmp, sel, flags = _nsa_cmp(b3(nsa_q), kc, vc, b3(gate))
    y_nsa = _nsa_main(b3(nsa_q), b3(k_sw), v_sw_t, sel, flags, b3(gate), y_cmp).reshape(t, -1)

    row = lambda v: v.reshape(1, -1).astype(F32)
    h1, h1b = _merge_ln1(y_sb, y_nsa, g_both, h, w_br_sb.astype(BF16), w_br_nsa.astype(BF16),
                         w_o.astype(BF16), row(ln1_g), row(ln1_b), alpha)

    idx_t, wt_t, rank_t, counts = _router(h1b, w_router.T.astype(BF16), b_router.reshape(-1, 1).astype(F32))
    src_tok, pos, tile_expert, n_valid = _moe_plan(idx_t, rank_t, counts[:, 0].astype(jnp.int32), EXPERT_TM)
    xs = h1.at[src_tok].get(mode='promise_in_bounds')
    y = _experts(tile_expert, n_valid, xs, w_gu_all, b_gu.reshape(N_EXPERTS, 1, -1),
                 w_down_all, b_down.reshape(N_EXPERTS, 1, -1), layer)
    yg = y.at[pos].get(mode='promise_in_bounds').reshape(TOP_K, t, d)
    return _out_ln2(h1, h1b, yg, wt_t.T, p_i, w_ple_gate.astype(BF16), w_ple_proj.astype(BF16),
                    row(ln2_g), row(ln2_b), alpha)


def kernel(x, p, w_in, w_cmp1, w_cmp2, pe_cmp, w_br_sb, w_br_nsa, w_o, ln1_g, ln1_b, ln2_g, ln2_b,
           w_router, b_router, w_gu, b_gu, w_down, b_down, w_ple_gate, w_ple_proj):
    batch, s, d = x.shape
    depth = w_in.shape[0]
    alpha = (2 * depth) ** 0.25
    h = x.reshape(batch * s, d)
    for i in range(depth):
        h = _layer(h, p[i].reshape(batch * s, -1), w_in[i], w_cmp1[i], w_cmp2[i], pe_cmp[i],
                   w_br_sb[i], w_br_nsa[i], w_o[i], ln1_g[i], ln1_b[i], ln2_g[i], ln2_b[i],
                   w_router[i], b_router[i], w_gu, b_gu[i], w_down, b_down[i],
                   w_ple_gate[i], w_ple_proj[i], batch=batch, alpha=alpha, layer=i)
    return h.reshape(batch, s, d)
```

```python
import functools

import jax
import jax.numpy as jnp
from jax import lax
from jax.experimental import pallas as pl
from jax.experimental.pallas import tpu as pltpu

F32 = jnp.float32
BF16 = jnp.bfloat16

HEAD_DIM = 64
LANES = 128
SB_HEADS = 8
NSA_HEADS = 8
NSA_KV_GROUPS = 2
NSA_GROUP = NSA_HEADS // NSA_KV_GROUPS
CMP_BLOCK = 32
CMP_HIDDEN = 128
SEL_BLOCK = 64
N_SEL = 16
WINDOW = 512
N_EXPERTS = 32
TOP_K = 4
D_FF = 1024
SWIGLU_ALPHA = 1.702
SWIGLU_LIMIT = 7.0
LN_EPS = 1e-5
NEG_INF = -1e30
FORCED_SCORE = 1e6
GATE_PAD = 128
MASK_BIG = 2.0 ** 100
FEAT_HI, FEAT_LO = 64, 65
SB_DEAD_LOG = -104.0
SB_FIRST_TILES = 3

VMEM_LIMIT = 56 * 1024 * 1024

ATTN_TQ = 128
NSA_TQ = 256
SEL_TK = 512
PROJ_TM = 256
MERGE_TM = 256
ROUTER_TM = 512
EXPERT_TM = 512
OUT_TM = 256


def _cparams(sem):
    return pltpu.CompilerParams(dimension_semantics=sem, vmem_limit_bytes=VMEM_LIMIT)


def _sigmoid(x):
    return 1.0 / (1.0 + jnp.exp(-x))


def _div_pow2(x, n):
    assert n & (n - 1) == 0
    return lax.shift_right_logical(x, n.bit_length() - 1)


def _mod_pow2(x, n):
    assert n & (n - 1) == 0
    return x & (n - 1)


def _split_bf16(x):
    hi = x.astype(BF16)
    lo = (x - hi.astype(F32)).astype(BF16)
    return hi, lo


def _dot(a, b):
    return jnp.dot(a, b, preferred_element_type=F32)


def _dot_nt(a, b):
    return lax.dot_general(a, b, (((1,), (1,)), ((), ())), preferred_element_type=F32)


def _dot_tn(a, b):
    return lax.dot_general(a, b, (((0,), (0,)), ((), ())), preferred_element_type=F32)


def _layer_norm(u, g, b):
    mu = jnp.mean(u, axis=-1, keepdims=True)
    d = u - mu
    var = jnp.mean(d * d, axis=-1, keepdims=True)
    return d * lax.rsqrt(var + LN_EPS) * g + b


def _in_proj_kernel(x_ref, *refs, scales, transposed):
    n = len(scales)
    x = x_ref[...].astype(BF16)
    for w_ref, o_ref, s, tr in zip(refs[:n], refs[n:], scales, transposed):
        acc = _dot_nt(w_ref[...], x) if tr else _dot(x, w_ref[...])
        if s != 1.0:
            acc = acc * s
        o_ref[...] = acc.astype(o_ref.dtype)


def _in_proj(h, weights, dtypes, scales, transposed):
    t, d = h.shape
    tm = PROJ_TM
    in_specs = [pl.BlockSpec((tm, d), lambda i: (i, 0))]
    in_specs += [pl.BlockSpec(w.shape, lambda i: (0, 0)) for w in weights]
    out_specs, out_shape = [], []
    for w, dt, tr in zip(weights, dtypes, transposed):
        if tr:
            out_specs.append(pl.BlockSpec((w.shape[0], tm), lambda i: (0, i)))
            out_shape.append(jax.ShapeDtypeStruct((w.shape[0], t), dt))
        else:
            out_specs.append(pl.BlockSpec((tm, w.shape[1]), lambda i: (i, 0)))
            out_shape.append(jax.ShapeDtypeStruct((t, w.shape[1]), dt))
    return pl.pallas_call(
        functools.partial(_in_proj_kernel, scales=tuple(scales), transposed=tuple(transposed)),
        grid=(t // tm,), in_specs=in_specs, out_specs=out_specs, out_shape=out_shape,
        compiler_params=_cparams(("parallel",)), name="in_proj",
    )(h, *weights)


def _stack_heads(q2, tq):
    lane = lax.broadcasted_iota(jnp.int32, (tq, LANES), 1)
    zero = jnp.zeros_like(q2)
    return jnp.concatenate([jnp.where(lane < HEAD_DIM, q2, zero),
                            jnp.where(lane >= HEAD_DIM, q2, zero)], axis=0)


def _unstack_heads(o, tq):
    lane = lax.broadcasted_iota(jnp.int32, (tq, LANES), 1)
    return jnp.where(lane < HEAD_DIM, o[:tq], o[tq:])


def _sb_kernel(q_ref, k_ref, v_ref, o_ref, *, tq, n_pairs):
    i = pl.program_id(1)
    r_i = lax.broadcasted_iota(jnp.int32, (tq, tq), 0)
    c_i = lax.broadcasted_iota(jnp.int32, (tq, tq), 1)
    tri = jnp.where(r_i > c_i, 1.0, 0.0).astype(BF16)
    rhs = jnp.concatenate([tri, jnp.ones((tq, tq), BF16)], axis=1)
    rhs = jnp.concatenate([rhs, rhs], axis=0)
    qs = [_stack_heads(q_ref[:, p * LANES:(p + 1) * LANES], tq) for p in range(n_pairs)]

    def log_stay(z):
        return -(jnp.maximum(z, 0.0) + jnp.log(1.0 + jnp.exp(-jnp.abs(z))))

    def tile_sums(ls):
        hi, lo = _split_bf16(ls)
        return _dot(jnp.concatenate([hi, lo], axis=1), rhs)

    wide = SB_FIRST_TILES * tq
    first = jnp.maximum(i - (SB_FIRST_TILES - 1), 0)
    start0 = pl.multiple_of(first * tq, tq)
    col_minus_row = (lax.broadcasted_iota(jnp.int32, (2 * tq, wide), 1)
                     - (lax.broadcasted_iota(jnp.int32, (2 * tq, wide), 0) & (tq - 1)))
    causal = col_minus_row < (i - first) * tq

    def first_step(p):
        kw = k_ref[pl.ds(start0, wide), p * LANES:(p + 1) * LANES]
        vw = v_ref[pl.ds(start0, wide), p * LANES:(p + 1) * LANES]
        z = _dot_nt(qs[p], kw)
        ls = jnp.where(causal, log_stay(z), 0.0)
        sums = [tile_sums(ls[:, j * tq:(j + 1) * tq]) for j in range(SB_FIRST_TILES)]
        later, run = [None] * SB_FIRST_TILES, None
        for j in reversed(range(SB_FIRST_TILES)):
            later[j] = sums[j][:, :tq] if run is None else sums[j][:, :tq] + run
            run = sums[j][:, tq:] if run is None else run + sums[j][:, tq:]
        w = jnp.where(causal, jnp.exp(z + ls + jnp.concatenate(later, axis=1)), 0.0)
        return run, _dot(w.astype(BF16), vw)

    def tile(p, j, carry, acc):
        start = pl.multiple_of(j * tq, tq)
        kj = k_ref[pl.ds(start, tq), p * LANES:(p + 1) * LANES]
        vj = v_ref[pl.ds(start, tq), p * LANES:(p + 1) * LANES]
        z = _dot_nt(qs[p], kj)
        ls = log_stay(z)
        sums = tile_sums(ls)
        w = jnp.exp(z + ls + sums[:, :tq] + carry)
        return carry + sums[:, tq:], acc + _dot(w.astype(BF16), vj)

    def live(carries):
        worst = functools.reduce(jnp.maximum, carries)
        return jnp.max(worst) > SB_DEAD_LOG

    state = [first_step(p) for p in range(n_pairs)]
    carries = tuple(s[0] for s in state)
    accs = tuple(s[1] for s in state)

    def cond(c):
        return jnp.logical_and(c[0] <= first, c[1])

    def body(c):
        step, _, carries, accs = c
        state = [tile(p, first - step, carries[p], accs[p]) for p in range(n_pairs)]
        carries = tuple(s[0] for s in state)
        return step + 1, live(carries), carries, tuple(s[1] for s in state)

    _, _, _, accs = lax.while_loop(cond, body, (jnp.int32(1), live(carries), carries, accs))
    for p in range(n_pairs):
        o_ref[:, p * LANES:(p + 1) * LANES] = _unstack_heads(accs[p], tq).astype(o_ref.dtype)


def _sb_attention(q, k, v):
    b, s, w = q.shape
    tq = ATTN_TQ
    return pl.pallas_call(
        functools.partial(_sb_kernel, tq=tq, n_pairs=w // LANES),
        grid=(b, s // tq),
        in_specs=[pl.BlockSpec((None, tq, w), lambda bi, i: (bi, i, 0)),
                  pl.BlockSpec((None, s, w), lambda bi, i: (bi, 0, 0)),
                  pl.BlockSpec((None, s, w), lambda bi, i: (bi, 0, 0))],
        out_specs=pl.BlockSpec((None, tq, w), lambda bi, i: (bi, i, 0)),
        out_shape=jax.ShapeDtypeStruct((b, s, w), BF16),
        compiler_params=_cparams(("parallel", "arbitrary")), name="sb_attn",
    )(q, k, v)


def _compress_kernel(x_ref, pe_ref, w1_ref, w2_ref, o_ref):
    x = (x_ref[...] + pe_ref[...]).astype(BF16)
    hid = _dot(x, w1_ref[...])
    hid = hid * _sigmoid(hid)
    o_ref[...] = _dot(hid.astype(BF16), w2_ref[...]).astype(o_ref.dtype)


def _compress(x, pe, w1, w2):
    b, half, width = x.shape
    blk = width // 2
    out = pl.pallas_call(
        _compress_kernel,
        grid=(b, 2),
        in_specs=[pl.BlockSpec((None, half, blk), lambda bi, par: (bi, 0, par)),
                  pl.BlockSpec(pe.shape, lambda bi, par: (0, 0)),
                  pl.BlockSpec(w1.shape, lambda bi, par: (0, 0)),
                  pl.BlockSpec(w2.shape, lambda bi, par: (0, 0))],
        out_specs=pl.BlockSpec((None, None, half, w2.shape[1]), lambda bi, par: (bi, par, 0, 0)),
        out_shape=jax.ShapeDtypeStruct((b, 2, half, w2.shape[1]), BF16),
        compiler_params=_cparams(("parallel", "parallel")), name="compress",
    )(x, pe, w1, w2)
    return out.reshape(b, 2 * half, w2.shape[1])


def _stack_group(q, tq):
    return jnp.concatenate([_stack_heads(q[:, :LANES], tq), _stack_heads(q[:, LANES:], tq)], axis=0)


def _unstack_group(o, tq):
    return jnp.concatenate([_unstack_heads(o[:2 * tq], tq), _unstack_heads(o[2 * tq:], tq)], axis=1)


def _alibi_slope(head_idx):
    return lax.bitcast_convert_type(lax.shift_left(126 - head_idx, 23), F32)


def _expand_gate(gate_sig, g, branch):
    c_i = lax.broadcasted_iota(jnp.int32, (GATE_PAD, 2 * LANES), 0)
    l_i = lax.broadcasted_iota(jnp.int32, (GATE_PAD, 2 * LANES), 1)
    col = g * (3 * NSA_GROUP) + _div_pow2(l_i, HEAD_DIM) * 3 + branch
    onehot = jnp.where(c_i == col, 1.0, 0.0).astype(BF16)
    hi, lo = _split_bf16(gate_sig)
    return _dot(hi, onehot) + _dot(lo, onehot)


def _nsa_cmp_kernel(q_ref, kc_ref, vc_ref, gate_ref, y_ref, sel_ref, flag_ref, *, tq, n_sel):
    g = pl.program_id(1)
    i = pl.program_id(2)
    nc = kc_ref.shape[0]
    half = nc // 2
    qst = _stack_group(q_ref[...], tq)
    s = _dot_nt(kc_ref[...], qst)
    r_i = lax.broadcasted_iota(jnp.int32, (nc, 4 * tq), 0)
    l_i = lax.broadcasted_iota(jnp.int32, (nc, 4 * tq), 1)
    cblk = 2 * _mod_pow2(r_i, half) + _div_pow2(r_i, half)
    c_end = cblk * CMP_BLOCK + (CMP_BLOCK - 1)
    t = i * tq + _mod_pow2(l_i, tq)
    dist = t - c_end
    slope = _alibi_slope(g * NSA_GROUP + _div_pow2(l_i, tq))
    s = s - slope * dist.astype(F32)
    mask = dist >= 0
    m = jnp.max(jnp.where(mask, s, NEG_INF), axis=0, keepdims=True)
    p = jnp.where(mask, jnp.exp(s - m), 0.0)
    l = jnp.sum(p, axis=0, keepdims=True)
    p = p * jnp.where(l > 0.0, 1.0 / l, 0.0)

    o = _dot_tn(p.astype(BF16), vc_ref[...])
    gate = _expand_gate(_sigmoid(gate_ref[...]), g, 0)
    y_ref[...] = (_unstack_group(o, tq) * gate).astype(y_ref.dtype)

    imp_c = p[:, 0:tq] + p[:, tq:2 * tq] + p[:, 2 * tq:3 * tq] + p[:, 3 * tq:4 * tq]
    imp = imp_c[:half] + imp_c[half:]
    n_blk = half
    blk = lax.broadcasted_iota(jnp.int32, (n_blk, tq), 0)
    tt = i * tq + lax.broadcasted_iota(jnp.int32, (n_blk, tq), 1)
    cur = _div_pow2(tt, SEL_BLOCK)
    score = jnp.where(blk * SEL_BLOCK <= tt, imp, NEG_INF)
    for forced_blk in (0, cur, cur - 1):
        score = jnp.where(blk == forced_blk, FORCED_SCORE, score)

    taken = -jnp.inf
    sel = jnp.zeros((n_blk, tq), F32)
    for _ in range(n_sel):
        mx = jnp.max(score, axis=0, keepdims=True)
        first = jnp.min(jnp.where(score == mx, blk, n_blk), axis=0, keepdims=True)
        hit = blk == first
        sel = jnp.where(hit, 1.0, sel)
        score = jnp.where(hit, taken, score)
    sel = jnp.concatenate([sel, jnp.zeros((LANES - n_blk, tq), F32)], axis=0)
    sel_ref[...] = sel.T
    picks = _dot_nt(jnp.ones((8, tq), BF16), sel.astype(BF16))
    flag_ref[...] = picks[0:1].astype(jnp.int32)


def _nsa_cmp(q, kc, vc, gate):
    b, s, _ = q.shape
    tq = NSA_TQ
    nc = kc.shape[1]
    n_blk = nc // 2
    assert n_blk <= LANES
    n_sel = min(N_SEL, n_blk)
    return pl.pallas_call(
        functools.partial(_nsa_cmp_kernel, tq=tq, n_sel=n_sel),
        grid=(b, NSA_KV_GROUPS, s // tq),
        in_specs=[pl.BlockSpec((None, tq, 2 * LANES), lambda bi, g, i: (bi, i, g)),
                  pl.BlockSpec((None, nc, LANES), lambda bi, g, i: (bi, 0, g)),
                  pl.BlockSpec((None, nc, LANES), lambda bi, g, i: (bi, 0, g)),
                  pl.BlockSpec((None, tq, GATE_PAD), lambda bi, g, i: (bi, i, 0))],
        out_specs=[pl.BlockSpec((None, tq, 2 * LANES), lambda bi, g, i: (bi, i, g)),
                   pl.BlockSpec((None, None, tq, LANES), lambda bi, g, i: (bi, g, i, 0)),
                   pl.BlockSpec((None, None, None, 1, LANES), lambda bi, g, i: (bi, g, i, 0, 0))],
        out_shape=[jax.ShapeDtypeStruct((b, s, 4 * LANES), F32),
                   jax.ShapeDtypeStruct((b, NSA_KV_GROUPS, s, LANES), F32),
                   jax.ShapeDtypeStruct((b, NSA_KV_GROUPS, s // tq, 1, LANES), jnp.int32)],
        compiler_params=_cparams(("parallel", "parallel", "parallel")), name="nsa_cmp",
    )(q, kc, vc, gate)


def _key_features(s, tk, wk):
    def table(n, period, with_blocks):
        key = lax.broadcasted_iota(jnp.int32, (n, LANES), 0)
        lane = lax.broadcasted_iota(jnp.int32, (n, LANES), 1)
        r = key % period
        pos = jnp.where(lane == FEAT_HI, r // SEL_BLOCK, jnp.where(lane == FEAT_LO, r % SEL_BLOCK, 0))
        if with_blocks:
            pos = jnp.where(lane < SEL_BLOCK, (key // SEL_BLOCK == lane).astype(jnp.int32), pos)
        return pos.astype(BF16)
    return table(s, tk, True), table(wk, wk, False)


def _nsa_main_kernel(flag_ref, q_ref, ks_ref, kw_ref, vs_ref, vw_ref, kfeat_ref, rfeat_ref, sel_ref,
                     gate_ref, ycmp_ref, o_ref, m_ref, l_ref, acc_ref, *, tq, tk):
    g = pl.program_id(1)
    i = pl.program_id(2)
    cols = NSA_GROUP * tq
    qs = pl.multiple_of(i * tq, tq)
    qst = _stack_group(q_ref[...], tq)
    col = lax.broadcasted_iota(jnp.int32, (1, cols), 1)
    t_loc = _mod_pow2(col, tq)
    slope_row = _alibi_slope(g * NSA_GROUP + _div_pow2(col, tq))

    lane = lax.broadcasted_iota(jnp.int32, (tq, LANES), 1)
    sel = sel_ref[...]
    first_blk = i * (tq // SEL_BLOCK)
    add_diag = jnp.where(sel > 0.5, 0.0, -MASK_BIG)
    add_bulk = jnp.where(lane < first_blk, add_diag, -MASK_BIG)
    f_bulk, f_diag, f_win = [], [], []
    for h in range(NSA_GROUP):
        slope = _alibi_slope(jnp.full((tq, LANES), g * NSA_GROUP + h, jnp.int32))
        pos = jnp.where(lane == FEAT_HI, slope * SEL_BLOCK, jnp.where(lane == FEAT_LO, slope, 0.0))
        f_win.append(pos)
        f_bulk.append(jnp.where(lane < SEL_BLOCK, add_bulk, pos))
        f_diag.append(jnp.where(lane < SEL_BLOCK, add_diag, pos))
    aug = lambda f: jnp.concatenate([qst, jnp.concatenate(f, axis=0).astype(BF16)], axis=1)
    q_sel, q_diag, q_win = aug(f_bulk), aug(f_diag), aug(f_win)

    def online(k, q_aug, mask_add, bias, v_t):
        s = _dot_nt(k, q_aug)
        if mask_add is not None:
            s = s + mask_add
        m = m_ref[...]
        m_new = jnp.maximum(m, jnp.max(s, axis=0, keepdims=True) + bias)
        alpha = jnp.exp(m - m_new)
        p = jnp.exp(s + (bias - m_new))
        m_ref[...] = m_new
        l_ref[...] = alpha * l_ref[...] + jnp.sum(p, axis=0, keepdims=True)
        acc_ref[...] = alpha * acc_ref[...] + _dot(v_t, p.astype(BF16))

    def untranspose(o_t):
        return jnp.concatenate([o_t[:, h * tq:(h + 1) * tq].T for h in range(NSA_GROUP)], axis=0)

    r_i = lax.broadcasted_iota(jnp.int32, (tq, tq), 0)
    c_i = lax.broadcasted_iota(jnp.int32, (tq, tq), 1)
    causal_add = jnp.where(r_i <= c_i, 0.0, -MASK_BIG)
    causal_add = jnp.concatenate([causal_add] * NSA_GROUP, axis=1)
    m_ref[...] = jnp.full((1, cols), NEG_INF, F32)
    l_ref[...] = jnp.zeros((1, cols), F32)
    acc_ref[...] = jnp.zeros((LANES, cols), F32)
    k = jnp.concatenate([ks_ref[pl.ds(qs, tq), :], kfeat_ref[pl.ds(qs, tq), :]], axis=1)
    bias = -slope_row * ((qs & (tk - 1)) + t_loc).astype(F32)
    online(k, q_diag, causal_add, bias, vs_ref[:, pl.ds(qs, tq)])

    def bulk(kt, _):
        start = pl.multiple_of(kt * tk, tk)
        picked = 0
        for sub in range(flag_ref.shape[0]):
            for j in range(tk // SEL_BLOCK):
                picked = picked + flag_ref[sub, 0, kt * (tk // SEL_BLOCK) + j]

        @pl.when(picked > 0)
        def _():
            k = jnp.concatenate([ks_ref[pl.ds(start, tk), :], kfeat_ref[pl.ds(start, tk), :]], axis=1)
            bias = -slope_row * (qs - start + t_loc).astype(F32)
            online(k, q_sel, None, bias, vs_ref[:, pl.ds(start, tk)])

        return 0

    lax.fori_loop(0, (qs + tk - 1) // tk, bulk, 0)
    o_sel = untranspose(acc_ref[...] * (1.0 / l_ref[...]))

    wk = WINDOW + tq
    start = pl.multiple_of(jnp.maximum(qs - WINDOW, 0), tq)
    w_r = lax.broadcasted_iota(jnp.int32, (wk, tq), 0)
    w_c = lax.broadcasted_iota(jnp.int32, (wk, tq), 1)
    dist = (qs - start) + w_c - w_r
    band_add = jnp.where(jnp.where(dist >= 0, dist, WINDOW) < WINDOW, 0.0, -MASK_BIG)
    band_add = jnp.concatenate([band_add] * NSA_GROUP, axis=1)
    k = jnp.concatenate([kw_ref[pl.ds(start, wk), :], rfeat_ref[...]], axis=1)
    s = _dot_nt(k, q_win) + band_add
    p = jnp.exp(s - jnp.max(s, axis=0, keepdims=True))
    l_w = jnp.sum(p, axis=0, keepdims=True)
    o_win = untranspose(_dot(vw_ref[:, pl.ds(start, wk)], p.astype(BF16)) * (1.0 / l_w))

    gate = _sigmoid(gate_ref[...])
    y = (ycmp_ref[...] + _expand_gate(gate, g, 1) * _unstack_group(o_sel, tq)
         + _expand_gate(gate, g, 2) * _unstack_group(o_win, tq))
    o_ref[...] = y.astype(o_ref.dtype)


def _nsa_main(q, k_sw, v_sw_t, sel, flags, gate, ycmp):
    b, s, _ = q.shape
    tq = NSA_TQ
    tk = min(SEL_TK, s)
    wk = WINDOW + tq
    cols = NSA_GROUP * tq
    flag_rows = 1
    assert s >= wk and s % tk == 0 and tk % tq == 0 and s // SEL_BLOCK <= SEL_BLOCK
    kfeat, rfeat = _key_features(s, tk, wk)
    k_spec = lambda off: pl.BlockSpec((None, s, LANES), lambda bi, g, i: (bi, 0, off + g))
    v_spec = lambda off: pl.BlockSpec((LANES, s), lambda bi, g, i: (off + g, bi))
    return pl.pallas_call(
        functools.partial(_nsa_main_kernel, tq=tq, tk=tk),
        grid=(b, NSA_KV_GROUPS, s // tq),
        in_specs=[pl.BlockSpec((None, None, flag_rows, 1, LANES), lambda bi, g, i: (bi, g, i, 0, 0),
                               memory_space=pltpu.SMEM),
                  pl.BlockSpec((None, tq, 2 * LANES), lambda bi, g, i: (bi, i, g)),
                  k_spec(0), k_spec(2), v_spec(0), v_spec(2),
                  pl.BlockSpec((s, LANES), lambda bi, g, i: (0, 0)),
                  pl.BlockSpec((wk, LANES), lambda bi, g, i: (0, 0)),
                  pl.BlockSpec((None, None, tq, LANES), lambda bi, g, i: (bi, g, i, 0)),
                  pl.BlockSpec((None, tq, GATE_PAD), lambda bi, g, i: (bi, i, 0)),
                  pl.BlockSpec((None, tq, 2 * LANES), lambda bi, g, i: (bi, i, g))],
        out_specs=pl.BlockSpec((None, tq, 2 * LANES), lambda bi, g, i: (bi, i, g)),
        out_shape=jax.ShapeDtypeStruct((b, s, 4 * LANES), BF16),
        scratch_shapes=[pltpu.VMEM((1, cols), F32), pltpu.VMEM((1, cols), F32), pltpu.VMEM((LANES, cols), F32)],
        compiler_params=_cparams(("parallel", "parallel", "arbitrary")), name="nsa_main",
    )(flags, q, k_sw, k_sw, v_sw_t, v_sw_t, kfeat, rfeat, sel, gate, ycmp)


def _merge_kernel(ysb_ref, ynsa_ref, g_ref, h_ref, wsb_ref, wnsa_ref, wo_ref, lng_ref, lnb_ref,
                  h1_ref, h1b_ref, *, alpha):
    d = h_ref.shape[1]
    g = g_ref[...]
    merged = (_sigmoid(g[:, :d]) * _dot(ysb_ref[...], wsb_ref[...])
              + _sigmoid(g[:, d:]) * _dot(ynsa_ref[...], wnsa_ref[...]))
    u = alpha * h_ref[...] + _dot(merged.astype(BF16), wo_ref[...])
    h1 = _layer_norm(u, lng_ref[...], lnb_ref[...])
    h1_ref[...] = h1
    h1b_ref[...] = h1.astype(BF16)


def _merge_ln1(ysb, ynsa, g, h, wsb, wnsa, wo, lng, lnb, alpha):
    t, d = h.shape
    tm = MERGE_TM
    row = lambda w: pl.BlockSpec((tm, w), lambda i: (i, 0))
    full = lambda a: pl.BlockSpec(a.shape, lambda i: (0, 0))
    return pl.pallas_call(
        functools.partial(_merge_kernel, alpha=alpha),
        grid=(t // tm,),
        in_specs=[row(ysb.shape[1]), row(ynsa.shape[1]), row(2 * d), row(d),
                  full(wsb), full(wnsa), full(wo), full(lng), full(lnb)],
        out_specs=[row(d), row(d)],
        out_shape=[jax.ShapeDtypeStruct((t, d), F32), jax.ShapeDtypeStruct((t, d), BF16)],
        compiler_params=_cparams(("parallel",)), name="merge_ln1",
    )(ysb, ynsa, g, h, wsb, wnsa, wo, lng, lnb)


def _router_kernel(x_ref, w_ref, b_ref, idx_ref, wt_ref, rank_ref, cnt_ref):
    logits = _dot_nt(w_ref[...], x_ref[...]) + b_ref[...]
    n_e, tm = logits.shape
    e_i = lax.broadcasted_iota(jnp.int32, (n_e, tm), 0)
    vals, idxs = [], []
    for _ in range(TOP_K):
        mx = jnp.max(logits, axis=0, keepdims=True)
        first = jnp.min(jnp.where(logits == mx, e_i, n_e), axis=0, keepdims=True)
        vals.append(mx)
        idxs.append(first)
        logits = jnp.where(e_i == first, -jnp.inf, logits)
    ex = [jnp.exp(v - vals[0]) for v in vals]
    inv = 1.0 / (ex[0] + ex[1] + ex[2] + ex[3])
    idx_ref[...] = jnp.concatenate(idxs, axis=0)
    wt_ref[...] = jnp.concatenate([e * inv for e in ex], axis=0)

    @pl.when(pl.program_id(0) == 0)
    def _():
        cnt_ref[...] = jnp.zeros_like(cnt_ref)

    r_i = lax.broadcasted_iota(jnp.int32, (tm, tm), 0)
    c_i = lax.broadcasted_iota(jnp.int32, (tm, tm), 1)
    before = jnp.where(r_i < c_i, 1.0, 0.0).astype(BF16)
    ones = jnp.ones((tm, LANES), BF16)
    seen = cnt_ref[...][:, 0:1]
    ranks = []
    for first in idxs:
        hot = jnp.where(e_i == first, 1.0, 0.0)
        hot_bf = hot.astype(BF16)
        ranks.append(jnp.sum(hot * (seen + _dot(hot_bf, before)), axis=0, keepdims=True))
        seen = seen + _dot(hot_bf, ones)[:, 0:1]
    rank_ref[...] = jnp.concatenate(ranks, axis=0).astype(jnp.int32)
    cnt_ref[...] = jnp.broadcast_to(seen, cnt_ref.shape)


def _router(h1b, w_t, b_col):
    t, d = h1b.shape
    tm = ROUTER_TM
    n_e = w_t.shape[0]
    return pl.pallas_call(
        _router_kernel,
        grid=(t // tm,),
        in_specs=[pl.BlockSpec((tm, d), lambda i: (i, 0)),
                  pl.BlockSpec(w_t.shape, lambda i: (0, 0)),
                  pl.BlockSpec(b_col.shape, lambda i: (0, 0))],
        out_specs=[pl.BlockSpec((TOP_K, tm), lambda i: (0, i)),
                   pl.BlockSpec((TOP_K, tm), lambda i: (0, i)),
                   pl.BlockSpec((TOP_K, tm), lambda i: (0, i)),
                   pl.BlockSpec((n_e, LANES), lambda i: (0, 0))],
        out_shape=[jax.ShapeDtypeStruct((TOP_K, t), jnp.int32),
                   jax.ShapeDtypeStruct((TOP_K, t), F32),
                   jax.ShapeDtypeStruct((TOP_K, t), jnp.int32),
                   jax.ShapeDtypeStruct((n_e, LANES), F32)],
        compiler_params=_cparams(("arbitrary",)), name="router",
    )(h1b, w_t, b_col)


def _expert_kernel(te_ref, nv_ref, x_ref, wgu_ref, bgu_ref, wd_ref, bd_ref, y_ref, wgu_bf, wd_bf):
    i = pl.program_id(0)

    @pl.when(jnp.logical_or(i == 0, te_ref[i] != te_ref[jnp.maximum(i - 1, 0)]))
    def _():
        wgu_bf[...] = wgu_ref[...].astype(BF16)
        wd_bf[...] = wd_ref[...].astype(BF16)

    @pl.when(i < nv_ref[0])
    def _():
        gu = _dot(x_ref[...].astype(BF16), wgu_bf[...]) + bgu_ref[...]
        gate = jnp.minimum(gu[:, :D_FF], SWIGLU_LIMIT)
        up = jnp.clip(gu[:, D_FF:], -SWIGLU_LIMIT, SWIGLU_LIMIT)
        act = (up + 1.0) * gate * _sigmoid(SWIGLU_ALPHA * gate)
        y_ref[...] = (_dot(act.astype(BF16), wd_bf[...]) + bd_ref[...]).astype(y_ref.dtype)

    @pl.when(i >= nv_ref[0])
    def _():
        y_ref[...] = jnp.zeros_like(y_ref)


def _experts(tile_expert, n_valid, xs, wgu, bgu, wd, bd, layer):
    p, d = xs.shape
    tm = EXPERT_TM
    two_ff = wgu.shape[-1]
    grid_spec = pltpu.PrefetchScalarGridSpec(
        num_scalar_prefetch=2, grid=(p // tm,),
        in_specs=[pl.BlockSpec((tm, d), lambda i, te, nv: (i, 0)),
                  pl.BlockSpec((None, None, d, two_ff), lambda i, te, nv: (layer, te[i], 0, 0)),
                  pl.BlockSpec((None, 1, two_ff), lambda i, te, nv: (te[i], 0, 0)),
                  pl.BlockSpec((None, None, two_ff // 2, d), lambda i, te, nv: (layer, te[i], 0, 0)),
                  pl.BlockSpec((None, 1, d), lambda i, te, nv: (te[i], 0, 0))],
        out_specs=pl.BlockSpec((tm, d), lambda i, te, nv: (i, 0)),
        scratch_shapes=[pltpu.VMEM((d, two_ff), BF16), pltpu.VMEM((two_ff // 2, d), BF16)])
    return pl.pallas_call(
        _expert_kernel, grid_spec=grid_spec,
        out_shape=jax.ShapeDtypeStruct((p, d), F32),
        compiler_params=_cparams(("arbitrary",)), name="experts",
    )(tile_expert, n_valid, xs, wgu, bgu, wd, bd)


def _out_kernel(h1_ref, h1b_ref, yg_ref, cw_ref, p_ref, wpg_ref, wpp_ref, lng_ref, lnb_ref, o_ref, *, alpha):
    cw = cw_ref[...]
    moe = cw[:, 0:1] * yg_ref[0]
    for k in range(1, TOP_K):
        moe = moe + cw[:, k:k + 1] * yg_ref[k]
    ple = _sigmoid(_dot(h1b_ref[...], wpg_ref[...])) * _dot(p_ref[...].astype(BF16), wpp_ref[...])
    u = alpha * h1_ref[...] + moe + ple
    o_ref[...] = _layer_norm(u, lng_ref[...], lnb_ref[...])


def _out_ln2(h1, h1b, yg, cw, p, wpg, wpp, lng, lnb, alpha):
    t, d = h1.shape
    tm = OUT_TM
    row = lambda w: pl.BlockSpec((tm, w), lambda i: (i, 0))
    full = lambda a: pl.BlockSpec(a.shape, lambda i: (0, 0))
    return pl.pallas_call(
        functools.partial(_out_kernel, alpha=alpha),
        grid=(t // tm,),
        in_specs=[row(d), row(d), pl.BlockSpec((TOP_K, tm, d), lambda i: (0, i, 0)), row(TOP_K),
                  row(p.shape[1]), full(wpg), full(wpp), full(lng), full(lnb)],
        out_specs=row(d),
        out_shape=jax.ShapeDtypeStruct((t, d), F32),
        compiler_params=_cparams(("parallel",)), name="out_ln2",
    )(h1, h1b, yg, cw, p, wpg, wpp, lng, lnb)


def _dup_groups(w):
    d = w.shape[0]
    w = w.reshape(d, NSA_KV_GROUPS, 1, HEAD_DIM)
    return jnp.broadcast_to(w, (d, NSA_KV_GROUPS, 2, HEAD_DIM)).reshape(d, NSA_KV_GROUPS * LANES)


def _in_proj_weights(w_in):
    sbw = SB_HEADS * HEAD_DIM
    qw = NSA_HEADS * HEAD_DIM
    kvw = NSA_KV_GROUPS * HEAD_DIM
    d = w_in.shape[0]
    widths = (sbw, sbw, sbw, qw) + (kvw,) * 6 + (3 * NSA_HEADS, d, d)
    parts, start = [], 0
    for wd in widths:
        parts.append(w_in[:, start:start + wd])
        start += wd
    (sb_q, sb_k, sb_v, nsa_q, k_cmp, v_cmp, k_sel, v_sel, k_win, v_win, gate, g_sb, g_nsa) = parts
    k_sw = jnp.concatenate([_dup_groups(k_sel), _dup_groups(k_win)], axis=1)
    v_sw_t = jnp.concatenate([_dup_groups(v_sel), _dup_groups(v_win)], axis=1).T
    gate = jnp.pad(gate, ((0, 0), (0, GATE_PAD - gate.shape[1])))
    g_both = jnp.concatenate([g_sb, g_nsa], axis=1)
    ws = [sb_q, sb_k, sb_v, nsa_q, k_cmp, v_cmp, k_sw, v_sw_t, gate, g_both]
    dtypes = [BF16, BF16, BF16, BF16, F32, F32, BF16, BF16, F32, F32]
    transposed = [False] * 7 + [True, False, False]
    scale = HEAD_DIM ** -0.5
    scales = [scale, 1.0, 1.0, scale] + [1.0] * 6
    return [w.astype(BF16) for w in ws], dtypes, scales, transposed


def _compress_weights(w1, w2, pe):
    eye = jnp.eye(NSA_KV_GROUPS, dtype=w1.dtype)
    w1b = jnp.einsum('ldh,pg->lpdgh', w1, eye).reshape(CMP_BLOCK * NSA_KV_GROUPS * HEAD_DIM,
                                                       NSA_KV_GROUPS * CMP_HIDDEN)
    w2b = jnp.einsum('hd,pg,r->phgrd', w2, eye, jnp.ones((2,), w2.dtype)).reshape(
        NSA_KV_GROUPS * CMP_HIDDEN, NSA_KV_GROUPS * LANES)
    peb = jnp.broadcast_to(pe[:, None, :], (CMP_BLOCK, NSA_KV_GROUPS, HEAD_DIM)).reshape(1, -1)
    return peb.astype(F32), w1b.astype(BF16), w2b.astype(BF16)


def _moe_plan(idx_t, rank_t, counts, tm):
    k, t = idx_t.shape
    n = k * t
    i32 = jnp.int32
    experts = jnp.arange(N_EXPERTS, dtype=i32)
    padded = ((counts + tm - 1) // tm) * tm
    pad_end = jnp.cumsum(padded)
    pad_off = pad_end - padded
    gap = padded - counts
    pos = rank_t + jnp.sum(jnp.where(idx_t[None] == experts[:, None, None], pad_off[:, None, None], 0), axis=0)
    tok = jnp.arange(t, dtype=i32)
    ids = ((tok // ROUTER_TM) * (k * ROUTER_TM) + tok % ROUTER_TM)[None, :] + jnp.arange(k, dtype=i32)[:, None] * ROUTER_TM
    n_pad = N_EXPERTS * tm
    p_rows = n + n_pad
    id_bits = (p_rows - 1).bit_length()
    assert (N_EXPERTS + 1) << id_bits < 2 ** 31
    spare_e = jnp.repeat(experts, tm, total_repeat_length=n_pad)
    spare_used = jnp.tile(jnp.arange(tm, dtype=i32), N_EXPERTS) < jnp.repeat(gap, tm, total_repeat_length=n_pad)
    keys = jnp.concatenate([idx_t.reshape(n), jnp.where(spare_used, spare_e, N_EXPERTS)])
    ids = jnp.concatenate([ids.reshape(n), n + jnp.arange(n_pad, dtype=i32)])
    entry = lax.sort(keys * (1 << id_bits) + ids) & ((1 << id_bits) - 1)
    tok_of_entry = (entry // (k * ROUTER_TM)) * ROUTER_TM + entry % ROUTER_TM
    src_tok = jnp.where(entry < n, tok_of_entry, 0)
    pos = pos.reshape(n)
    tile_start = jnp.arange(p_rows // tm, dtype=i32) * tm
    tile_expert = jnp.minimum(jnp.sum((tile_start[:, None] >= pad_end[None, :]).astype(i32), axis=1), N_EXPERTS - 1)
    n_valid = (pad_end[-1] // tm).astype(i32).reshape(1)
    return src_tok, pos, tile_expert, n_valid


def _layer(h, p_i, w_in, w_cmp1, w_cmp2, pe_cmp, w_br_sb, w_br_nsa, w_o, ln1_g, ln1_b, ln2_g, ln2_b,
           w_router, b_router, w_gu_all, b_gu, w_down_all, b_down, w_ple_gate, w_ple_proj,
           *, batch, alpha, layer):
    t, d = h.shape
    s = t // batch
    sb_q, sb_k, sb_v, nsa_q, k_cmp, v_cmp, k_sw, v_sw_t, gate, g_both = _in_proj(h, *_in_proj_weights(w_in))
    b3 = lambda a: a.reshape(batch, s, a.shape[1])

    y_sb = _sb_attention(b3(sb_q), b3(sb_k), b3(sb_v)).reshape(t, -1)

    nc = s // CMP_BLOCK
    cmp_rows = lambda a: a.reshape(batch, nc // 2, 2 * CMP_BLOCK * a.shape[1])
    kc = _compress(cmp_rows(k_cmp), *_compress_weights(w_cmp1[0], w_cmp2[0], pe_cmp[0]))
    vc = _compress(cmp_rows(v_cmp), *_compress_weights(w_cmp1[1], w_cmp2[1], pe_cmp[1]))
    y_cmp, sel, flags = _nsa_cmp(b3(nsa_q), kc, vc, b3(gate))
    y_nsa = _nsa_main(b3(nsa_q), b3(k_sw), v_sw_t, sel, flags, b3(gate), y_cmp).reshape(t, -1)

    row = lambda v: v.reshape(1, -1).astype(F32)
    h1, h1b = _merge_ln1(y_sb, y_nsa, g_both, h, w_br_sb.astype(BF16), w_br_nsa.astype(BF16),
                         w_o.astype(BF16), row(ln1_g), row(ln1_b), alpha)

    idx_t, wt_t, rank_t, counts = _router(h1b, w_router.T.astype(BF16), b_router.reshape(-1, 1).astype(F32))
    src_tok, pos, tile_expert, n_valid = _moe_plan(idx_t, rank_t, counts[:, 0].astype(jnp.int32), EXPERT_TM)
    xs = h1.at[src_tok].get(mode='promise_in_bounds')
    y = _experts(tile_expert, n_valid, xs, w_gu_all, b_gu.reshape(N_EXPERTS, 1, -1),
                 w_down_all, b_down.reshape(N_EXPERTS, 1, -1), layer)
    yg = y.at[pos].get(mode='promise_in_bounds').reshape(TOP_K, t, d)
    return _out_ln2(h1, h1b, yg, wt_t.T, p_i, w_ple_gate.astype(BF16), w_ple_proj.astype(BF16),
                    row(ln2_g), row(ln2_b), alpha)


def kernel(x, p, w_in, w_cmp1, w_cmp2, pe_cmp, w_br_sb, w_br_nsa, w_o, ln1_g, ln1_b, ln2_g, ln2_b,
           w_router, b_router, w_gu, b_gu, w_down, b_down, w_ple_gate, w_ple_proj):
    batch, s, d = x.shape
    depth = w_in.shape[0]
    alpha = (2 * depth) ** 0.25
    h = x.reshape(batch * s, d)
    for i in range(depth):
        h = _layer(h, p[i].reshape(batch * s, -1), w_in[i], w_cmp1[i], w_cmp2[i], pe_cmp[i],
                   w_br_sb[i], w_br_nsa[i], w_o[i], ln1_g[i], ln1_b[i], ln2_g[i], ln2_b[i],
                   w_router[i], b_router[i], w_gu, b_gu[i], w_down, b_down[i],
                   w_ple_gate[i], w_ple_proj[i], batch=batch, alpha=alpha, layer=i)
    return h.reshape(batch, s, d)
```

```python
import functools

import jax
import jax.numpy as jnp
from jax import lax
from jax.experimental import pallas as pl
from jax.experimental.pallas import tpu as pltpu

F32 = jnp.float32
BF16 = jnp.bfloat16

HEAD_DIM = 64
LANES = 128
SB_HEADS = 8
NSA_HEADS = 8
NSA_KV_GROUPS = 2
NSA_GROUP = NSA_HEADS // NSA_KV_GROUPS
CMP_BLOCK = 32
CMP_HIDDEN = 128
SEL_BLOCK = 64
N_SEL = 16
WINDOW = 512
N_EXPERTS = 32
TOP_K = 4
D_FF = 1024
SWIGLU_ALPHA = 1.702
SWIGLU_LIMIT = 7.0
LN_EPS = 1e-5
NEG_INF = -1e30
FORCED_SCORE = 1e6
GATE_PAD = 128
HI_HALF = -65536
MASK_BIG = 2.0 ** 100
FEAT_HI, FEAT_LO = 64, 65
SB_DEAD_LOG = -104.0
SB_FIRST_TILES = 3

VMEM_LIMIT = 56 * 1024 * 1024

ATTN_TQ = 128
NSA_TQ = 256
SEL_TK = 512
PROJ_TM = 256
MERGE_TM = 256
ROUTER_TM = 512
EXPERT_TM = 512
OUT_TM = 256


def _cparams(sem):
    return pltpu.CompilerParams(dimension_semantics=sem, vmem_limit_bytes=VMEM_LIMIT)


def _sigmoid(x):
    return 1.0 / (1.0 + jnp.exp(-x))


def _div_pow2(x, n):
    assert n & (n - 1) == 0
    return lax.shift_right_logical(x, n.bit_length() - 1)


def _mod_pow2(x, n):
    assert n & (n - 1) == 0
    return x & (n - 1)


def _split_bf16(x):
    hi = x.astype(BF16)
    lo = (x - hi.astype(F32)).astype(BF16)
    return hi, lo


def _dot(a, b):
    return jnp.dot(a, b, preferred_element_type=F32)


def _dot_nt(a, b):
    return lax.dot_general(a, b, (((1,), (1,)), ((), ())), preferred_element_type=F32)


def _dot_tn(a, b):
    return lax.dot_general(a, b, (((0,), (0,)), ((), ())), preferred_element_type=F32)


def _pack_bf16_pairs(x):
    n = x.shape[1] // 2
    bits = lax.bitcast_convert_type(x.astype(BF16).astype(F32), jnp.int32)
    return (bits[:, n:] & HI_HALF) | lax.shift_right_logical(bits[:, :n], 16)


def _unpack_bf16_pairs(p):
    lo = lax.bitcast_convert_type(lax.shift_left(p, 16), F32)
    hi = lax.bitcast_convert_type(p & HI_HALF, F32)
    return jnp.concatenate([lo, hi], axis=1)


def _layer_norm(u, g, b):
    mu = jnp.mean(u, axis=-1, keepdims=True)
    d = u - mu
    var = jnp.mean(d * d, axis=-1, keepdims=True)
    return d * lax.rsqrt(var + LN_EPS) * g + b


def _in_proj_kernel(x_ref, *refs, scales, transposed):
    n = len(scales)
    x = x_ref[...].astype(BF16)
    for w_ref, o_ref, s, tr in zip(refs[:n], refs[n:], scales, transposed):
        acc = _dot_nt(w_ref[...], x) if tr else _dot(x, w_ref[...])
        if s != 1.0:
            acc = acc * s
        o_ref[...] = acc.astype(o_ref.dtype)


def _in_proj(h, weights, dtypes, scales, transposed):
    t, d = h.shape
    tm = PROJ_TM
    in_specs = [pl.BlockSpec((tm, d), lambda i: (i, 0))]
    in_specs += [pl.BlockSpec(w.shape, lambda i: (0, 0)) for w in weights]
    out_specs, out_shape = [], []
    for w, dt, tr in zip(weights, dtypes, transposed):
        if tr:
            out_specs.append(pl.BlockSpec((w.shape[0], tm), lambda i: (0, i)))
            out_shape.append(jax.ShapeDtypeStruct((w.shape[0], t), dt))
        else:
            out_specs.append(pl.BlockSpec((tm, w.shape[1]), lambda i: (i, 0)))
            out_shape.append(jax.ShapeDtypeStruct((t, w.shape[1]), dt))
    return pl.pallas_call(
        functools.partial(_in_proj_kernel, scales=tuple(scales), transposed=tuple(transposed)),
        grid=(t // tm,), in_specs=in_specs, out_specs=out_specs, out_shape=out_shape,
        compiler_params=_cparams(("parallel",)), name="in_proj",
    )(h, *weights)


def _stack_heads(q2, tq):
    lane = lax.broadcasted_iota(jnp.int32, (tq, LANES), 1)
    zero = jnp.zeros_like(q2)
    return jnp.concatenate([jnp.where(lane < HEAD_DIM, q2, zero),
                            jnp.where(lane >= HEAD_DIM, q2, zero)], axis=0)


def _unstack_heads(o, tq):
    lane = lax.broadcasted_iota(jnp.int32, (tq, LANES), 1)
    return jnp.where(lane < HEAD_DIM, o[:tq], o[tq:])


def _sb_kernel(q_ref, k_ref, v_ref, o_ref, *, tq, n_pairs):
    i = pl.program_id(1)
    r_i = lax.broadcasted_iota(jnp.int32, (tq, tq), 0)
    c_i = lax.broadcasted_iota(jnp.int32, (tq, tq), 1)
    tri = jnp.where(r_i > c_i, 1.0, 0.0).astype(BF16)
    rhs = jnp.concatenate([tri, jnp.ones((tq, tq), BF16)], axis=1)
    rhs = jnp.concatenate([rhs, rhs], axis=0)
    qs = [_stack_heads(q_ref[:, p * LANES:(p + 1) * LANES], tq) for p in range(n_pairs)]

    def log_stay(z):
        return -(jnp.maximum(z, 0.0) + jnp.log(1.0 + jnp.exp(-jnp.abs(z))))

    def tile_sums(ls):
        hi, lo = _split_bf16(ls)
        return _dot(jnp.concatenate([hi, lo], axis=1), rhs)

    wide = SB_FIRST_TILES * tq
    first = jnp.maximum(i - (SB_FIRST_TILES - 1), 0)
    start0 = pl.multiple_of(first * tq, tq)
    col_minus_row = (lax.broadcasted_iota(jnp.int32, (2 * tq, wide), 1)
                     - (lax.broadcasted_iota(jnp.int32, (2 * tq, wide), 0) & (tq - 1)))
    causal = col_minus_row < (i - first) * tq

    def first_step(p):
        kw = k_ref[pl.ds(start0, wide), p * LANES:(p + 1) * LANES]
        vw = v_ref[pl.ds(start0, wide), p * LANES:(p + 1) * LANES]
        z = _dot_nt(qs[p], kw)
        ls = jnp.where(causal, log_stay(z), 0.0)
        sums = [tile_sums(ls[:, j * tq:(j + 1) * tq]) for j in range(SB_FIRST_TILES)]
        later, run = [None] * SB_FIRST_TILES, None
        for j in reversed(range(SB_FIRST_TILES)):
            later[j] = sums[j][:, :tq] if run is None else sums[j][:, :tq] + run
            run = sums[j][:, tq:] if run is None else run + sums[j][:, tq:]
        w = jnp.where(causal, jnp.exp(z + ls + jnp.concatenate(later, axis=1)), 0.0)
        return run, _dot(w.astype(BF16), vw)

    def tile(p, j, carry, acc):
        start = pl.multiple_of(j * tq, tq)
        kj = k_ref[pl.ds(start, tq), p * LANES:(p + 1) * LANES]
        vj = v_ref[pl.ds(start, tq), p * LANES:(p + 1) * LANES]
        z = _dot_nt(qs[p], kj)
        ls = log_stay(z)
        sums = tile_sums(ls)
        w = jnp.exp(z + ls + sums[:, :tq] + carry)
        return carry + sums[:, tq:], acc + _dot(w.astype(BF16), vj)

    def live(carries):
        worst = functools.reduce(jnp.maximum, carries)
        return jnp.max(worst) > SB_DEAD_LOG

    state = [first_step(p) for p in range(n_pairs)]
    carries = tuple(s[0] for s in state)
    accs = tuple(s[1] for s in state)

    def cond(c):
        return jnp.logical_and(c[0] <= first, c[1])

    def body(c):
        step, _, carries, accs = c
        state = [tile(p, first - step, carries[p], accs[p]) for p in range(n_pairs)]
        carries = tuple(s[0] for s in state)
        return step + 1, live(carries), carries, tuple(s[1] for s in state)

    _, _, _, accs = lax.while_loop(cond, body, (jnp.int32(1), live(carries), carries, accs))
    for p in range(n_pairs):
        o_ref[:, p * LANES:(p + 1) * LANES] = _unstack_heads(accs[p], tq).astype(o_ref.dtype)


def _sb_attention(q, k, v):
    b, s, w = q.shape
    tq = ATTN_TQ
    return pl.pallas_call(
        functools.partial(_sb_kernel, tq=tq, n_pairs=w // LANES),
        grid=(b, s // tq),
        in_specs=[pl.BlockSpec((None, tq, w), lambda bi, i: (bi, i, 0)),
                  pl.BlockSpec((None, s, w), lambda bi, i: (bi, 0, 0)),
                  pl.BlockSpec((None, s, w), lambda bi, i: (bi, 0, 0))],
        out_specs=pl.BlockSpec((None, tq, w), lambda bi, i: (bi, i, 0)),
        out_shape=jax.ShapeDtypeStruct((b, s, w), BF16),
        compiler_params=_cparams(("parallel", "arbitrary")), name="sb_attn",
    )(q, k, v)


def _compress_kernel(x_ref, pe_ref, w1_ref, w2_ref, o_ref):
    x = (x_ref[...] + pe_ref[...]).astype(BF16)
    hid = _dot(x, w1_ref[...])
    hid = hid * _sigmoid(hid)
    o_ref[...] = _dot(hid.astype(BF16), w2_ref[...]).astype(o_ref.dtype)


def _compress(x, pe, w1, w2):
    b, half, width = x.shape
    blk = width // 2
    out = pl.pallas_call(
        _compress_kernel,
        grid=(b, 2),
        in_specs=[pl.BlockSpec((None, half, blk), lambda bi, par: (bi, 0, par)),
                  pl.BlockSpec(pe.shape, lambda bi, par: (0, 0)),
                  pl.BlockSpec(w1.shape, lambda bi, par: (0, 0)),
                  pl.BlockSpec(w2.shape, lambda bi, par: (0, 0))],
        out_specs=pl.BlockSpec((None, None, half, w2.shape[1]), lambda bi, par: (bi, par, 0, 0)),
        out_shape=jax.ShapeDtypeStruct((b, 2, half, w2.shape[1]), BF16),
        compiler_params=_cparams(("parallel", "parallel")), name="compress",
    )(x, pe, w1, w2)
    return out.reshape(b, 2 * half, w2.shape[1])


def _stack_group(q, tq):
    return jnp.concatenate([_stack_heads(q[:, :LANES], tq), _stack_heads(q[:, LANES:], tq)], axis=0)


def _unstack_group(o, tq):
    return jnp.concatenate([_unstack_heads(o[:2 * tq], tq), _unstack_heads(o[2 * tq:], tq)], axis=1)


def _alibi_slope(head_idx):
    return lax.bitcast_convert_type(lax.shift_left(126 - head_idx, 23), F32)


def _expand_gate(gate_sig, g, branch):
    c_i = lax.broadcasted_iota(jnp.int32, (GATE_PAD, 2 * LANES), 0)
    l_i = lax.broadcasted_iota(jnp.int32, (GATE_PAD, 2 * LANES), 1)
    col = g * (3 * NSA_GROUP) + _div_pow2(l_i, HEAD_DIM) * 3 + branch
    onehot = jnp.where(c_i == col, 1.0, 0.0).astype(BF16)
    hi, lo = _split_bf16(gate_sig)
    return _dot(hi, onehot) + _dot(lo, onehot)


def _nsa_cmp_kernel(q_ref, kc_ref, vc_ref, gate_ref, y_ref, sel_ref, flag_ref, *, tq, n_sel):
    g = pl.program_id(1)
    i = pl.program_id(2)
    nc = kc_ref.shape[0]
    half = nc // 2
    qst = _stack_group(q_ref[...], tq)
    s = _dot_nt(kc_ref[...], qst)
    r_i = lax.broadcasted_iota(jnp.int32, (nc, 4 * tq), 0)
    l_i = lax.broadcasted_iota(jnp.int32, (nc, 4 * tq), 1)
    cblk = 2 * _mod_pow2(r_i, half) + _div_pow2(r_i, half)
    c_end = cblk * CMP_BLOCK + (CMP_BLOCK - 1)
    t = i * tq + _mod_pow2(l_i, tq)
    dist = t - c_end
    slope = _alibi_slope(g * NSA_GROUP + _div_pow2(l_i, tq))
    s = s - slope * dist.astype(F32)
    mask = dist >= 0
    m = jnp.max(jnp.where(mask, s, NEG_INF), axis=0, keepdims=True)
    p = jnp.where(mask, jnp.exp(s - m), 0.0)
    l = jnp.sum(p, axis=0, keepdims=True)
    p = p * jnp.where(l > 0.0, 1.0 / l, 0.0)

    o = _dot_tn(p.astype(BF16), vc_ref[...])
    gate = _expand_gate(_sigmoid(gate_ref[...]), g, 0)
    y_ref[...] = (_unstack_group(o, tq) * gate).astype(y_ref.dtype)

    imp_c = p[:, 0:tq] + p[:, tq:2 * tq] + p[:, 2 * tq:3 * tq] + p[:, 3 * tq:4 * tq]
    imp = imp_c[:half] + imp_c[half:]
    n_blk = half
    blk = lax.broadcasted_iota(jnp.int32, (n_blk, tq), 0)
    tt = i * tq + lax.broadcasted_iota(jnp.int32, (n_blk, tq), 1)
    cur = _div_pow2(tt, SEL_BLOCK)
    score = jnp.where(blk * SEL_BLOCK <= tt, imp, NEG_INF)
    for forced_blk in (0, cur, cur - 1):
        score = jnp.where(blk == forced_blk, FORCED_SCORE, score)

    taken = -jnp.inf
    sel = jnp.zeros((n_blk, tq), F32)
    for _ in range(n_sel):
        mx = jnp.max(score, axis=0, keepdims=True)
        first = jnp.min(jnp.where(score == mx, blk, n_blk), axis=0, keepdims=True)
        hit = blk == first
        sel = jnp.where(hit, 1.0, sel)
        score = jnp.where(hit, taken, score)
    sel = jnp.concatenate([sel, jnp.zeros((LANES - n_blk, tq), F32)], axis=0)
    sel_ref[...] = sel.T
    picks = _dot_nt(jnp.ones((8, tq), BF16), sel.astype(BF16))
    flag_ref[...] = picks[0:1].astype(jnp.int32)


def _nsa_cmp(q, kc, vc, gate):
    b, s, _ = q.shape
    tq = NSA_TQ
    nc = kc.shape[1]
    n_blk = nc // 2
    assert n_blk <= LANES
    n_sel = min(N_SEL, n_blk)
    return pl.pallas_call(
        functools.partial(_nsa_cmp_kernel, tq=tq, n_sel=n_sel),
        grid=(b, NSA_KV_GROUPS, s // tq),
        in_specs=[pl.BlockSpec((None, tq, 2 * LANES), lambda bi, g, i: (bi, i, g)),
                  pl.BlockSpec((None, nc, LANES), lambda bi, g, i: (bi, 0, g)),
                  pl.BlockSpec((None, nc, LANES), lambda bi, g, i: (bi, 0, g)),
                  pl.BlockSpec((None, tq, GATE_PAD), lambda bi, g, i: (bi, i, 0))],
        out_specs=[pl.BlockSpec((None, tq, 2 * LANES), lambda bi, g, i: (bi, i, g)),
                   pl.BlockSpec((None, None, tq, LANES), lambda bi, g, i: (bi, g, i, 0)),
                   pl.BlockSpec((None, None, None, 1, LANES), lambda bi, g, i: (bi, g, i, 0, 0))],
        out_shape=[jax.ShapeDtypeStruct((b, s, 4 * LANES), F32),
                   jax.ShapeDtypeStruct((b, NSA_KV_GROUPS, s, LANES), F32),
                   jax.ShapeDtypeStruct((b, NSA_KV_GROUPS, s // tq, 1, LANES), jnp.int32)],
        compiler_params=_cparams(("parallel", "parallel", "parallel")), name="nsa_cmp",
    )(q, kc, vc, gate)


def _key_features(s, tk, wk):
    def table(n, period, with_blocks):
        key = lax.broadcasted_iota(jnp.int32, (n, LANES), 0)
        lane = lax.broadcasted_iota(jnp.int32, (n, LANES), 1)
        r = key % period
        pos = jnp.where(lane == FEAT_HI, r // SEL_BLOCK, jnp.where(lane == FEAT_LO, r % SEL_BLOCK, 0))
        if with_blocks:
            pos = jnp.where(lane < SEL_BLOCK, (key // SEL_BLOCK == lane).astype(jnp.int32), pos)
        return pos.astype(BF16)
    return table(s, tk, True), table(wk, wk, False)


def _nsa_main_kernel(flag_ref, q_ref, ks_ref, kw_ref, vs_ref, vw_ref, kfeat_ref, rfeat_ref, sel_ref,
                     gate_ref, ycmp_ref, o_ref, m_ref, l_ref, acc_ref, *, tq, tk):
    g = pl.program_id(1)
    i = pl.program_id(2)
    cols = NSA_GROUP * tq
    qs = pl.multiple_of(i * tq, tq)
    qst = _stack_group(q_ref[...], tq)
    col = lax.broadcasted_iota(jnp.int32, (1, cols), 1)
    t_loc = _mod_pow2(col, tq)
    slope_row = _alibi_slope(g * NSA_GROUP + _div_pow2(col, tq))

    lane = lax.broadcasted_iota(jnp.int32, (tq, LANES), 1)
    sel = sel_ref[...]
    first_blk = i * (tq // SEL_BLOCK)
    add_diag = jnp.where(sel > 0.5, 0.0, -MASK_BIG)
    add_bulk = jnp.where(lane < first_blk, add_diag, -MASK_BIG)
    f_bulk, f_diag, f_win = [], [], []
    for h in range(NSA_GROUP):
        slope = _alibi_slope(jnp.full((tq, LANES), g * NSA_GROUP + h, jnp.int32))
        pos = jnp.where(lane == FEAT_HI, slope * SEL_BLOCK, jnp.where(lane == FEAT_LO, slope, 0.0))
        f_win.append(pos)
        f_bulk.append(jnp.where(lane < SEL_BLOCK, add_bulk, pos))
        f_diag.append(jnp.where(lane < SEL_BLOCK, add_diag, pos))
    aug = lambda f: jnp.concatenate([qst, jnp.concatenate(f, axis=0).astype(BF16)], axis=1)
    q_sel, q_diag, q_win = aug(f_bulk), aug(f_diag), aug(f_win)

    def online(k, q_aug, mask_add, bias, v_t):
        s = _dot_nt(k, q_aug)
        if mask_add is not None:
            s = s + mask_add
        m = m_ref[...]
        m_new = jnp.maximum(m, jnp.max(s, axis=0, keepdims=True) + bias)
        alpha = jnp.exp(m - m_new)
        p = jnp.exp(s + (bias - m_new))
        m_ref[...] = m_new
        l_ref[...] = alpha * l_ref[...] + jnp.sum(p, axis=0, keepdims=True)
        acc_ref[...] = alpha * acc_ref[...] + _dot(v_t, p.astype(BF16))

    def untranspose(o_t):
        return jnp.concatenate([o_t[:, h * tq:(h + 1) * tq].T for h in range(NSA_GROUP)], axis=0)

    r_i = lax.broadcasted_iota(jnp.int32, (tq, tq), 0)
    c_i = lax.broadcasted_iota(jnp.int32, (tq, tq), 1)
    causal_add = jnp.where(r_i <= c_i, 0.0, -MASK_BIG)
    causal_add = jnp.concatenate([causal_add] * NSA_GROUP, axis=1)
    m_ref[...] = jnp.full((1, cols), NEG_INF, F32)
    l_ref[...] = jnp.zeros((1, cols), F32)
    acc_ref[...] = jnp.zeros((LANES, cols), F32)
    k = jnp.concatenate([ks_ref[pl.ds(qs, tq), :], kfeat_ref[pl.ds(qs, tq), :]], axis=1)
    bias = -slope_row * ((qs & (tk - 1)) + t_loc).astype(F32)
    online(k, q_diag, causal_add, bias, vs_ref[:, pl.ds(qs, tq)])

    def bulk(kt, _):
        start = pl.multiple_of(kt * tk, tk)
        picked = 0
        for sub in range(flag_ref.shape[0]):
            for j in range(tk // SEL_BLOCK):
                picked = picked + flag_ref[sub, 0, kt * (tk // SEL_BLOCK) + j]

        @pl.when(picked > 0)
        def _():
            k = jnp.concatenate([ks_ref[pl.ds(start, tk), :], kfeat_ref[pl.ds(start, tk), :]], axis=1)
            bias = -slope_row * (qs - start + t_loc).astype(F32)
            online(k, q_sel, None, bias, vs_ref[:, pl.ds(start, tk)])

        return 0

    lax.fori_loop(0, (qs + tk - 1) // tk, bulk, 0)
    o_sel = untranspose(acc_ref[...] * (1.0 / l_ref[...]))

    wk = WINDOW + tq
    start = pl.multiple_of(jnp.maximum(qs - WINDOW, 0), tq)
    w_r = lax.broadcasted_iota(jnp.int32, (wk, tq), 0)
    w_c = lax.broadcasted_iota(jnp.int32, (wk, tq), 1)
    dist = (qs - start) + w_c - w_r
    band_add = jnp.where(jnp.where(dist >= 0, dist, WINDOW) < WINDOW, 0.0, -MASK_BIG)
    band_add = jnp.concatenate([band_add] * NSA_GROUP, axis=1)
    k = jnp.concatenate([kw_ref[pl.ds(start, wk), :], rfeat_ref[...]], axis=1)
    s = _dot_nt(k, q_win) + band_add
    p = jnp.exp(s - jnp.max(s, axis=0, keepdims=True))
    l_w = jnp.sum(p, axis=0, keepdims=True)
    o_win = untranspose(_dot(vw_ref[:, pl.ds(start, wk)], p.astype(BF16)) * (1.0 / l_w))

    gate = _sigmoid(gate_ref[...])
    y = (ycmp_ref[...] + _expand_gate(gate, g, 1) * _unstack_group(o_sel, tq)
         + _expand_gate(gate, g, 2) * _unstack_group(o_win, tq))
    o_ref[...] = y.astype(o_ref.dtype)


def _nsa_main(q, k_sw, v_sw_t, sel, flags, gate, ycmp):
    b, s, _ = q.shape
    tq = NSA_TQ
    tk = min(SEL_TK, s)
    wk = WINDOW + tq
    cols = NSA_GROUP * tq
    flag_rows = 1
    assert s >= wk and s % tk == 0 and tk % tq == 0 and s // SEL_BLOCK <= SEL_BLOCK
    kfeat, rfeat = _key_features(s, tk, wk)
    k_spec = lambda off: pl.BlockSpec((None, s, LANES), lambda bi, g, i: (bi, 0, off + g))
    v_spec = lambda off: pl.BlockSpec((LANES, s), lambda bi, g, i: (off + g, bi))
    return pl.pallas_call(
        functools.partial(_nsa_main_kernel, tq=tq, tk=tk),
        grid=(b, NSA_KV_GROUPS, s // tq),
        in_specs=[pl.BlockSpec((None, None, flag_rows, 1, LANES), lambda bi, g, i: (bi, g, i, 0, 0),
                               memory_space=pltpu.SMEM),
                  pl.BlockSpec((None, tq, 2 * LANES), lambda bi, g, i: (bi, i, g)),
                  k_spec(0), k_spec(2), v_spec(0), v_spec(2),
                  pl.BlockSpec((s, LANES), lambda bi, g, i: (0, 0)),
                  pl.BlockSpec((wk, LANES), lambda bi, g, i: (0, 0)),
                  pl.BlockSpec((None, None, tq, LANES), lambda bi, g, i: (bi, g, i, 0)),
                  pl.BlockSpec((None, tq, GATE_PAD), lambda bi, g, i: (bi, i, 0)),
                  pl.BlockSpec((None, tq, 2 * LANES), lambda bi, g, i: (bi, i, g))],
        out_specs=pl.BlockSpec((None, tq, 2 * LANES), lambda bi, g, i: (bi, i, g)),
        out_shape=jax.ShapeDtypeStruct((b, s, 4 * LANES), BF16),
        scratch_shapes=[pltpu.VMEM((1, cols), F32), pltpu.VMEM((1, cols), F32), pltpu.VMEM((LANES, cols), F32)],
        compiler_params=_cparams(("parallel", "parallel", "arbitrary")), name="nsa_main",
    )(flags, q, k_sw, k_sw, v_sw_t, v_sw_t, kfeat, rfeat, sel, gate, ycmp)


def _merge_kernel(ysb_ref, ynsa_ref, g_ref, h_ref, wsb_ref, wnsa_ref, wo_ref, lng_ref, lnb_ref,
                  h1_ref, h1b_ref, h1p_ref, *, alpha):
    d = h_ref.shape[1]
    g = g_ref[...]
    merged = (_sigmoid(g[:, :d]) * _dot(ysb_ref[...], wsb_ref[...])
              + _sigmoid(g[:, d:]) * _dot(ynsa_ref[...], wnsa_ref[...]))
    u = alpha * h_ref[...] + _dot(merged.astype(BF16), wo_ref[...])
    h1 = _layer_norm(u, lng_ref[...], lnb_ref[...])
    h1_ref[...] = h1
    h1b_ref[...] = h1.astype(BF16)
    h1p_ref[...] = _pack_bf16_pairs(h1)


def _merge_ln1(ysb, ynsa, g, h, wsb, wnsa, wo, lng, lnb, alpha):
    t, d = h.shape
    tm = MERGE_TM
    row = lambda w: pl.BlockSpec((tm, w), lambda i: (i, 0))
    full = lambda a: pl.BlockSpec(a.shape, lambda i: (0, 0))
    return pl.pallas_call(
        functools.partial(_merge_kernel, alpha=alpha),
        grid=(t // tm,),
        in_specs=[row(ysb.shape[1]), row(ynsa.shape[1]), row(2 * d), row(d),
                  full(wsb), full(wnsa), full(wo), full(lng), full(lnb)],
        out_specs=[row(d), row(d), row(d // 2)],
        out_shape=[jax.ShapeDtypeStruct((t, d), F32), jax.ShapeDtypeStruct((t, d), BF16),
                   jax.ShapeDtypeStruct((t, d // 2), jnp.int32)],
        compiler_params=_cparams(("parallel",)), name="merge_ln1",
    )(ysb, ynsa, g, h, wsb, wnsa, wo, lng, lnb)


def _router_kernel(x_ref, w_ref, b_ref, idx_ref, wt_ref, rank_ref, cnt_ref):
    logits = _dot_nt(w_ref[...], x_ref[...]) + b_ref[...]
    n_e, tm = logits.shape
    e_i = lax.broadcasted_iota(jnp.int32, (n_e, tm), 0)
    vals, idxs = [], []
    for _ in range(TOP_K):
        mx = jnp.max(logits, axis=0, keepdims=True)
        first = jnp.min(jnp.where(logits == mx, e_i, n_e), axis=0, keepdims=True)
        vals.append(mx)
        idxs.append(first)
        logits = jnp.where(e_i == first, -jnp.inf, logits)
    ex = [jnp.exp(v - vals[0]) for v in vals]
    inv = 1.0 / (ex[0] + ex[1] + ex[2] + ex[3])
    idx_ref[...] = jnp.concatenate(idxs, axis=0)
    wt_ref[...] = jnp.concatenate([e * inv for e in ex], axis=0)

    @pl.when(pl.program_id(0) == 0)
    def _():
        cnt_ref[...] = jnp.zeros_like(cnt_ref)

    r_i = lax.broadcasted_iota(jnp.int32, (tm, tm), 0)
    c_i = lax.broadcasted_iota(jnp.int32, (tm, tm), 1)
    before = jnp.where(r_i < c_i, 1.0, 0.0).astype(BF16)
    ones = jnp.ones((tm, LANES), BF16)
    seen = cnt_ref[...][:, 0:1]
    ranks = []
    for first in idxs:
        hot = jnp.where(e_i == first, 1.0, 0.0)
        hot_bf = hot.astype(BF16)
        ranks.append(jnp.sum(hot * (seen + _dot(hot_bf, before)), axis=0, keepdims=True))
        seen = seen + _dot(hot_bf, ones)[:, 0:1]
    rank_ref[...] = jnp.concatenate(ranks, axis=0).astype(jnp.int32)
    cnt_ref[...] = jnp.broadcast_to(seen, cnt_ref.shape)


def _router(h1b, w_t, b_col):
    t, d = h1b.shape
    tm = ROUTER_TM
    n_e = w_t.shape[0]
    return pl.pallas_call(
        _router_kernel,
        grid=(t // tm,),
        in_specs=[pl.BlockSpec((tm, d), lambda i: (i, 0)),
                  pl.BlockSpec(w_t.shape, lambda i: (0, 0)),
                  pl.BlockSpec(b_col.shape, lambda i: (0, 0))],
        out_specs=[pl.BlockSpec((TOP_K, tm), lambda i: (0, i)),
                   pl.BlockSpec((TOP_K, tm), lambda i: (0, i)),
                   pl.BlockSpec((TOP_K, tm), lambda i: (0, i)),
                   pl.BlockSpec((n_e, LANES), lambda i: (0, 0))],
        out_shape=[jax.ShapeDtypeStruct((TOP_K, t), jnp.int32),
                   jax.ShapeDtypeStruct((TOP_K, t), F32),
                   jax.ShapeDtypeStruct((TOP_K, t), jnp.int32),
                   jax.ShapeDtypeStruct((n_e, LANES), F32)],
        compiler_params=_cparams(("arbitrary",)), name="router",
    )(h1b, w_t, b_col)


def _expert_kernel(te_ref, nv_ref, x_ref, wgu_ref, bgu_ref, wd_ref, bd_ref, y_ref, wgu_bf, wd_bf):
    i = pl.program_id(0)

    @pl.when(jnp.logical_or(i == 0, te_ref[i] != te_ref[jnp.maximum(i - 1, 0)]))
    def _():
        wgu_bf[...] = wgu_ref[...].astype(BF16)
        wd_bf[...] = wd_ref[...].astype(BF16)

    @pl.when(i < nv_ref[0])
    def _():
        gu = _dot(_unpack_bf16_pairs(x_ref[...]).astype(BF16), wgu_bf[...]) + bgu_ref[...]
        gate = jnp.minimum(gu[:, :D_FF], SWIGLU_LIMIT)
        up = jnp.clip(gu[:, D_FF:], -SWIGLU_LIMIT, SWIGLU_LIMIT)
        act = (up + 1.0) * gate * _sigmoid(SWIGLU_ALPHA * gate)
        y_ref[...] = _pack_bf16_pairs(_dot(act.astype(BF16), wd_bf[...]) + bd_ref[...])

    @pl.when(i >= nv_ref[0])
    def _():
        y_ref[...] = jnp.zeros_like(y_ref)


def _experts(tile_expert, n_valid, xs, wgu, bgu, wd, bd, layer):
    p = xs.shape[0]
    d = wgu.shape[-2]
    tm = EXPERT_TM
    two_ff = wgu.shape[-1]
    grid_spec = pltpu.PrefetchScalarGridSpec(
        num_scalar_prefetch=2, grid=(p // tm,),
        in_specs=[pl.BlockSpec((tm, d // 2), lambda i, te, nv: (i, 0)),
                  pl.BlockSpec((None, None, d, two_ff), lambda i, te, nv: (layer, te[i], 0, 0)),
                  pl.BlockSpec((None, 1, two_ff), lambda i, te, nv: (te[i], 0, 0)),
                  pl.BlockSpec((None, None, two_ff // 2, d), lambda i, te, nv: (layer, te[i], 0, 0)),
                  pl.BlockSpec((None, 1, d), lambda i, te, nv: (te[i], 0, 0))],
        out_specs=pl.BlockSpec((tm, d // 2), lambda i, te, nv: (i, 0)),
        scratch_shapes=[pltpu.VMEM((d, two_ff), BF16), pltpu.VMEM((two_ff // 2, d), BF16)])
    return pl.pallas_call(
        _expert_kernel, grid_spec=grid_spec,
        out_shape=jax.ShapeDtypeStruct((p, d // 2), jnp.int32),
        compiler_params=_cparams(("arbitrary",)), name="experts",
    )(tile_expert, n_valid, xs, wgu, bgu, wd, bd)


def _out_kernel(h1_ref, h1b_ref, yg_ref, cw_ref, p_ref, wpg_ref, wpp_ref, lng_ref, lnb_ref, o_ref, *, alpha):
    cw = cw_ref[...]
    moe = cw[:, 0:1] * _unpack_bf16_pairs(yg_ref[0])
    for k in range(1, TOP_K):
        moe = moe + cw[:, k:k + 1] * _unpack_bf16_pairs(yg_ref[k])
    ple = _sigmoid(_dot(h1b_ref[...], wpg_ref[...])) * _dot(p_ref[...].astype(BF16), wpp_ref[...])
    u = alpha * h1_ref[...] + moe + ple
    o_ref[...] = _layer_norm(u, lng_ref[...], lnb_ref[...])


def _out_ln2(h1, h1b, yg, cw, p, wpg, wpp, lng, lnb, alpha):
    t, d = h1.shape
    tm = OUT_TM
    row = lambda w: pl.BlockSpec((tm, w), lambda i: (i, 0))
    full = lambda a: pl.BlockSpec(a.shape, lambda i: (0, 0))
    return pl.pallas_call(
        functools.partial(_out_kernel, alpha=alpha),
        grid=(t // tm,),
        in_specs=[row(d), row(d), pl.BlockSpec((TOP_K, tm, d // 2), lambda i: (0, i, 0)), row(TOP_K),
                  row(p.shape[1]), full(wpg), full(wpp), full(lng), full(lnb)],
        out_specs=row(d),
        out_shape=jax.ShapeDtypeStruct((t, d), F32),
        compiler_params=_cparams(("parallel",)), name="out_ln2",
    )(h1, h1b, yg, cw, p, wpg, wpp, lng, lnb)


def _dup_groups(w):
    d = w.shape[0]
    w = w.reshape(d, NSA_KV_GROUPS, 1, HEAD_DIM)
    return jnp.broadcast_to(w, (d, NSA_KV_GROUPS, 2, HEAD_DIM)).reshape(d, NSA_KV_GROUPS * LANES)


def _in_proj_weights(w_in):
    sbw = SB_HEADS * HEAD_DIM
    qw = NSA_HEADS * HEAD_DIM
    kvw = NSA_KV_GROUPS * HEAD_DIM
    d = w_in.shape[0]
    widths = (sbw, sbw, sbw, qw) + (kvw,) * 6 + (3 * NSA_HEADS, d, d)
    parts, start = [], 0
    for wd in widths:
        parts.append(w_in[:, start:start + wd])
        start += wd
    (sb_q, sb_k, sb_v, nsa_q, k_cmp, v_cmp, k_sel, v_sel, k_win, v_win, gate, g_sb, g_nsa) = parts
    k_sw = jnp.concatenate([_dup_groups(k_sel), _dup_groups(k_win)], axis=1)
    v_sw_t = jnp.concatenate([_dup_groups(v_sel), _dup_groups(v_win)], axis=1).T
    gate = jnp.pad(gate, ((0, 0), (0, GATE_PAD - gate.shape[1])))
    g_both = jnp.concatenate([g_sb, g_nsa], axis=1)
    ws = [sb_q, sb_k, sb_v, nsa_q, k_cmp, v_cmp, k_sw, v_sw_t, gate, g_both]
    dtypes = [BF16, BF16, BF16, BF16, F32, F32, BF16, BF16, F32, F32]
    transposed = [False] * 7 + [True, False, False]
    scale = HEAD_DIM ** -0.5
    scales = [scale, 1.0, 1.0, scale] + [1.0] * 6
    return [w.astype(BF16) for w in ws], dtypes, scales, transposed


def _compress_weights(w1, w2, pe):
    eye = jnp.eye(NSA_KV_GROUPS, dtype=w1.dtype)
    w1b = jnp.einsum('ldh,pg->lpdgh', w1, eye).reshape(CMP_BLOCK * NSA_KV_GROUPS * HEAD_DIM,
                                                       NSA_KV_GROUPS * CMP_HIDDEN)
    w2b = jnp.einsum('hd,pg,r->phgrd', w2, eye, jnp.ones((2,), w2.dtype)).reshape(
        NSA_KV_GROUPS * CMP_HIDDEN, NSA_KV_GROUPS * LANES)
    peb = jnp.broadcast_to(pe[:, None, :], (CMP_BLOCK, NSA_KV_GROUPS, HEAD_DIM)).reshape(1, -1)
    return peb.astype(F32), w1b.astype(BF16), w2b.astype(BF16)


def _moe_plan(idx_t, rank_t, counts, tm):
    k, t = idx_t.shape
    n = k * t
    i32 = jnp.int32
    experts = jnp.arange(N_EXPERTS, dtype=i32)
    padded = ((counts + tm - 1) // tm) * tm
    pad_end = jnp.cumsum(padded)
    pad_off = pad_end - padded
    gap = padded - counts
    pos = rank_t + jnp.sum(jnp.where(idx_t[None] == experts[:, None, None], pad_off[:, None, None], 0), axis=0)
    tok = jnp.arange(t, dtype=i32)
    ids = ((tok // ROUTER_TM) * (k * ROUTER_TM) + tok % ROUTER_TM)[None, :] + jnp.arange(k, dtype=i32)[:, None] * ROUTER_TM
    n_pad = N_EXPERTS * tm
    p_rows = n + n_pad
    id_bits = (p_rows - 1).bit_length()
    assert (N_EXPERTS + 1) << id_bits < 2 ** 31
    spare_e = jnp.repeat(experts, tm, total_repeat_length=n_pad)
    spare_used = jnp.tile(jnp.arange(tm, dtype=i32), N_EXPERTS) < jnp.repeat(gap, tm, total_repeat_length=n_pad)
    keys = jnp.concatenate([idx_t.reshape(n), jnp.where(spare_used, spare_e, N_EXPERTS)])
    ids = jnp.concatenate([ids.reshape(n), n + jnp.arange(n_pad, dtype=i32)])
    entry = lax.sort(keys * (1 << id_bits) + ids) & ((1 << id_bits) - 1)
    tok_of_entry = (entry // (k * ROUTER_TM)) * ROUTER_TM + entry % ROUTER_TM
    src_tok = jnp.where(entry < n, tok_of_entry, 0)
    pos = pos.reshape(n)
    tile_start = jnp.arange(p_rows // tm, dtype=i32) * tm
    tile_expert = jnp.minimum(jnp.sum((tile_start[:, None] >= pad_end[None, :]).astype(i32), axis=1), N_EXPERTS - 1)
    n_valid = (pad_end[-1] // tm).astype(i32).reshape(1)
    return src_tok, pos, tile_expert, n_valid


def _layer(h, p_i, w_in, w_cmp1, w_cmp2, pe_cmp, w_br_sb, w_br_nsa, w_o, ln1_g, ln1_b, ln2_g, ln2_b,
           w_router, b_router, w_gu_all, b_gu, w_down_all, b_down, w_ple_gate, w_ple_proj,
           *, batch, alpha, layer):
    t, d = h.shape
    s = t // batch
    sb_q, sb_k, sb_v, nsa_q, k_cmp, v_cmp, k_sw, v_sw_t, gate, g_both = _in_proj(h, *_in_proj_weights(w_in))
    b3 = lambda a: a.reshape(batch, s, a.shape[1])

    y_sb = _sb_attention(b3(sb_q), b3(sb_k), b3(sb_v)).reshape(t, -1)

    nc = s // CMP_BLOCK
    cmp_rows = lambda a: a.reshape(batch, nc // 2, 2 * CMP_BLOCK * a.shape[1])
    kc = _compress(cmp_rows(k_cmp), *_compress_weights(w_cmp1[0], w_cmp2[0], pe_cmp[0]))
    vc = _compress(cmp_rows(v_cmp), *_compress_weights(w_cmp1[1], w_cmp2[1], pe_cmp[1]))
    y_cmp, sel, flags = _nsa_cmp(b3(nsa_q), kc, vc, b3(gate))
    y_nsa = _nsa_main(b3(nsa_q), b3(k_sw), v_sw_t, sel, flags, b3(gate), y_cmp).reshape(t, -1)

    row = lambda v: v.reshape(1, -1).astype(F32)
    h1, h1b, h1p = _merge_ln1(y_sb, y_nsa, g_both, h, w_br_sb.astype(BF16), w_br_nsa.astype(BF16),
                              w_o.astype(BF16), row(ln1_g), row(ln1_b), alpha)

    idx_t, wt_t, rank_t, counts = _router(h1b, w_router.T.astype(BF16), b_router.reshape(-1, 1).astype(F32))
    src_tok, pos, tile_expert, n_valid = _moe_plan(idx_t, rank_t, counts[:, 0].astype(jnp.int32), EXPERT_TM)
    xs = h1p.at[src_tok].get(mode='promise_in_bounds')
    y = _experts(tile_expert, n_valid, xs, w_gu_all, b_gu.reshape(N_EXPERTS, 1, -1),
                 w_down_all, b_down.reshape(N_EXPERTS, 1, -1), layer)
    yg = y.at[pos].get(mode='promise_in_bounds').reshape(TOP_K, t, d // 2)
    return _out_ln2(h1, h1b, yg, wt_t.T, p_i, w_ple_gate.astype(BF16), w_ple_proj.astype(BF16),
                    row(ln2_g), row(ln2_b), alpha)


def kernel(x, p, w_in, w_cmp1, w_cmp2, pe_cmp, w_br_sb, w_br_nsa, w_o, ln1_g, ln1_b, ln2_g, ln2_b,
           w_router, b_router, w_gu, b_gu, w_down, b_down, w_ple_gate, w_ple_proj):
    batch, s, d = x.shape
    depth = w_in.shape[0]
    alpha = (2 * depth) ** 0.25
    h = x.reshape(batch * s, d)
    for i in range(depth):
        h = _layer(h, p[i].reshape(batch * s, -1), w_in[i], w_cmp1[i], w_cmp2[i], pe_cmp[i],
                   w_br_sb[i], w_br_nsa[i], w_o[i], ln1_g[i], ln1_b[i], ln2_g[i], ln2_b[i],
                   w_router[i], b_router[i], w_gu, b_gu[i], w_down, b_down[i],
                   w_ple_gate[i], w_ple_proj[i], batch=batch, alpha=alpha, layer=i)
    return h.reshape(batch, s, d)
```

```python
import functools

import jax
import jax.numpy as jnp
from jax import lax
from jax.experimental import pallas as pl
from jax.experimental.pallas import tpu as pltpu

F32 = jnp.float32
BF16 = jnp.bfloat16

HEAD_DIM = 64
LANES = 128
SB_HEADS = 8
NSA_HEADS = 8
NSA_KV_GROUPS = 2
NSA_GROUP = NSA_HEADS // NSA_KV_GROUPS
CMP_BLOCK = 32
CMP_HIDDEN = 128
SEL_BLOCK = 64
N_SEL = 16
WINDOW = 512
N_EXPERTS = 32
TOP_K = 4
D_FF = 1024
SWIGLU_ALPHA = 1.702
SWIGLU_LIMIT = 7.0
LN_EPS = 1e-5
NEG_INF = -1e30
FORCED_SCORE = 1e6
GATE_PAD = 128
HI_HALF = -65536
MASK_BIG = 2.0 ** 100
FEAT_HI, FEAT_LO = 64, 65
SB_DEAD_LOG = -104.0
SB_FIRST_TILES = 3

VMEM_LIMIT = 56 * 1024 * 1024

ATTN_TQ = 128
NSA_TQ = 256
SEL_TK = 512
PROJ_TM = 256
MERGE_TM = 512
ROUTER_TM = 512
EXPERT_TM = 512
OUT_TM = 512


def _cparams(sem):
    return pltpu.CompilerParams(dimension_semantics=sem, vmem_limit_bytes=VMEM_LIMIT)


def _sigmoid(x):
    return 1.0 / (1.0 + jnp.exp(-x))


def _div_pow2(x, n):
    assert n & (n - 1) == 0
    return lax.shift_right_logical(x, n.bit_length() - 1)


def _mod_pow2(x, n):
    assert n & (n - 1) == 0
    return x & (n - 1)


def _split_bf16(x):
    hi = x.astype(BF16)
    lo = (x - hi.astype(F32)).astype(BF16)
    return hi, lo


def _dot(a, b):
    return jnp.dot(a, b, preferred_element_type=F32)


def _dot_nt(a, b):
    return lax.dot_general(a, b, (((1,), (1,)), ((), ())), preferred_element_type=F32)


def _dot_tn(a, b):
    return lax.dot_general(a, b, (((0,), (0,)), ((), ())), preferred_element_type=F32)


def _pack_bf16_pairs(x):
    n = x.shape[1] // 2
    bits = lax.bitcast_convert_type(x.astype(BF16).astype(F32), jnp.int32)
    return (bits[:, n:] & HI_HALF) | lax.shift_right_logical(bits[:, :n], 16)


def _unpack_bf16_pairs(p):
    lo = lax.bitcast_convert_type(lax.shift_left(p, 16), F32)
    hi = lax.bitcast_convert_type(p & HI_HALF, F32)
    return jnp.concatenate([lo, hi], axis=1)


def _layer_norm(u, g, b):
    mu = jnp.mean(u, axis=-1, keepdims=True)
    d = u - mu
    var = jnp.mean(d * d, axis=-1, keepdims=True)
    return d * lax.rsqrt(var + LN_EPS) * g + b


def _in_proj_kernel(x_ref, *refs, scales, transposed):
    n = len(scales)
    x = x_ref[...].astype(BF16)
    for w_ref, o_ref, s, tr in zip(refs[:n], refs[n:], scales, transposed):
        acc = _dot_nt(w_ref[...], x) if tr else _dot(x, w_ref[...])
        if s != 1.0:
            acc = acc * s
        o_ref[...] = acc.astype(o_ref.dtype)


def _in_proj(h, weights, dtypes, scales, transposed):
    t, d = h.shape
    tm = PROJ_TM
    in_specs = [pl.BlockSpec((tm, d), lambda i: (i, 0))]
    in_specs += [pl.BlockSpec(w.shape, lambda i: (0, 0)) for w in weights]
    out_specs, out_shape = [], []
    for w, dt, tr in zip(weights, dtypes, transposed):
        if tr:
            out_specs.append(pl.BlockSpec((w.shape[0], tm), lambda i: (0, i)))
            out_shape.append(jax.ShapeDtypeStruct((w.shape[0], t), dt))
        else:
            out_specs.append(pl.BlockSpec((tm, w.shape[1]), lambda i: (i, 0)))
            out_shape.append(jax.ShapeDtypeStruct((t, w.shape[1]), dt))
    return pl.pallas_call(
        functools.partial(_in_proj_kernel, scales=tuple(scales), transposed=tuple(transposed)),
        grid=(t // tm,), in_specs=in_specs, out_specs=out_specs, out_shape=out_shape,
        compiler_params=_cparams(("parallel",)), name="in_proj",
    )(h, *weights)


def _stack_heads(q2, tq):
    lane = lax.broadcasted_iota(jnp.int32, (tq, LANES), 1)
    zero = jnp.zeros_like(q2)
    return jnp.concatenate([jnp.where(lane < HEAD_DIM, q2, zero),
                            jnp.where(lane >= HEAD_DIM, q2, zero)], axis=0)


def _unstack_heads(o, tq):
    lane = lax.broadcasted_iota(jnp.int32, (tq, LANES), 1)
    return jnp.where(lane < HEAD_DIM, o[:tq], o[tq:])


def _sb_kernel(q_ref, k_ref, v_ref, o_ref, *, tq, n_pairs):
    i = pl.program_id(1)
    r_i = lax.broadcasted_iota(jnp.int32, (tq, tq), 0)
    c_i = lax.broadcasted_iota(jnp.int32, (tq, tq), 1)
    tri = jnp.where(r_i > c_i, 1.0, 0.0).astype(BF16)
    rhs = jnp.concatenate([tri, jnp.ones((tq, tq), BF16)], axis=1)
    rhs = jnp.concatenate([rhs, rhs], axis=0)
    qs = [_stack_heads(q_ref[:, p * LANES:(p + 1) * LANES], tq) for p in range(n_pairs)]

    def log_stay(z):
        return -(jnp.maximum(z, 0.0) + jnp.log(1.0 + jnp.exp(-jnp.abs(z))))

    def tile_sums(ls):
        hi, lo = _split_bf16(ls)
        return _dot(jnp.concatenate([hi, lo], axis=1), rhs)

    wide = SB_FIRST_TILES * tq
    first = jnp.maximum(i - (SB_FIRST_TILES - 1), 0)
    start0 = pl.multiple_of(first * tq, tq)
    col_minus_row = (lax.broadcasted_iota(jnp.int32, (2 * tq, wide), 1)
                     - (lax.broadcasted_iota(jnp.int32, (2 * tq, wide), 0) & (tq - 1)))
    causal = col_minus_row < (i - first) * tq

    def first_step(p):
        kw = k_ref[pl.ds(start0, wide), p * LANES:(p + 1) * LANES]
        vw = v_ref[pl.ds(start0, wide), p * LANES:(p + 1) * LANES]
        z = _dot_nt(qs[p], kw)
        ls = jnp.where(causal, log_stay(z), 0.0)
        sums = [tile_sums(ls[:, j * tq:(j + 1) * tq]) for j in range(SB_FIRST_TILES)]
        later, run = [None] * SB_FIRST_TILES, None
        for j in reversed(range(SB_FIRST_TILES)):
            later[j] = sums[j][:, :tq] if run is None else sums[j][:, :tq] + run
            run = sums[j][:, tq:] if run is None else run + sums[j][:, tq:]
        w = jnp.where(causal, jnp.exp(z + ls + jnp.concatenate(later, axis=1)), 0.0)
        return run, _dot(w.astype(BF16), vw)

    def tile(p, j, carry, acc):
        start = pl.multiple_of(j * tq, tq)
        kj = k_ref[pl.ds(start, tq), p * LANES:(p + 1) * LANES]
        vj = v_ref[pl.ds(start, tq), p * LANES:(p + 1) * LANES]
        z = _dot_nt(qs[p], kj)
        ls = log_stay(z)
        sums = tile_sums(ls)
        w = jnp.exp(z + ls + sums[:, :tq] + carry)
        return carry + sums[:, tq:], acc + _dot(w.astype(BF16), vj)

    def live(carries):
        worst = functools.reduce(jnp.maximum, carries)
        return jnp.max(worst) > SB_DEAD_LOG

    state = [first_step(p) for p in range(n_pairs)]
    carries = tuple(s[0] for s in state)
    accs = tuple(s[1] for s in state)

    def cond(c):
        return jnp.logical_and(c[0] <= first, c[1])

    def body(c):
        step, _, carries, accs = c
        state = [tile(p, first - step, carries[p], accs[p]) for p in range(n_pairs)]
        carries = tuple(s[0] for s in state)
        return step + 1, live(carries), carries, tuple(s[1] for s in state)

    _, _, _, accs = lax.while_loop(cond, body, (jnp.int32(1), live(carries), carries, accs))
    for p in range(n_pairs):
        o_ref[:, p * LANES:(p + 1) * LANES] = _unstack_heads(accs[p], tq).astype(o_ref.dtype)


def _sb_attention(q, k, v):
    b, s, w = q.shape
    tq = ATTN_TQ
    return pl.pallas_call(
        functools.partial(_sb_kernel, tq=tq, n_pairs=w // LANES),
        grid=(b, s // tq),
        in_specs=[pl.BlockSpec((None, tq, w), lambda bi, i: (bi, i, 0)),
                  pl.BlockSpec((None, s, w), lambda bi, i: (bi, 0, 0)),
                  pl.BlockSpec((None, s, w), lambda bi, i: (bi, 0, 0))],
        out_specs=pl.BlockSpec((None, tq, w), lambda bi, i: (bi, i, 0)),
        out_shape=jax.ShapeDtypeStruct((b, s, w), BF16),
        compiler_params=_cparams(("parallel", "arbitrary")), name="sb_attn",
    )(q, k, v)


def _compress_kernel(x_ref, pe_ref, w1_ref, w2_ref, o_ref):
    x = (x_ref[...] + pe_ref[...]).astype(BF16)
    hid = _dot(x, w1_ref[...])
    hid = hid * _sigmoid(hid)
    o_ref[...] = _dot(hid.astype(BF16), w2_ref[...]).astype(o_ref.dtype)


def _compress(x, pe, w1, w2):
    b, half, width = x.shape
    blk = width // 2
    out = pl.pallas_call(
        _compress_kernel,
        grid=(b, 2),
        in_specs=[pl.BlockSpec((None, half, blk), lambda bi, par: (bi, 0, par)),
                  pl.BlockSpec(pe.shape, lambda bi, par: (0, 0)),
                  pl.BlockSpec(w1.shape, lambda bi, par: (0, 0)),
                  pl.BlockSpec(w2.shape, lambda bi, par: (0, 0))],
        out_specs=pl.BlockSpec((None, None, half, w2.shape[1]), lambda bi, par: (bi, par, 0, 0)),
        out_shape=jax.ShapeDtypeStruct((b, 2, half, w2.shape[1]), BF16),
        compiler_params=_cparams(("parallel", "parallel")), name="compress",
    )(x, pe, w1, w2)
    return out.reshape(b, 2 * half, w2.shape[1])


def _stack_group(q, tq):
    return jnp.concatenate([_stack_heads(q[:, :LANES], tq), _stack_heads(q[:, LANES:], tq)], axis=0)


def _unstack_group(o, tq):
    return jnp.concatenate([_unstack_heads(o[:2 * tq], tq), _unstack_heads(o[2 * tq:], tq)], axis=1)


def _alibi_slope(head_idx):
    return lax.bitcast_convert_type(lax.shift_left(126 - head_idx, 23), F32)


def _expand_gate(gate_sig, g, branch):
    c_i = lax.broadcasted_iota(jnp.int32, (GATE_PAD, 2 * LANES), 0)
    l_i = lax.broadcasted_iota(jnp.int32, (GATE_PAD, 2 * LANES), 1)
    col = g * (3 * NSA_GROUP) + _div_pow2(l_i, HEAD_DIM) * 3 + branch
    onehot = jnp.where(c_i == col, 1.0, 0.0).astype(BF16)
    hi, lo = _split_bf16(gate_sig)
    return _dot(hi, onehot) + _dot(lo, onehot)


def _nsa_cmp_kernel(q_ref, kc_ref, vc_ref, gate_ref, y_ref, sel_ref, flag_ref, *, tq, n_sel):
    g = pl.program_id(1)
    i = pl.program_id(2)
    nc = kc_ref.shape[0]
    half = nc // 2
    qst = _stack_group(q_ref[...], tq)
    s = _dot_nt(kc_ref[...], qst)
    r_i = lax.broadcasted_iota(jnp.int32, (nc, 4 * tq), 0)
    l_i = lax.broadcasted_iota(jnp.int32, (nc, 4 * tq), 1)
    cblk = 2 * _mod_pow2(r_i, half) + _div_pow2(r_i, half)
    c_end = cblk * CMP_BLOCK + (CMP_BLOCK - 1)
    t = i * tq + _mod_pow2(l_i, tq)
    dist = t - c_end
    slope = _alibi_slope(g * NSA_GROUP + _div_pow2(l_i, tq))
    s = s - slope * dist.astype(F32)
    mask = dist >= 0
    m = jnp.max(jnp.where(mask, s, NEG_INF), axis=0, keepdims=True)
    p = jnp.where(mask, jnp.exp(s - m), 0.0)
    l = jnp.sum(p, axis=0, keepdims=True)
    p = p * jnp.where(l > 0.0, 1.0 / l, 0.0)

    o = _dot_tn(p.astype(BF16), vc_ref[...])
    gate = _expand_gate(_sigmoid(gate_ref[...]), g, 0)
    y_ref[...] = (_unstack_group(o, tq) * gate).astype(y_ref.dtype)

    imp_c = p[:, 0:tq] + p[:, tq:2 * tq] + p[:, 2 * tq:3 * tq] + p[:, 3 * tq:4 * tq]
    imp = imp_c[:half] + imp_c[half:]
    n_blk = half
    blk = lax.broadcasted_iota(jnp.int32, (n_blk, tq), 0)
    tt = i * tq + lax.broadcasted_iota(jnp.int32, (n_blk, tq), 1)
    cur = _div_pow2(tt, SEL_BLOCK)
    score = jnp.where(blk * SEL_BLOCK <= tt, imp, NEG_INF)
    for forced_blk in (0, cur, cur - 1):
        score = jnp.where(blk == forced_blk, FORCED_SCORE, score)

    taken = -jnp.inf
    sel = jnp.zeros((n_blk, tq), F32)
    for _ in range(n_sel):
        mx = jnp.max(score, axis=0, keepdims=True)
        first = jnp.min(jnp.where(score == mx, blk, n_blk), axis=0, keepdims=True)
        hit = blk == first
        sel = jnp.where(hit, 1.0, sel)
        score = jnp.where(hit, taken, score)
    sel = jnp.concatenate([sel, jnp.zeros((LANES - n_blk, tq), F32)], axis=0)
    sel_ref[...] = sel.T
    picks = _dot_nt(jnp.ones((8, tq), BF16), sel.astype(BF16))
    flag_ref[...] = picks[0:1].astype(jnp.int32)


def _nsa_cmp(q, kc, vc, gate):
    b, s, _ = q.shape
    tq = NSA_TQ
    nc = kc.shape[1]
    n_blk = nc // 2
    assert n_blk <= LANES
    n_sel = min(N_SEL, n_blk)
    return pl.pallas_call(
        functools.partial(_nsa_cmp_kernel, tq=tq, n_sel=n_sel),
        grid=(b, NSA_KV_GROUPS, s // tq),
        in_specs=[pl.BlockSpec((None, tq, 2 * LANES), lambda bi, g, i: (bi, i, g)),
                  pl.BlockSpec((None, nc, LANES), lambda bi, g, i: (bi, 0, g)),
                  pl.BlockSpec((None, nc, LANES), lambda bi, g, i: (bi, 0, g)),
                  pl.BlockSpec((None, tq, GATE_PAD), lambda bi, g, i: (bi, i, 0))],
        out_specs=[pl.BlockSpec((None, tq, 2 * LANES), lambda bi, g, i: (bi, i, g)),
                   pl.BlockSpec((None, None, tq, LANES), lambda bi, g, i: (bi, g, i, 0)),
                   pl.BlockSpec((None, None, None, 1, LANES), lambda bi, g, i: (bi, g, i, 0, 0))],
        out_shape=[jax.ShapeDtypeStruct((b, s, 4 * LANES), F32),
                   jax.ShapeDtypeStruct((b, NSA_KV_GROUPS, s, LANES), F32),
                   jax.ShapeDtypeStruct((b, NSA_KV_GROUPS, s // tq, 1, LANES), jnp.int32)],
        compiler_params=_cparams(("parallel", "parallel", "parallel")), name="nsa_cmp",
    )(q, kc, vc, gate)


def _key_features(s, tk, wk):
    def table(n, period, with_blocks):
        key = lax.broadcasted_iota(jnp.int32, (n, LANES), 0)
        lane = lax.broadcasted_iota(jnp.int32, (n, LANES), 1)
        r = key % period
        pos = jnp.where(lane == FEAT_HI, r // SEL_BLOCK, jnp.where(lane == FEAT_LO, r % SEL_BLOCK, 0))
        if with_blocks:
            pos = jnp.where(lane < SEL_BLOCK, (key // SEL_BLOCK == lane).astype(jnp.int32), pos)
        return pos.astype(BF16)
    return table(s, tk, True), table(wk, wk, False)


def _nsa_main_kernel(flag_ref, q_ref, ks_ref, kw_ref, vs_ref, vw_ref, kfeat_ref, rfeat_ref, sel_ref,
                     gate_ref, ycmp_ref, o_ref, m_ref, l_ref, acc_ref, *, tq, tk):
    g = pl.program_id(1)
    i = pl.program_id(2)
    cols = NSA_GROUP * tq
    qs = pl.multiple_of(i * tq, tq)
    qst = _stack_group(q_ref[...], tq)
    col = lax.broadcasted_iota(jnp.int32, (1, cols), 1)
    t_loc = _mod_pow2(col, tq)
    slope_row = _alibi_slope(g * NSA_GROUP + _div_pow2(col, tq))

    lane = lax.broadcasted_iota(jnp.int32, (tq, LANES), 1)
    sel = sel_ref[...]
    first_blk = i * (tq // SEL_BLOCK)
    add_diag = jnp.where(sel > 0.5, 0.0, -MASK_BIG)
    add_bulk = jnp.where(lane < first_blk, add_diag, -MASK_BIG)
    f_bulk, f_diag, f_win = [], [], []
    for h in range(NSA_GROUP):
        slope = _alibi_slope(jnp.full((tq, LANES), g * NSA_GROUP + h, jnp.int32))
        pos = jnp.where(lane == FEAT_HI, slope * SEL_BLOCK, jnp.where(lane == FEAT_LO, slope, 0.0))
        f_win.append(pos)
        f_bulk.append(jnp.where(lane < SEL_BLOCK, add_bulk, pos))
        f_diag.append(jnp.where(lane < SEL_BLOCK, add_diag, pos))
    aug = lambda f: jnp.concatenate([qst, jnp.concatenate(f, axis=0).astype(BF16)], axis=1)
    q_sel, q_diag, q_win = aug(f_bulk), aug(f_diag), aug(f_win)

    def online(k, q_aug, mask_add, bias, v_t):
        s = _dot_nt(k, q_aug)
        if mask_add is not None:
            s = s + mask_add
        m = m_ref[...]
        m_new = jnp.maximum(m, jnp.max(s, axis=0, keepdims=True) + bias)
        alpha = jnp.exp(m - m_new)
        p = jnp.exp(s + (bias - m_new))
        m_ref[...] = m_new
        l_ref[...] = alpha * l_ref[...] + jnp.sum(p, axis=0, keepdims=True)
        acc_ref[...] = alpha * acc_ref[...] + _dot(v_t, p.astype(BF16))

    def untranspose(o_t):
        return jnp.concatenate([o_t[:, h * tq:(h + 1) * tq].T for h in range(NSA_GROUP)], axis=0)

    r_i = lax.broadcasted_iota(jnp.int32, (tq, tq), 0)
    c_i = lax.broadcasted_iota(jnp.int32, (tq, tq), 1)
    causal_add = jnp.where(r_i <= c_i, 0.0, -MASK_BIG)
    causal_add = jnp.concatenate([causal_add] * NSA_GROUP, axis=1)
    m_ref[...] = jnp.full((1, cols), NEG_INF, F32)
    l_ref[...] = jnp.zeros((1, cols), F32)
    acc_ref[...] = jnp.zeros((LANES, cols), F32)
    k = jnp.concatenate([ks_ref[pl.ds(qs, tq), :], kfeat_ref[pl.ds(qs, tq), :]], axis=1)
    bias = -slope_row * ((qs & (tk - 1)) + t_loc).astype(F32)
    online(k, q_diag, causal_add, bias, vs_ref[:, pl.ds(qs, tq)])

    def bulk(kt, _):
        start = pl.multiple_of(kt * tk, tk)
        picked = 0
        for sub in range(flag_ref.shape[0]):
            for j in range(tk // SEL_BLOCK):
                picked = picked + flag_ref[sub, 0, kt * (tk // SEL_BLOCK) + j]

        @pl.when(picked > 0)
        def _():
            k = jnp.concatenate([ks_ref[pl.ds(start, tk), :], kfeat_ref[pl.ds(start, tk), :]], axis=1)
            bias = -slope_row * (qs - start + t_loc).astype(F32)
            online(k, q_sel, None, bias, vs_ref[:, pl.ds(start, tk)])

        return 0

    lax.fori_loop(0, (qs + tk - 1) // tk, bulk, 0)
    o_sel = untranspose(acc_ref[...] * (1.0 / l_ref[...]))

    wk = WINDOW + tq
    start = pl.multiple_of(jnp.maximum(qs - WINDOW, 0), tq)
    w_r = lax.broadcasted_iota(jnp.int32, (wk, tq), 0)
    w_c = lax.broadcasted_iota(jnp.int32, (wk, tq), 1)
    dist = (qs - start) + w_c - w_r
    band_add = jnp.where(jnp.where(dist >= 0, dist, WINDOW) < WINDOW, 0.0, -MASK_BIG)
    band_add = jnp.concatenate([band_add] * NSA_GROUP, axis=1)
    k = jnp.concatenate([kw_ref[pl.ds(start, wk), :], rfeat_ref[...]], axis=1)
    s = _dot_nt(k, q_win) + band_add
    p = jnp.exp(s - jnp.max(s, axis=0, keepdims=True))
    l_w = jnp.sum(p, axis=0, keepdims=True)
    o_win = untranspose(_dot(vw_ref[:, pl.ds(start, wk)], p.astype(BF16)) * (1.0 / l_w))

    gate = _sigmoid(gate_ref[...])
    y = (ycmp_ref[...] + _expand_gate(gate, g, 1) * _unstack_group(o_sel, tq)
         + _expand_gate(gate, g, 2) * _unstack_group(o_win, tq))
    o_ref[...] = y.astype(o_ref.dtype)


def _nsa_main(q, k_sw, v_sw_t, sel, flags, gate, ycmp):
    b, s, _ = q.shape
    tq = NSA_TQ
    tk = min(SEL_TK, s)
    wk = WINDOW + tq
    cols = NSA_GROUP * tq
    flag_rows = 1
    assert s >= wk and s % tk == 0 and tk % tq == 0 and s // SEL_BLOCK <= SEL_BLOCK
    kfeat, rfeat = _key_features(s, tk, wk)
    k_spec = lambda off: pl.BlockSpec((None, s, LANES), lambda bi, g, i: (bi, 0, off + g))
    v_spec = lambda off: pl.BlockSpec((LANES, s), lambda bi, g, i: (off + g, bi))
    return pl.pallas_call(
        functools.partial(_nsa_main_kernel, tq=tq, tk=tk),
        grid=(b, NSA_KV_GROUPS, s // tq),
        in_specs=[pl.BlockSpec((None, None, flag_rows, 1, LANES), lambda bi, g, i: (bi, g, i, 0, 0),
                               memory_space=pltpu.SMEM),
                  pl.BlockSpec((None, tq, 2 * LANES), lambda bi, g, i: (bi, i, g)),
                  k_spec(0), k_spec(2), v_spec(0), v_spec(2),
                  pl.BlockSpec((s, LANES), lambda bi, g, i: (0, 0)),
                  pl.BlockSpec((wk, LANES), lambda bi, g, i: (0, 0)),
                  pl.BlockSpec((None, None, tq, LANES), lambda bi, g, i: (bi, g, i, 0)),
                  pl.BlockSpec((None, tq, GATE_PAD), lambda bi, g, i: (bi, i, 0)),
                  pl.BlockSpec((None, tq, 2 * LANES), lambda bi, g, i: (bi, i, g))],
        out_specs=pl.BlockSpec((None, tq, 2 * LANES), lambda bi, g, i: (bi, i, g)),
        out_shape=jax.ShapeDtypeStruct((b, s, 4 * LANES), BF16),
        scratch_shapes=[pltpu.VMEM((1, cols), F32), pltpu.VMEM((1, cols), F32), pltpu.VMEM((LANES, cols), F32)],
        compiler_params=_cparams(("parallel", "parallel", "arbitrary")), name="nsa_main",
    )(flags, q, k_sw, k_sw, v_sw_t, v_sw_t, kfeat, rfeat, sel, gate, ycmp)


def _merge_kernel(ysb_ref, ynsa_ref, g_ref, h_ref, wsb_ref, wnsa_ref, wo_ref, lng_ref, lnb_ref,
                  h1_ref, h1b_ref, h1p_ref, *, alpha):
    d = h_ref.shape[1]
    g = g_ref[...]
    merged = (_sigmoid(g[:, :d]) * _dot(ysb_ref[...], wsb_ref[...])
              + _sigmoid(g[:, d:]) * _dot(ynsa_ref[...], wnsa_ref[...]))
    u = alpha * h_ref[...] + _dot(merged.astype(BF16), wo_ref[...])
    h1 = _layer_norm(u, lng_ref[...], lnb_ref[...])
    h1_ref[...] = h1
    h1b_ref[...] = h1.astype(BF16)
    h1p_ref[...] = _pack_bf16_pairs(h1)


def _merge_ln1(ysb, ynsa, g, h, wsb, wnsa, wo, lng, lnb, alpha):
    t, d = h.shape
    tm = MERGE_TM
    row = lambda w: pl.BlockSpec((tm, w), lambda i: (i, 0))
    full = lambda a: pl.BlockSpec(a.shape, lambda i: (0, 0))
    return pl.pallas_call(
        functools.partial(_merge_kernel, alpha=alpha),
        grid=(t // tm,),
        in_specs=[row(ysb.shape[1]), row(ynsa.shape[1]), row(2 * d), row(d),
                  full(wsb), full(wnsa), full(wo), full(lng), full(lnb)],
        out_specs=[row(d), row(d), row(d // 2)],
        out_shape=[jax.ShapeDtypeStruct((t, d), F32), jax.ShapeDtypeStruct((t, d), BF16),
                   jax.ShapeDtypeStruct((t, d // 2), jnp.int32)],
        compiler_params=_cparams(("parallel",)), name="merge_ln1",
    )(ysb, ynsa, g, h, wsb, wnsa, wo, lng, lnb)


def _router_kernel(x_ref, w_ref, b_ref, idx_ref, wt_ref, rank_ref, cnt_ref):
    logits = _dot_nt(w_ref[...], x_ref[...]) + b_ref[...]
    n_e, tm = logits.shape
    e_i = lax.broadcasted_iota(jnp.int32, (n_e, tm), 0)
    vals, idxs = [], []
    for _ in range(TOP_K):
        mx = jnp.max(logits, axis=0, keepdims=True)
        first = jnp.min(jnp.where(logits == mx, e_i, n_e), axis=0, keepdims=True)
        vals.append(mx)
        idxs.append(first)
        logits = jnp.where(e_i == first, -jnp.inf, logits)
    ex = [jnp.exp(v - vals[0]) for v in vals]
    inv = 1.0 / (ex[0] + ex[1] + ex[2] + ex[3])
    idx_ref[...] = jnp.concatenate(idxs, axis=0)
    wt_ref[...] = jnp.concatenate([e * inv for e in ex], axis=0)

    @pl.when(pl.program_id(0) == 0)
    def _():
        cnt_ref[...] = jnp.zeros_like(cnt_ref)

    r_i = lax.broadcasted_iota(jnp.int32, (tm, tm), 0)
    c_i = lax.broadcasted_iota(jnp.int32, (tm, tm), 1)
    before = jnp.where(r_i < c_i, 1.0, 0.0).astype(BF16)
    ones = jnp.ones((tm, LANES), BF16)
    seen = cnt_ref[...][:, 0:1]
    ranks = []
    for first in idxs:
        hot = jnp.where(e_i == first, 1.0, 0.0)
        hot_bf = hot.astype(BF16)
        ranks.append(jnp.sum(hot * (seen + _dot(hot_bf, before)), axis=0, keepdims=True))
        seen = seen + _dot(hot_bf, ones)[:, 0:1]
    rank_ref[...] = jnp.concatenate(ranks, axis=0).astype(jnp.int32)
    cnt_ref[...] = jnp.broadcast_to(seen, cnt_ref.shape)


def _router(h1b, w_t, b_col):
    t, d = h1b.shape
    tm = ROUTER_TM
    n_e = w_t.shape[0]
    return pl.pallas_call(
        _router_kernel,
        grid=(t // tm,),
        in_specs=[pl.BlockSpec((tm, d), lambda i: (i, 0)),
                  pl.BlockSpec(w_t.shape, lambda i: (0, 0)),
                  pl.BlockSpec(b_col.shape, lambda i: (0, 0))],
        out_specs=[pl.BlockSpec((TOP_K, tm), lambda i: (0, i)),
                   pl.BlockSpec((TOP_K, tm), lambda i: (0, i)),
                   pl.BlockSpec((TOP_K, tm), lambda i: (0, i)),
                   pl.BlockSpec((n_e, LANES), lambda i: (0, 0))],
        out_shape=[jax.ShapeDtypeStruct((TOP_K, t), jnp.int32),
                   jax.ShapeDtypeStruct((TOP_K, t), F32),
                   jax.ShapeDtypeStruct((TOP_K, t), jnp.int32),
                   jax.ShapeDtypeStruct((n_e, LANES), F32)],
        compiler_params=_cparams(("arbitrary",)), name="router",
    )(h1b, w_t, b_col)


def _expert_kernel(te_ref, nv_ref, x_ref, wgu_ref, bgu_ref, wd_ref, bd_ref, y_ref, wgu_bf, wd_bf):
    i = pl.program_id(0)

    @pl.when(jnp.logical_or(i == 0, te_ref[i] != te_ref[jnp.maximum(i - 1, 0)]))
    def _():
        wgu_bf[...] = wgu_ref[...].astype(BF16)
        wd_bf[...] = wd_ref[...].astype(BF16)

    @pl.when(i < nv_ref[0])
    def _():
        gu = _dot(_unpack_bf16_pairs(x_ref[...]).astype(BF16), wgu_bf[...]) + bgu_ref[...]
        gate = jnp.minimum(gu[:, :D_FF], SWIGLU_LIMIT)
        up = jnp.clip(gu[:, D_FF:], -SWIGLU_LIMIT, SWIGLU_LIMIT)
        act = (up + 1.0) * gate * _sigmoid(SWIGLU_ALPHA * gate)
        y_ref[...] = _pack_bf16_pairs(_dot(act.astype(BF16), wd_bf[...]) + bd_ref[...])

    @pl.when(i >= nv_ref[0])
    def _():
        y_ref[...] = jnp.zeros_like(y_ref)


def _experts(tile_expert, n_valid, xs, wgu, bgu, wd, bd, layer):
    p = xs.shape[0]
    d = wgu.shape[-2]
    tm = EXPERT_TM
    two_ff = wgu.shape[-1]
    grid_spec = pltpu.PrefetchScalarGridSpec(
        num_scalar_prefetch=2, grid=(p // tm,),
        in_specs=[pl.BlockSpec((tm, d // 2), lambda i, te, nv: (i, 0)),
                  pl.BlockSpec((None, None, d, two_ff), lambda i, te, nv: (layer, te[i], 0, 0)),
                  pl.BlockSpec((None, 1, two_ff), lambda i, te, nv: (te[i], 0, 0)),
                  pl.BlockSpec((None, None, two_ff // 2, d), lambda i, te, nv: (layer, te[i], 0, 0)),
                  pl.BlockSpec((None, 1, d), lambda i, te, nv: (te[i], 0, 0))],
        out_specs=pl.BlockSpec((tm, d // 2), lambda i, te, nv: (i, 0)),
        scratch_shapes=[pltpu.VMEM((d, two_ff), BF16), pltpu.VMEM((two_ff // 2, d), BF16)])
    return pl.pallas_call(
        _expert_kernel, grid_spec=grid_spec,
        out_shape=jax.ShapeDtypeStruct((p, d // 2), jnp.int32),
        compiler_params=_cparams(("arbitrary",)), name="experts",
    )(tile_expert, n_valid, xs, wgu, bgu, wd, bd)


def _out_kernel(h1_ref, h1b_ref, yg_ref, cw_ref, p_ref, wpg_ref, wpp_ref, lng_ref, lnb_ref, o_ref, *, alpha):
    cw = cw_ref[...]
    moe = cw[:, 0:1] * _unpack_bf16_pairs(yg_ref[0])
    for k in range(1, TOP_K):
        moe = moe + cw[:, k:k + 1] * _unpack_bf16_pairs(yg_ref[k])
    ple = _sigmoid(_dot(h1b_ref[...], wpg_ref[...])) * _dot(p_ref[...].astype(BF16), wpp_ref[...])
    u = alpha * h1_ref[...] + moe + ple
    o_ref[...] = _layer_norm(u, lng_ref[...], lnb_ref[...])


def _out_ln2(h1, h1b, yg, cw, p_all, layer, wpg, wpp, lng, lnb, alpha):
    t, d = h1.shape
    tm = OUT_TM
    row = lambda w: pl.BlockSpec((tm, w), lambda i: (i, 0))
    full = lambda a: pl.BlockSpec(a.shape, lambda i: (0, 0))
    return pl.pallas_call(
        functools.partial(_out_kernel, alpha=alpha),
        grid=(t // tm,),
        in_specs=[row(d), row(d), pl.BlockSpec((TOP_K, tm, d // 2), lambda i: (0, i, 0)), row(TOP_K),
                  pl.BlockSpec((None, tm, p_all.shape[2]), lambda i: (layer, i, 0)),
                  full(wpg), full(wpp), full(lng), full(lnb)],
        out_specs=row(d),
        out_shape=jax.ShapeDtypeStruct((t, d), F32),
        compiler_params=_cparams(("parallel",)), name="out_ln2",
    )(h1, h1b, yg, cw, p_all, wpg, wpp, lng, lnb)


def _dup_groups(w):
    d = w.shape[0]
    w = w.reshape(d, NSA_KV_GROUPS, 1, HEAD_DIM)
    return jnp.broadcast_to(w, (d, NSA_KV_GROUPS, 2, HEAD_DIM)).reshape(d, NSA_KV_GROUPS * LANES)


def _in_proj_weights(w_in):
    sbw = SB_HEADS * HEAD_DIM
    qw = NSA_HEADS * HEAD_DIM
    kvw = NSA_KV_GROUPS * HEAD_DIM
    d = w_in.shape[0]
    widths = (sbw, sbw, sbw, qw) + (kvw,) * 6 + (3 * NSA_HEADS, d, d)
    parts, start = [], 0
    for wd in widths:
        parts.append(w_in[:, start:start + wd])
        start += wd
    (sb_q, sb_k, sb_v, nsa_q, k_cmp, v_cmp, k_sel, v_sel, k_win, v_win, gate, g_sb, g_nsa) = parts
    k_sw = jnp.concatenate([_dup_groups(k_sel), _dup_groups(k_win)], axis=1)
    v_sw_t = jnp.concatenate([_dup_groups(v_sel), _dup_groups(v_win)], axis=1).T
    gate = jnp.pad(gate, ((0, 0), (0, GATE_PAD - gate.shape[1])))
    g_both = jnp.concatenate([g_sb, g_nsa], axis=1)
    ws = [sb_q, sb_k, sb_v, nsa_q, k_cmp, v_cmp, k_sw, v_sw_t, gate, g_both]
    dtypes = [BF16, BF16, BF16, BF16, F32, F32, BF16, BF16, F32, F32]
    transposed = [False] * 7 + [True, False, False]
    scale = HEAD_DIM ** -0.5
    scales = [scale, 1.0, 1.0, scale] + [1.0] * 6
    return [w.astype(BF16) for w in ws], dtypes, scales, transposed


def _compress_weights(w1, w2, pe):
    eye = jnp.eye(NSA_KV_GROUPS, dtype=w1.dtype)
    w1b = jnp.einsum('ldh,pg->lpdgh', w1, eye).reshape(CMP_BLOCK * NSA_KV_GROUPS * HEAD_DIM,
                                                       NSA_KV_GROUPS * CMP_HIDDEN)
    w2b = jnp.einsum('hd,pg,r->phgrd', w2, eye, jnp.ones((2,), w2.dtype)).reshape(
        NSA_KV_GROUPS * CMP_HIDDEN, NSA_KV_GROUPS * LANES)
    peb = jnp.broadcast_to(pe[:, None, :], (CMP_BLOCK, NSA_KV_GROUPS, HEAD_DIM)).reshape(1, -1)
    return peb.astype(F32), w1b.astype(BF16), w2b.astype(BF16)


def _moe_plan(idx_t, rank_t, counts, tm):
    k, t = idx_t.shape
    n = k * t
    i32 = jnp.int32
    experts = jnp.arange(N_EXPERTS, dtype=i32)
    padded = ((counts + tm - 1) // tm) * tm
    pad_end = jnp.cumsum(padded)
    pad_off = pad_end - padded
    gap = padded - counts
    pos = rank_t + jnp.sum(jnp.where(idx_t[None] == experts[:, None, None], pad_off[:, None, None], 0), axis=0)
    tok = jnp.arange(t, dtype=i32)
    ids = ((tok // ROUTER_TM) * (k * ROUTER_TM) + tok % ROUTER_TM)[None, :] + jnp.arange(k, dtype=i32)[:, None] * ROUTER_TM
    n_pad = N_EXPERTS * tm
    p_rows = n + n_pad
    id_bits = (p_rows - 1).bit_length()
    assert (N_EXPERTS + 1) << id_bits < 2 ** 31
    spare_e = jnp.repeat(experts, tm, total_repeat_length=n_pad)
    spare_used = jnp.tile(jnp.arange(tm, dtype=i32), N_EXPERTS) < jnp.repeat(gap, tm, total_repeat_length=n_pad)
    keys = jnp.concatenate([idx_t.reshape(n), jnp.where(spare_used, spare_e, N_EXPERTS)])
    ids = jnp.concatenate([ids.reshape(n), n + jnp.arange(n_pad, dtype=i32)])
    entry = lax.sort(keys * (1 << id_bits) + ids) & ((1 << id_bits) - 1)
    tok_of_entry = (entry // (k * ROUTER_TM)) * ROUTER_TM + entry % ROUTER_TM
    src_tok = jnp.where(entry < n, tok_of_entry, 0)
    pos = pos.reshape(n)
    tile_start = jnp.arange(p_rows // tm, dtype=i32) * tm
    tile_expert = jnp.minimum(jnp.sum((tile_start[:, None] >= pad_end[None, :]).astype(i32), axis=1), N_EXPERTS - 1)
    n_valid = (pad_end[-1] // tm).astype(i32).reshape(1)
    return src_tok, pos, tile_expert, n_valid


def _layer(h, p_all, w_in, w_cmp1, w_cmp2, pe_cmp, w_br_sb, w_br_nsa, w_o, ln1_g, ln1_b, ln2_g, ln2_b,
           w_router, b_router, w_gu_all, b_gu, w_down_all, b_down, w_ple_gate, w_ple_proj,
           *, batch, alpha, layer):
    t, d = h.shape
    s = t // batch
    sb_q, sb_k, sb_v, nsa_q, k_cmp, v_cmp, k_sw, v_sw_t, gate, g_both = _in_proj(h, *_in_proj_weights(w_in))
    b3 = lambda a: a.reshape(batch, s, a.shape[1])

    y_sb = _sb_attention(b3(sb_q), b3(sb_k), b3(sb_v)).reshape(t, -1)

    nc = s // CMP_BLOCK
    cmp_rows = lambda a: a.reshape(batch, nc // 2, 2 * CMP_BLOCK * a.shape[1])
    kc = _compress(cmp_rows(k_cmp), *_compress_weights(w_cmp1[0], w_cmp2[0], pe_cmp[0]))
    vc = _compress(cmp_rows(v_cmp), *_compress_weights(w_cmp1[1], w_cmp2[1], pe_cmp[1]))
    y_cmp, sel, flags = _nsa_cmp(b3(nsa_q), kc, vc, b3(gate))
    y_nsa = _nsa_main(b3(nsa_q), b3(k_sw), v_sw_t, sel, flags, b3(gate), y_cmp).reshape(t, -1)

    row = lambda v: v.reshape(1, -1).astype(F32)
    h1, h1b, h1p = _merge_ln1(y_sb, y_nsa, g_both, h, w_br_sb.astype(BF16), w_br_nsa.astype(BF16),
                              w_o.astype(BF16), row(ln1_g), row(ln1_b), alpha)

    idx_t, wt_t, rank_t, counts = _router(h1b, w_router.T.astype(BF16), b_router.reshape(-1, 1).astype(F32))
    src_tok, pos, tile_expert, n_valid = _moe_plan(idx_t, rank_t, counts[:, 0].astype(jnp.int32), EXPERT_TM)
    xs = h1p.at[src_tok].get(mode='promise_in_bounds')
    y = _experts(tile_expert, n_valid, xs, w_gu_all, b_gu.reshape(N_EXPERTS, 1, -1),
                 w_down_all, b_down.reshape(N_EXPERTS, 1, -1), layer)
    yg = y.at[pos].get(mode='promise_in_bounds').reshape(TOP_K, t, d // 2)
    return _out_ln2(h1, h1b, yg, wt_t.T, p_all, layer, w_ple_gate.astype(BF16), w_ple_proj.astype(BF16),
                    row(ln2_g), row(ln2_b), alpha)


def kernel(x, p, w_in, w_cmp1, w_cmp2, pe_cmp, w_br_sb, w_br_nsa, w_o, ln1_g, ln1_b, ln2_g, ln2_b,
           w_router, b_router, w_gu, b_gu, w_down, b_down, w_ple_gate, w_ple_proj):
    batch, s, d = x.shape
    depth = w_in.shape[0]
    alpha = (2 * depth) ** 0.25
    h = x.reshape(batch * s, d)
    for i in range(depth):
        h = _layer(h, p.reshape(depth, batch * s, -1), w_in[i], w_cmp1[i], w_cmp2[i], pe_cmp[i],
                   w_br_sb[i], w_br_nsa[i], w_o[i], ln1_g[i], ln1_b[i], ln2_g[i], ln2_b[i],
                   w_router[i], b_router[i], w_gu, b_gu[i], w_down, b_down[i],
                   w_ple_gate[i], w_ple_proj[i], batch=batch, alpha=alpha, layer=i)
    return h.reshape(batch, s, d)
```
